```python
import jax, jax.numpy as jnp
from jax import lax
import numpy as np

D_MODEL = 1024
BATCH = 8
SEQ = 2048
DEPTH = 1
DEC_BATCH = 128
DEC_SEQ = 1
PAST_LEN = 16384
PAGE_SIZE = 128

N_META = 16
POOL_WIDTH = D_MODEL // 2
POOL_WINDOWS = (2, 4, 8, 16)
POOL_GROUPS = len(POOL_WINDOWS)
POOL_GROUP = POOL_WIDTH // POOL_GROUPS
POOL_MAXW = max(POOL_WINDOWS)
POOL_BUF = POOL_MAXW - 1
HG_HEADS = 4
HG_WIDTH = D_MODEL // 2
HG_DK = 128
HG_DV = HG_WIDTH // HG_HEADS
HG_KTOT = HG_HEADS * HG_DK
HG_CHUNK = 64
D_FF = 4 * D_MODEL
EPS = 1e-6
SPLITS = (POOL_WIDTH, HG_KTOT, HG_KTOT, HG_WIDTH, HG_WIDTH, D_MODEL, D_MODEL)
N_IN = sum(SPLITS)

kernel_name = 'hybrid_pool_hgrn2_meta_decoder_step'


def _rmsnorm(x, g):
    xf = x.astype(jnp.float32)
    y = xf * lax.rsqrt(jnp.mean(xf * xf, axis=-1, keepdims=True) + EPS) * g.astype(jnp.float32)
    return y.astype(x.dtype)


def _pool_mix(u, pos):
    L = u.shape[1]
    cs = jnp.pad(jnp.cumsum(u, axis=1), ((0, 0), (POOL_MAXW, 0), (0, 0)))
    outs = []
    for gi, w in enumerate(POOL_WINDOWS):
        sl = slice(gi * POOL_GROUP, (gi + 1) * POOL_GROUP)
        wsum = cs[:, POOL_MAXW:, sl] - cs[:, POOL_MAXW - w:POOL_MAXW - w + L, sl]
        cnt = jnp.minimum(pos + 1, w).astype(jnp.float32)
        outs.append(wsum / cnt[None, :, None])
    return jnp.concatenate(outs, axis=-1) - u


def _hgrn_chunk(S0, q, k, v, g):
    C = q.shape[1]
    G = jnp.cumsum(g, axis=1)
    causal = jnp.tril(jnp.ones((C, C), dtype=bool))[None, :, :, None, None]
    diff = G[:, :, None] - G[:, None, :]
    decay = jnp.exp(jnp.where(causal, diff, -jnp.inf))
    A = jnp.einsum('bthk,btshk,bshk->bhts', q, decay, k)
    o = jnp.einsum('bhts,bshv->bthv', A, v) + jnp.einsum('bthk,bhkv->bthv', q * jnp.exp(G), S0)
    G_last = G[:, -1]
    S = jnp.exp(G_last)[..., None] * S0 + jnp.einsum('bshk,bshv->bhkv', k * jnp.exp(G_last[:, None] - G), v)
    return S, o


def _hgrn_scan(S0, q, k, v, g):
    B, L = q.shape[:2]
    n = L // HG_CHUNK

    def blk(a):
        return a.reshape(B, n, HG_CHUNK, *a.shape[2:]).swapaxes(0, 1)

    def step(S, inp):
        return _hgrn_chunk(S, *inp)

    S, o = lax.scan(step, S0, (blk(q), blk(k), blk(v), blk(g)))
    return S, o.swapaxes(0, 1).reshape(B, L, *o.shape[3:])


def _layer(x, pool_prev, S0, start_pos, prompt, lb, g_mix, w_in, w_pool, pool_scale,
           g_onorm, w_a, w_b, w_out, g_mlp, w_up, w_down):
    B, L, _ = x.shape
    f32 = jnp.float32
    h = _rmsnorm(x, g_mix)
    z = h @ w_in
    offs = [int(o) for o in np.cumsum(SPLITS)[:-1]]
    u, q, f, v, og, ga, gb = jnp.split(z, offs, axis=-1)
    u32 = u.astype(f32)
    if pool_prev is None:
        u_cat = u32
        pos = start_pos + jnp.arange(L)
    else:
        u_cat = jnp.concatenate([pool_prev.astype(f32), u32], axis=1)
        pos = start_pos - POOL_BUF + jnp.arange(POOL_BUF + L)
    pooled = _pool_mix(u_cat, pos)[:, -L:].reshape(B, L, POOL_GROUPS, POOL_GROUP)
    new_pool = u_cat[:, -POOL_BUF:].astype(x.dtype)
    ya = jnp.einsum('blgc,gcd->blgd', pooled, w_pool.astype(f32)).reshape(B, L, POOL_WIDTH)
    ya = (ya * pool_scale.astype(f32)).astype(x.dtype) @ w_a
    qh = jax.nn.silu(q.astype(f32)).reshape(B, L, HG_HEADS, HG_DK)
    fg = lb + (1.0 - lb) * jax.nn.sigmoid(f.astype(f32))
    kh = (1.0 - fg).reshape(B, L, HG_HEADS, HG_DK)
    gh = jnp.log(fg).reshape(B, L, HG_HEADS, HG_DK)
    vh = v.astype(f32).reshape(B, L, HG_HEADS, HG_DV)
    S0 = S0.astype(f32)
    if prompt:
        S, o_meta = _hgrn_chunk(S0, qh[:, :N_META], kh[:, :N_META], vh[:, :N_META], gh[:, :N_META])
        S, o_real = _hgrn_scan(S, qh[:, N_META:], kh[:, N_META:], vh[:, N_META:], gh[:, N_META:])
        o = jnp.concatenate([o_meta, o_real], axis=1)
    else:
        S, o = _hgrn_chunk(S0, qh, kh, vh, gh)
    o = _rmsnorm(o, g_onorm) * jax.nn.silu(og.astype(f32)).reshape(B, L, HG_HEADS, HG_DV)
    yb = o.reshape(B, L, HG_WIDTH).astype(x.dtype) @ w_b
    m = jax.nn.sigmoid(ga) * ya + jax.nn.sigmoid(gb) * yb
    x = x + m @ w_out
    h2 = _rmsnorm(x, g_mlp)
    x = x + jnp.square(jax.nn.relu(h2 @ w_up)) @ w_down
    return x, new_pool, S.astype(x.dtype)


def setup_inputs(seed: int = 0) -> dict:
    key = jax.random.key(seed)
    ks = jax.random.split(key, 20)
    nrm = jax.random.normal
    f32 = jnp.float32
    return {
        'x_prompt': nrm(ks[0], (BATCH, SEQ, D_MODEL), f32),
        'x_sample': nrm(ks[1], (DEC_BATCH, DEC_SEQ, D_MODEL), f32),
        'state_pool': nrm(ks[2], (DEPTH, DEC_BATCH, POOL_BUF, POOL_WIDTH), f32) * 0.5,
        'state_hgrn': nrm(ks[3], (DEPTH, DEC_BATCH, HG_HEADS, HG_DK, HG_DV), f32) * 0.5,
        'meta_tokens': nrm(ks[4], (N_META, D_MODEL), f32),
        'g_mix': 1.0 + 0.02 * nrm(ks[5], (DEPTH, D_MODEL), f32),
        'w_in': nrm(ks[6], (DEPTH, D_MODEL, N_IN), f32) * D_MODEL ** -0.5,
        'w_pool': nrm(ks[7], (DEPTH, POOL_GROUPS, POOL_GROUP, POOL_GROUP), f32) * POOL_GROUP ** -0.5,
        'pool_scale': 1.0 + 0.02 * nrm(ks[8], (DEPTH, POOL_WIDTH), f32),
        'hgrn_lb_logits': 0.5 * nrm(ks[9], (DEPTH + 1, HG_KTOT), f32),
        'g_onorm': 1.0 + 0.02 * nrm(ks[10], (DEPTH, HG_DV), f32),
        'w_a': nrm(ks[11], (DEPTH, POOL_WIDTH, D_MODEL), f32) * POOL_WIDTH ** -0.5,
        'w_b': nrm(ks[12], (DEPTH, HG_WIDTH, D_MODEL), f32) * HG_WIDTH ** -0.5,
        'w_out': nrm(ks[13], (DEPTH, D_MODEL, D_MODEL), f32) * D_MODEL ** -0.5,
        'g_mlp': 1.0 + 0.02 * nrm(ks[14], (DEPTH, D_MODEL), f32),
        'w_up': nrm(ks[15], (DEPTH, D_MODEL, D_FF), f32) * D_MODEL ** -0.5,
        'w_down': nrm(ks[16], (DEPTH, D_FF, D_MODEL), f32) * D_FF ** -0.5,
        'g_final': 1.0 + 0.02 * nrm(ks[17], (D_MODEL,), f32),
    }


def reference(x_prompt, x_sample, state_pool, state_hgrn, meta_tokens, g_mix, w_in, w_pool,
              pool_scale, hgrn_lb_logits, g_onorm, w_a, w_b, w_out, g_mlp, w_up, w_down, g_final):
    B = x_prompt.shape[0]
    lb_all = jnp.cumsum(jax.nn.softmax(hgrn_lb_logits.astype(jnp.float32), axis=0), axis=0)
    meta = jnp.broadcast_to(meta_tokens[None].astype(x_prompt.dtype), (B, N_META, D_MODEL))
    xp = jnp.concatenate([meta, x_prompt], axis=1)
    xs = x_sample
    pool_p, hgrn_p, pool_s, hgrn_s = [], [], [], []
    for l in range(DEPTH):
        params = (lb_all[l], g_mix[l], w_in[l], w_pool[l], pool_scale[l], g_onorm[l],
                  w_a[l], w_b[l], w_out[l], g_mlp[l], w_up[l], w_down[l])
        S_init = jnp.zeros((B, HG_HEADS, HG_DK, HG_DV), jnp.float32)
        xp, pp, sp = _layer(xp, None, S_init, 0, True, *params)
        xs, ps, ss = _layer(xs, state_pool[l], state_hgrn[l], PAST_LEN, False, *params)
        pool_p.append(pp)
        hgrn_p.append(sp)
        pool_s.append(ps)
        hgrn_s.append(ss)
    y_prompt = _rmsnorm(xp, g_final)[:, N_META:]
    y_sample = _rmsnorm(xs, g_final)
    new_pool_prompt = jnp.stack(pool_p)
    new_hgrn_prompt = jnp.stack(hgrn_p)
    new_pool_sample = jnp.stack(pool_s)
    new_hgrn_sample = jnp.stack(hgrn_s)
    return (y_prompt, y_sample, new_pool_prompt, new_hgrn_prompt, new_pool_sample, new_hgrn_sample)
```

```python
import functools

import jax
import jax.numpy as jnp
from jax import lax
from jax.experimental import pallas as pl
from jax.experimental.pallas import tpu as pltpu

D_MODEL = 1024
N_META = 16
POOL_WIDTH = 512
POOL_WINDOWS = (2, 4, 8, 16)
POOL_GROUP = 128
POOL_MAXW = 16
POOL_BUF = 15
HEADS = 4
DK = 128
DV = 128
KTOT = 512
HWIDTH = 512
D_FF = 4096
EPS = 1e-6
N_IN = 4608
O_U, O_Q, O_F, O_V, O_OG, O_GA, O_GB = 0, 512, 1024, 1536, 2048, 2560, 3584

SUB = 16
VMEM_LIMIT = 56 * 1024 * 1024

F32 = jnp.float32
BF16 = jnp.bfloat16


def _rms(x, g):
    return x * lax.rsqrt(jnp.mean(x * x, axis=-1, keepdims=True) + EPS) * g


def _sigmoid(x):
    return 1.0 / (1.0 + jnp.exp(-x))


def _dot(a, b):
    return jnp.dot(a, b, preferred_element_type=F32)


def _dot_nt(a, b):
    return lax.dot_general(a, b, (((1,), (1,)), ((), ())), preferred_element_type=F32)


def _dot_tn(a, b):
    return lax.dot_general(a, b, (((0,), (0,)), ((), ())), preferred_element_type=F32)


def _lower_bound(lb_logits):
    m = jnp.max(lb_logits, axis=0, keepdims=True)
    e = jnp.exp(lb_logits - m)
    return e[0:1, :] / jnp.sum(e, axis=0, keepdims=True)


def _gates(zq, zf, lb):
    q = zq * _sigmoid(zq)
    fg = lb + (1.0 - lb) * _sigmoid(zf)
    return q, fg


def _cumsum_rows(g):
    n = g.shape[0]
    r = lax.broadcasted_iota(jnp.int32, (n, n), 0)
    c = lax.broadcasted_iota(jnp.int32, (n, n), 1)
    tril = (r >= c).astype(F32)
    return jnp.dot(tril, g, precision=lax.Precision.HIGHEST, preferred_element_type=F32)


def _prompt_mixer_kernel(x_ref, meta_ref, gmix_ref, win_ref, wpool_ref, pscale_ref, lbl_ref,
                         gon_ref, wa_ref, wb_ref, wout_ref,
                         x1_ref, pool_ref, hgrn_ref,
                         z_sc, ubuf, q_sc, k_sc, g_sc, o_sc, st_sc, st_meta, u_meta, *, tile):
    b = pl.program_id(0)
    t = pl.program_id(1)
    nt = pl.num_programs(1)
    lb = _lower_bound(lbl_ref[...])
    gmix = gmix_ref[...]

    @pl.when((b == 0) & (t == 0))
    def _meta():
        hm = _rms(meta_ref[...], gmix).astype(BF16)
        zm = _dot(hm, win_ref[:, 0:O_OG])
        u_meta[...] = zm[:, O_U:O_U + POOL_WIDTH]
        _, fg = _gates(zm[:, O_Q:O_Q + KTOT], zm[:, O_F:O_F + KTOT], lb)
        G = _cumsum_rows(jnp.log(fg))
        kt = (1.0 - fg) * jnp.exp(G[N_META - 1:N_META, :] - G)
        v = zm[:, O_V:O_V + HWIDTH]
        for h in range(HEADS):
            sl = slice(h * DK, (h + 1) * DK)
            st_meta[h] = _dot_tn(v[:, sl].astype(BF16), kt[:, sl].astype(BF16))

    @pl.when(t == 0)
    def _init():
        st_sc[...] = st_meta[...]
        ubuf[0:POOL_MAXW, :] = u_meta[...]

    x = x_ref[0]
    h = _rms(x, gmix).astype(BF16)
    z_sc[...] = _dot(h, win_ref[...])

    u = z_sc[:, O_U:O_U + POOL_WIDTH]
    ubuf[POOL_MAXW:POOL_MAXW + tile, :] = u
    ya_parts = []
    for gi, w in enumerate(POOL_WINDOWS):
        cs = slice(gi * POOL_GROUP, (gi + 1) * POOL_GROUP)
        wsum = ubuf[POOL_MAXW:POOL_MAXW + tile, cs]
        for j in range(1, w):
            wsum = wsum + ubuf[POOL_MAXW - j:POOL_MAXW - j + tile, cs]
        pooled = wsum / float(w) - ubuf[POOL_MAXW:POOL_MAXW + tile, cs]
        ya_parts.append(_dot(pooled.astype(BF16), wpool_ref[gi]))
    ya_pre = jnp.concatenate(ya_parts, axis=-1) * pscale_ref[...]
    ya = _dot(ya_pre.astype(BF16), wa_ref[...])

    @pl.when(t == nt - 1)
    def _pool_out():
        pool_ref[0] = ubuf[tile + 1:tile + POOL_MAXW, :]

    ubuf[0:POOL_MAXW, :] = ubuf[tile:tile + POOL_MAXW, :]

    q_all, fg_all = _gates(z_sc[:, O_Q:O_Q + KTOT], z_sc[:, O_F:O_F + KTOT], lb)
    q_sc[...] = q_all
    k_sc[...] = 1.0 - fg_all
    g_sc[...] = jnp.log(fg_all)
    row = lax.broadcasted_iota(jnp.int32, (SUB, 1), 0)

    def step(i, carry):
        r = pl.ds(pl.multiple_of(i * SUB, SUB), SUB)
        G = _cumsum_rows(g_sc[r, :])
        q = q_sc[r, :]
        k = k_sc[r, :]
        v = z_sc[r, O_V:O_V + HWIDTH]
        g_last = G[SUB - 1:SUB, :]
        qt = q * jnp.exp(G)
        kt = k * jnp.exp(g_last - G)
        dec = jnp.exp(g_last)
        for hd in range(HEADS):
            sl = slice(hd * DK, (hd + 1) * DK)
            st = st_sc[hd]
            o = _dot_nt(qt[:, sl].astype(BF16), st.astype(BF16))
            Gh, qh, kh, vh = G[:, sl], q[:, sl], k[:, sl], v[:, sl]
            for s in range(SUB):
                e = jnp.exp(jnp.minimum(Gh - Gh[s:s + 1, :], 0.0))
                p = jnp.sum(qh * e * kh[s:s + 1, :], axis=-1, keepdims=True)
                p = jnp.where(row >= s, p, 0.0)
                o = o + p * vh[s:s + 1, :]
            o_sc[r, sl] = o
            st_sc[hd] = st * dec[:, sl] + _dot_tn(vh.astype(BF16), kt[:, sl].astype(BF16))
        return carry

    lax.fori_loop(0, tile // SUB, step, 0)

    gon = gon_ref[...]
    o_parts = []
    for hd in range(HEADS):
        sl = slice(hd * DV, (hd + 1) * DV)
        og = z_sc[:, O_OG + hd * DV:O_OG + (hd + 1) * DV]
        o_parts.append(_rms(o_sc[:, sl], gon) * (og * _sigmoid(og)))
    yb = _dot(jnp.concatenate(o_parts, axis=-1).astype(BF16), wb_ref[...])

    m = (_sigmoid(z_sc[:, O_GA:O_GA + D_MODEL]) * ya
         + _sigmoid(z_sc[:, O_GB:O_GB + D_MODEL]) * yb)
    x1_ref[0] = x + _dot(m.astype(BF16), wout_ref[...])

    @pl.when(t == nt - 1)
    def _state_out():
        for hd in range(HEADS):
            hgrn_ref[0, hd] = st_sc[hd].T


def _const_spec(shape):
    nd = len(shape)
    return pl.BlockSpec(shape, lambda *_: (0,) * nd, pipeline_mode=pl.Buffered(1))


def _prompt_mixer(x, meta, gmix, win, wpool, pscale, lbl, gon, wa, wb, wout, *, tile):
    B, L, _ = x.shape
    nt = L // tile
    kern = functools.partial(_prompt_mixer_kernel, tile=tile)
    return pl.pallas_call(
        kern,
        grid=(B, nt),
        in_specs=[
            pl.BlockSpec((1, tile, D_MODEL), lambda b, t: (b, t, 0)),
            _const_spec((N_META, D_MODEL)),
            _const_spec((1, D_MODEL)),
            _const_spec((D_MODEL, N_IN)),
            _const_spec((len(POOL_WINDOWS), POOL_GROUP, POOL_GROUP)),
            _const_spec((1, POOL_WIDTH)),
            _const_spec((2, KTOT)),
            _const_spec((1, DV)),
            _const_spec((POOL_WIDTH, D_MODEL)),
            _const_spec((HWIDTH, D_MODEL)),
            _const_spec((D_MODEL, D_MODEL)),
        ],
        out_specs=[
            pl.BlockSpec((1, tile, D_MODEL), lambda b, t: (b, t, 0)),
            pl.BlockSpec((1, POOL_BUF, POOL_WIDTH), lambda b, t: (b, 0, 0)),
            pl.BlockSpec((1, HEADS, DK, DV), lambda b, t: (b, 0, 0, 0)),
        ],
        out_shape=[
            jax.ShapeDtypeStruct((B, L, D_MODEL), F32),
            jax.ShapeDtypeStruct((B, POOL_BUF, POOL_WIDTH), F32),
            jax.ShapeDtypeStruct((B, HEADS, DK, DV), F32),
        ],
        scratch_shapes=[
            pltpu.VMEM((tile, N_IN), F32),
            pltpu.VMEM((tile + POOL_MAXW, POOL_WIDTH), F32),
            pltpu.VMEM((tile, KTOT), F32),
            pltpu.VMEM((tile, KTOT), F32),
            pltpu.VMEM((tile, KTOT), F32),
            pltpu.VMEM((tile, HWIDTH), F32),
            pltpu.VMEM((HEADS, DV, DK), F32),
            pltpu.VMEM((HEADS, DV, DK), F32),
            pltpu.VMEM((N_META, POOL_WIDTH), F32),
        ],
        compiler_params=pltpu.CompilerParams(
            dimension_semantics=("arbitrary", "arbitrary"),
            vmem_limit_bytes=VMEM_LIMIT),
        name="prompt_mixer",
    )(x, meta, gmix, win, wpool, pscale, lbl, gon, wa, wb, wout)


def _decode_mixer_kernel(x_ref, poolT_ref, s_ref, gmix_ref, win_ref, wpool_ref, pscale_ref,
                         lbl_ref, gon_ref, wa_ref, wb_ref, wout_ref,
                         x1_ref, unew_ref, snew_ref,
                         z_sc, o_sc, *, ts):
    i = pl.program_id(0)
    n = pl.num_programs(0)
    lb = _lower_bound(lbl_ref[...])

    @pl.when(i == 0)
    def _in_proj():
        h = _rms(x_ref[...], gmix_ref[...]).astype(BF16)
        z_sc[...] = _dot(h, win_ref[...])

    r = pl.ds(pl.multiple_of(i * ts, ts), ts)
    q, fg = _gates(z_sc[r, O_Q:O_Q + KTOT], z_sc[r, O_F:O_F + KTOT], lb)
    k = 1.0 - fg
    v = z_sc[r, O_V:O_V + HWIDTH]
    row = lax.broadcasted_iota(jnp.int32, (ts, 1), 0)
    for hd in range(HEADS):
        sl = slice(hd * DK, (hd + 1) * DK)
        fT = fg[:, sl].T
        kT = k[:, sl].T
        qT = q[:, sl].T
        o = jnp.zeros((ts, DV), F32)
        for s in range(ts):
            s_new = fT[:, s:s + 1] * s_ref[s, hd] + kT[:, s:s + 1] * v[s:s + 1, sl]
            snew_ref[s, hd] = s_new
            o_row = jnp.sum(qT[:, s:s + 1] * s_new, axis=0, keepdims=True)
            o = jnp.where(row == s, o_row, o)
        o_sc[r, sl] = o

    @pl.when(i == n - 1)
    def _out_proj():
        x = x_ref[...]
        u = z_sc[:, O_U:O_U + POOL_WIDTH]
        unew_ref[...] = u
        ya_parts = []
        for gi, w in enumerate(POOL_WINDOWS):
            cs = slice(gi * POOL_GROUP, (gi + 1) * POOL_GROUP)
            wsum = u[:, cs]
            for j in range(1, w):
                wsum = wsum + poolT_ref[POOL_BUF - j, :, cs]
            pooled = wsum / float(w) - u[:, cs]
            ya_parts.append(_dot(pooled.astype(BF16), wpool_ref[gi]))
        ya_pre = jnp.concatenate(ya_parts, axis=-1) * pscale_ref[...]
        ya = _dot(ya_pre.astype(BF16), wa_ref[...])
        gon = gon_ref[...]
        o_parts = []
        for hd in range(HEADS):
            sl = slice(hd * DV, (hd + 1) * DV)
            og = z_sc[:, O_OG + hd * DV:O_OG + (hd + 1) * DV]
            o_parts.append(_rms(o_sc[:, sl], gon) * (og * _sigmoid(og)))
        yb = _dot(jnp.concatenate(o_parts, axis=-1).astype(BF16), wb_ref[...])
        m = (_sigmoid(z_sc[:, O_GA:O_GA + D_MODEL]) * ya
             + _sigmoid(z_sc[:, O_GB:O_GB + D_MODEL]) * yb)
        x1_ref[...] = x + _dot(m.astype(BF16), wout_ref[...])


def _decode_mixer(x, poolT, state, gmix, win, wpool, pscale, lbl, gon, wa, wb, wout, *, ts):
    n = x.shape[0]
    kern = functools.partial(_decode_mixer_kernel, ts=ts)
    return pl.pallas_call(
        kern,
        grid=(n // ts,),
        in_specs=[
            _const_spec((n, D_MODEL)),
            _const_spec((POOL_BUF, n, POOL_WIDTH)),
            pl.BlockSpec((ts, HEADS, DK, DV), lambda i: (i, 0, 0, 0)),
            _const_spec((1, D_MODEL)),
            _const_spec((D_MODEL, N_IN)),
            _const_spec((len(POOL_WINDOWS), POOL_GROUP, POOL_GROUP)),
            _const_spec((1, POOL_WIDTH)),
            _const_spec((2, KTOT)),
            _const_spec((1, DV)),
            _const_spec((POOL_WIDTH, D_MODEL)),
            _const_spec((HWIDTH, D_MODEL)),
            _const_spec((D_MODEL, D_MODEL)),
        ],
        out_specs=[
            pl.BlockSpec((n, D_MODEL), lambda i: (0, 0)),
            pl.BlockSpec((n, POOL_WIDTH), lambda i: (0, 0)),
            pl.BlockSpec((ts, HEADS, DK, DV), lambda i: (i, 0, 0, 0)),
        ],
        out_shape=[
            jax.ShapeDtypeStruct((n, D_MODEL), F32),
            jax.ShapeDtypeStruct((n, POOL_WIDTH), F32),
            jax.ShapeDtypeStruct((n, HEADS, DK, DV), F32),
        ],
        scratch_shapes=[
            pltpu.VMEM((n, N_IN), F32),
            pltpu.VMEM((n, HWIDTH), F32),
        ],
        compiler_params=pltpu.CompilerParams(
            dimension_semantics=("arbitrary",),
            vmem_limit_bytes=VMEM_LIMIT),
        name="decode_mixer",
    )(x, poolT, state, gmix, win, wpool, pscale, lbl, gon, wa, wb, wout)


def _mlp_kernel(x_ref, gmlp_ref, wup_ref, wdown_ref, gfin_ref, y_ref, *, ff_chunk):
    x = x_ref[...]
    h = _rms(x, gmlp_ref[...]).astype(BF16)
    acc = x
    for c in range(D_FF // ff_chunk):
        cs = slice(c * ff_chunk, (c + 1) * ff_chunk)
        a = jnp.maximum(_dot(h, wup_ref[:, cs]), 0.0)
        acc = acc + _dot((a * a).astype(BF16), wdown_ref[cs, :])
    y_ref[...] = _rms(acc, gfin_ref[...])


def _mlp(x, gmlp, wup, wdown, gfin, *, tm, ff_chunk=1024):
    n = x.shape[0]
    kern = functools.partial(_mlp_kernel, ff_chunk=ff_chunk)
    return pl.pallas_call(
        kern,
        grid=(n // tm,),
        in_specs=[
            pl.BlockSpec((tm, D_MODEL), lambda i: (i, 0)),
            _const_spec((1, D_MODEL)),
            _const_spec((D_MODEL, D_FF)),
            _const_spec((D_FF, D_MODEL)),
            _const_spec((1, D_MODEL)),
        ],
        out_specs=pl.BlockSpec((tm, D_MODEL), lambda i: (i, 0)),
        out_shape=jax.ShapeDtypeStruct((n, D_MODEL), F32),
        compiler_params=pltpu.CompilerParams(
            dimension_semantics=("arbitrary",),
            vmem_limit_bytes=VMEM_LIMIT),
        name="channel_mlp",
    )(x, gmlp, wup, wdown, gfin)


def kernel(x_prompt, x_sample, state_pool, state_hgrn, meta_tokens, g_mix, w_in, w_pool, pool_scale,
           hgrn_lb_logits, g_onorm, w_a, w_b, w_out, g_mlp, w_up, w_down, g_final):
    B, L, _ = x_prompt.shape
    NS = x_sample.shape[0]
    assert g_mix.shape[0] == 1, "single-layer trunk"
    gmix = g_mix[0][None, :]
    win = w_in[0].astype(BF16)
    wpool = w_pool[0].astype(BF16)
    pscale = pool_scale[0][None, :]
    gon = g_onorm[0][None, :]
    wa = w_a[0].astype(BF16)
    wb = w_b[0].astype(BF16)
    wout = w_out[0].astype(BF16)
    gmlp = g_mlp[0][None, :]
    wup = w_up[0].astype(BF16)
    wdown = w_down[0].astype(BF16)
    gfin = g_final[None, :]

    x1_p, pool_p, hgrn_p = _prompt_mixer(
        x_prompt, meta_tokens, gmix, win, wpool, pscale, hgrn_lb_logits, gon, wa, wb, wout, tile=256)
    y_p = _mlp(x1_p.reshape(B * L, D_MODEL), gmlp, wup, wdown, gfin, tm=512)

    xs = x_sample.reshape(NS, D_MODEL)
    poolT = jnp.swapaxes(state_pool[0], 0, 1)
    x1_s, u_s, hgrn_s = _decode_mixer(
        xs, poolT, state_hgrn[0], gmix, win, wpool, pscale, hgrn_lb_logits, gon, wa, wb, wout, ts=16)
    y_s = _mlp(x1_s, gmlp, wup, wdown, gfin, tm=NS)
    pool_s = jnp.concatenate([state_pool[0][:, 1:, :], u_s[:, None, :]], axis=1)

    return (y_p.reshape(B, L, D_MODEL), y_s.reshape(NS, 1, D_MODEL),
            pool_p[None], hgrn_p[None], pool_s[None], hgrn_s[None])
```

```python
import functools

import jax
import jax.numpy as jnp
from jax import lax
from jax.experimental import pallas as pl
from jax.experimental.pallas import tpu as pltpu

D_MODEL = 1024
N_META = 16
POOL_WIDTH = 512
POOL_WINDOWS = (2, 4, 8, 16)
POOL_GROUP = 128
POOL_MAXW = 16
POOL_BUF = 15
HEADS = 4
DK = 128
DV = 128
KTOT = 512
HWIDTH = 512
D_FF = 4096
EPS = 1e-6
N_IN = 4608
O_U, O_Q, O_F, O_V, O_OG, O_GA, O_GB = 0, 512, 1024, 1536, 2048, 2560, 3584

SUB = 16
CHUNK = 128
LEVEL_HALVES = tuple(1 << i for i in range(CHUNK.bit_length() - 1))
VMEM_LIMIT = 56 * 1024 * 1024

F32 = jnp.float32
BF16 = jnp.bfloat16


def _rms(x, g):
    return x * lax.rsqrt(jnp.mean(x * x, axis=-1, keepdims=True) + EPS) * g


def _sigmoid(x):
    return 1.0 / (1.0 + jnp.exp(-x))


def _dot(a, b):
    return jnp.dot(a, b, preferred_element_type=F32)


def _dot_nt(a, b):
    return lax.dot_general(a, b, (((1,), (1,)), ((), ())), preferred_element_type=F32)


def _dot_tn(a, b):
    return lax.dot_general(a, b, (((0,), (0,)), ((), ())), preferred_element_type=F32)


def _lower_bound(lb_logits):
    m = jnp.max(lb_logits, axis=0, keepdims=True)
    e = jnp.exp(lb_logits - m)
    return e[0:1, :] / jnp.sum(e, axis=0, keepdims=True)


def _gates(zq, zf, lb):
    q = zq * _sigmoid(zq)
    fg = lb + (1.0 - lb) * _sigmoid(zf)
    return q, fg


def _cumsum_rows(g):
    n = g.shape[0]
    r = lax.broadcasted_iota(jnp.int32, (n, n), 0)
    c = lax.broadcasted_iota(jnp.int32, (n, n), 1)
    tril = (r >= c).astype(BF16)
    hi = g.astype(BF16)
    r1 = g - hi.astype(F32)
    mid = r1.astype(BF16)
    lo = (r1 - mid.astype(F32)).astype(BF16)
    return _dot(tril, hi) + _dot(tril, mid) + _dot(tril, lo)


def _level_map(n):
    t = lax.broadcasted_iota(jnp.int32, (n, n), 0)
    s = lax.broadcasted_iota(jnp.int32, (n, n), 1)
    x = t ^ s
    lvl = jnp.full((n, n), -1, jnp.int32)
    for li in range(n.bit_length() - 1):
        lvl = jnp.where((x >> li) == 1, li, lvl)
    return jnp.where(t > s, lvl, -1)


def _level_weights(m, r0, Gg, qg, kg, fgg, g_sc, row8, row16):
    if m == 1:
        return jnp.where((row16 & 1) != 0, qg * fgg, kg)
    if m >= SUB:
        blk = (r0 // (2 * m)) * (2 * m)
        ref = g_sc[blk + m - 1:blk + m, :]
        sel = qg if (r0 & m) else kg
    elif m == SUB // 2:
        ref = g_sc[r0 + m - 1:r0 + m, :]
        sel = jnp.where(row16 >= m, qg, kg)
    else:
        halves = []
        for rb in (r0, r0 + 8):
            if m == 4:
                halves.append(jnp.broadcast_to(g_sc[rb + 3:rb + 4, :], (8, KTOT)))
            else:
                halves.append(jnp.where(row8 >= 4, g_sc[rb + 5:rb + 6, :], g_sc[rb + 1:rb + 2, :]))
        ref = jnp.concatenate(halves, axis=0)
        sel = jnp.where((row16 & m) != 0, qg, kg)
    return jnp.exp(-jnp.abs(Gg - ref)) * sel


def _recurrence_chunk(base, z_sc, lb, lvl, g_sc, w_sc, qb_sc, kb_sc, o_sc, st_sc):
    rs = slice(base, base + CHUNK)
    q, fg = _gates(z_sc[rs, O_Q:O_Q + KTOT], z_sc[rs, O_F:O_F + KTOT], lb)
    k = 1.0 - fg
    G = _cumsum_rows(jnp.log(fg))
    g_sc[...] = G
    g_last = G[CHUNK - 1:CHUNK, :]
    v = z_sc[rs, O_V:O_V + HWIDTH]
    v_bf = v.astype(BF16)
    row8 = lax.broadcasted_iota(jnp.int32, (8, 1), 0)
    row16 = lax.broadcasted_iota(jnp.int32, (SUB, 1), 0)
    for j in range(CHUNK // SUB):
        r0 = j * SUB
        gs = slice(r0, r0 + SUB)
        Gg, qg, kg, fgg = G[gs], q[gs], k[gs], fg[gs]
        for li, m in enumerate(LEVEL_HALVES):
            w_sc[li, gs, :] = _level_weights(m, r0, Gg, qg, kg, fgg, g_sc, row8, row16).astype(BF16)
        qb_sc[gs, :] = (qg * jnp.exp(Gg)).astype(BF16)
        kb_sc[gs, :] = (kg * jnp.exp(g_last - Gg)).astype(BF16)
    dec = jnp.exp(g_last)
    qk = q * k
    for hd in range(HEADS):
        sl = slice(hd * DK, (hd + 1) * DK)
        A = jnp.zeros((CHUNK, CHUNK), F32)
        for li in range(len(LEVEL_HALVES)):
            w = w_sc[li, :, sl]
            A = jnp.where(lvl == li, _dot_nt(w, w), A)
        st = st_sc[hd]
        o = (_dot(A.astype(BF16), v_bf[:, sl])
             + _dot_nt(qb_sc[:, sl], st.astype(BF16))
             + jnp.sum(qk[:, sl], axis=-1, keepdims=True) * v[:, sl])
        o_sc[rs, sl] = o
        st_sc[hd] = st * dec[:, sl] + _dot_tn(v_bf[:, sl], kb_sc[:, sl])


def _prompt_mixer_kernel(x_ref, meta_ref, gmix_ref, win_ref, wpool_ref, pscale_ref, lbl_ref,
                         gon_ref, wa_ref, wb_ref, wout_ref,
                         x1_ref, pool_ref, hgrn_ref,
                         z_sc, ubuf, g_sc, w_sc, qb_sc, kb_sc, o_sc, st_sc, st_meta, u_meta, *, tile):
    b = pl.program_id(0)
    t = pl.program_id(1)
    nt = pl.num_programs(1)
    lb = _lower_bound(lbl_ref[...])
    gmix = gmix_ref[...]

    @pl.when((b == 0) & (t == 0))
    def _meta():
        hm = _rms(meta_ref[...], gmix).astype(BF16)
        zm = _dot(hm, win_ref[:, 0:O_OG])
        u_meta[...] = zm[:, O_U:O_U + POOL_WIDTH]
        _, fg = _gates(zm[:, O_Q:O_Q + KTOT], zm[:, O_F:O_F + KTOT], lb)
        G = _cumsum_rows(jnp.log(fg))
        kt = (1.0 - fg) * jnp.exp(G[N_META - 1:N_META, :] - G)
        v = zm[:, O_V:O_V + HWIDTH]
        for h in range(HEADS):
            sl = slice(h * DK, (h + 1) * DK)
            st_meta[h] = _dot_tn(v[:, sl].astype(BF16), kt[:, sl].astype(BF16))

    @pl.when(t == 0)
    def _init():
        st_sc[...] = st_meta[...]
        ubuf[0:POOL_MAXW, :] = u_meta[...]

    x = x_ref[0]
    h = _rms(x, gmix).astype(BF16)
    z_sc[...] = _dot(h, win_ref[...])

    u = z_sc[:, O_U:O_U + POOL_WIDTH]
    ubuf[POOL_MAXW:POOL_MAXW + tile, :] = u
    ya_parts = []
    for gi, w in enumerate(POOL_WINDOWS):
        cs = slice(gi * POOL_GROUP, (gi + 1) * POOL_GROUP)
        wsum = ubuf[POOL_MAXW:POOL_MAXW + tile, cs]
        for j in range(1, w):
            wsum = wsum + ubuf[POOL_MAXW - j:POOL_MAXW - j + tile, cs]
        pooled = wsum / float(w) - ubuf[POOL_MAXW:POOL_MAXW + tile, cs]
        ya_parts.append(_dot(pooled.astype(BF16), wpool_ref[gi]))
    ya_pre = jnp.concatenate(ya_parts, axis=-1) * pscale_ref[...]
    ya = _dot(ya_pre.astype(BF16), wa_ref[...])

    @pl.when(t == nt - 1)
    def _pool_out():
        pool_ref[0] = ubuf[tile + 1:tile + POOL_MAXW, :]

    ubuf[0:POOL_MAXW, :] = ubuf[tile:tile + POOL_MAXW, :]

    lvl = _level_map(CHUNK)
    for c in range(tile // CHUNK):
        _recurrence_chunk(c * CHUNK, z_sc, lb, lvl, g_sc, w_sc, qb_sc, kb_sc, o_sc, st_sc)

    gon = gon_ref[...]
    o_parts = []
    for hd in range(HEADS):
        sl = slice(hd * DV, (hd + 1) * DV)
        og = z_sc[:, O_OG + hd * DV:O_OG + (hd + 1) * DV]
        o_parts.append(_rms(o_sc[:, sl], gon) * (og * _sigmoid(og)))
    yb = _dot(jnp.concatenate(o_parts, axis=-1).astype(BF16), wb_ref[...])

    m = (_sigmoid(z_sc[:, O_GA:O_GA + D_MODEL]) * ya
         + _sigmoid(z_sc[:, O_GB:O_GB + D_MODEL]) * yb)
    x1_ref[0] = x + _dot(m.astype(BF16), wout_ref[...])

    @pl.when(t == nt - 1)
    def _state_out():
        for hd in range(HEADS):
            hgrn_ref[0, hd] = st_sc[hd].T


def _const_spec(shape):
    nd = len(shape)
    return pl.BlockSpec(shape, lambda *_: (0,) * nd, pipeline_mode=pl.Buffered(1))


def _prompt_mixer(x, meta, gmix, win, wpool, pscale, lbl, gon, wa, wb, wout, *, tile):
    B, L, _ = x.shape
    nt = L // tile
    kern = functools.partial(_prompt_mixer_kernel, tile=tile)
    return pl.pallas_call(
        kern,
        grid=(B, nt),
        in_specs=[
            pl.BlockSpec((1, tile, D_MODEL), lambda b, t: (b, t, 0)),
            _const_spec((N_META, D_MODEL)),
            _const_spec((1, D_MODEL)),
            _const_spec((D_MODEL, N_IN)),
            _const_spec((len(POOL_WINDOWS), POOL_GROUP, POOL_GROUP)),
            _const_spec((1, POOL_WIDTH)),
            _const_spec((2, KTOT)),
            _const_spec((1, DV)),
            _const_spec((POOL_WIDTH, D_MODEL)),
            _const_spec((HWIDTH, D_MODEL)),
            _const_spec((D_MODEL, D_MODEL)),
        ],
        out_specs=[
            pl.BlockSpec((1, tile, D_MODEL), lambda b, t: (b, t, 0)),
            pl.BlockSpec((1, POOL_BUF, POOL_WIDTH), lambda b, t: (b, 0, 0)),
            pl.BlockSpec((1, HEADS, DK, DV), lambda b, t: (b, 0, 0, 0)),
        ],
        out_shape=[
            jax.ShapeDtypeStruct((B, L, D_MODEL), F32),
            jax.ShapeDtypeStruct((B, POOL_BUF, POOL_WIDTH), F32),
            jax.ShapeDtypeStruct((B, HEADS, DK, DV), F32),
        ],
        scratch_shapes=[
            pltpu.VMEM((tile, N_IN), F32),
            pltpu.VMEM((tile + POOL_MAXW, POOL_WIDTH), F32),
            pltpu.VMEM((CHUNK, KTOT), F32),
            pltpu.VMEM((len(LEVEL_HALVES), CHUNK, KTOT), BF16),
            pltpu.VMEM((CHUNK, KTOT), BF16),
            pltpu.VMEM((CHUNK, KTOT), BF16),
            pltpu.VMEM((tile, HWIDTH), F32),
            pltpu.VMEM((HEADS, DV, DK), F32),
            pltpu.VMEM((HEADS, DV, DK), F32),
            pltpu.VMEM((N_META, POOL_WIDTH), F32),
        ],
        compiler_params=pltpu.CompilerParams(
            dimension_semantics=("arbitrary", "arbitrary"),
            vmem_limit_bytes=VMEM_LIMIT),
        name="prompt_mixer",
    )(x, meta, gmix, win, wpool, pscale, lbl, gon, wa, wb, wout)


def _decode_mixer_kernel(x_ref, poolT_ref, s_ref, gmix_ref, win_ref, wpool_ref, pscale_ref,
                         lbl_ref, gon_ref, wa_ref, wb_ref, wout_ref,
                         x1_ref, unew_ref, snew_ref,
                         z_sc, o_sc, *, ts):
    i = pl.program_id(0)
    n = pl.num_programs(0)
    lb = _lower_bound(lbl_ref[...])

    @pl.when(i == 0)
    def _in_proj():
        h = _rms(x_ref[...], gmix_ref[...]).astype(BF16)
        z_sc[...] = _dot(h, win_ref[...])

    r = pl.ds(pl.multiple_of(i * ts, ts), ts)
    q, fg = _gates(z_sc[r, O_Q:O_Q + KTOT], z_sc[r, O_F:O_F + KTOT], lb)
    k = 1.0 - fg
    v = z_sc[r, O_V:O_V + HWIDTH]
    row = lax.broadcasted_iota(jnp.int32, (ts, 1), 0)
    for hd in range(HEADS):
        sl = slice(hd * DK, (hd + 1) * DK)
        fT = fg[:, sl].T
        kT = k[:, sl].T
        qT = q[:, sl].T
        o = jnp.zeros((ts, DV), F32)
        for s in range(ts):
            s_new = fT[:, s:s + 1] * s_ref[s, hd] + kT[:, s:s + 1] * v[s:s + 1, sl]
            snew_ref[s, hd] = s_new
            o_row = jnp.sum(qT[:, s:s + 1] * s_new, axis=0, keepdims=True)
            o = jnp.where(row == s, o_row, o)
        o_sc[r, sl] = o

    @pl.when(i == n - 1)
    def _out_proj():
        x = x_ref[...]
        u = z_sc[:, O_U:O_U + POOL_WIDTH]
        unew_ref[...] = u
        ya_parts = []
        for gi, w in enumerate(POOL_WINDOWS):
            cs = slice(gi * POOL_GROUP, (gi + 1) * POOL_GROUP)
            wsum = u[:, cs]
            for j in range(1, w):
                wsum = wsum + poolT_ref[POOL_BUF - j, :, cs]
            pooled = wsum / float(w) - u[:, cs]
            ya_parts.append(_dot(pooled.astype(BF16), wpool_ref[gi]))
        ya_pre = jnp.concatenate(ya_parts, axis=-1) * pscale_ref[...]
        ya = _dot(ya_pre.astype(BF16), wa_ref[...])
        gon = gon_ref[...]
        o_parts = []
        for hd in range(HEADS):
            sl = slice(hd * DV, (hd + 1) * DV)
            og = z_sc[:, O_OG + hd * DV:O_OG + (hd + 1) * DV]
            o_parts.append(_rms(o_sc[:, sl], gon) * (og * _sigmoid(og)))
        yb = _dot(jnp.concatenate(o_parts, axis=-1).astype(BF16), wb_ref[...])
        m = (_sigmoid(z_sc[:, O_GA:O_GA + D_MODEL]) * ya
             + _sigmoid(z_sc[:, O_GB:O_GB + D_MODEL]) * yb)
        x1_ref[...] = x + _dot(m.astype(BF16), wout_ref[...])


def _decode_mixer(x, poolT, state, gmix, win, wpool, pscale, lbl, gon, wa, wb, wout, *, ts):
    n = x.shape[0]
    kern = functools.partial(_decode_mixer_kernel, ts=ts)
    return pl.pallas_call(
        kern,
        grid=(n // ts,),
        in_specs=[
            _const_spec((n, D_MODEL)),
            _const_spec((POOL_BUF, n, POOL_WIDTH)),
            pl.BlockSpec((ts, HEADS, DK, DV), lambda i: (i, 0, 0, 0)),
            _const_spec((1, D_MODEL)),
            _const_spec((D_MODEL, N_IN)),
            _const_spec((len(POOL_WINDOWS), POOL_GROUP, POOL_GROUP)),
            _const_spec((1, POOL_WIDTH)),
            _const_spec((2, KTOT)),
            _const_spec((1, DV)),
            _const_spec((POOL_WIDTH, D_MODEL)),
            _const_spec((HWIDTH, D_MODEL)),
            _const_spec((D_MODEL, D_MODEL)),
        ],
        out_specs=[
            pl.BlockSpec((n, D_MODEL), lambda i: (0, 0)),
            pl.BlockSpec((n, POOL_WIDTH), lambda i: (0, 0)),
            pl.BlockSpec((ts, HEADS, DK, DV), lambda i: (i, 0, 0, 0)),
        ],
        out_shape=[
            jax.ShapeDtypeStruct((n, D_MODEL), F32),
            jax.ShapeDtypeStruct((n, POOL_WIDTH), F32),
            jax.ShapeDtypeStruct((n, HEADS, DK, DV), F32),
        ],
        scratch_shapes=[
            pltpu.VMEM((n, N_IN), F32),
            pltpu.VMEM((n, HWIDTH), F32),
        ],
        compiler_params=pltpu.CompilerParams(
            dimension_semantics=("arbitrary",),
            vmem_limit_bytes=VMEM_LIMIT),
        name="decode_mixer",
    )(x, poolT, state, gmix, win, wpool, pscale, lbl, gon, wa, wb, wout)


def _mlp_kernel(x_ref, gmlp_ref, wup_ref, wdown_ref, gfin_ref, y_ref, *, ff_chunk):
    x = x_ref[...]
    h = _rms(x, gmlp_ref[...]).astype(BF16)
    acc = x
    for c in range(D_FF // ff_chunk):
        cs = slice(c * ff_chunk, (c + 1) * ff_chunk)
        a = jnp.maximum(_dot(h, wup_ref[:, cs]), 0.0)
        acc = acc + _dot((a * a).astype(BF16), wdown_ref[cs, :])
    y_ref[...] = _rms(acc, gfin_ref[...])


def _mlp(x, gmlp, wup, wdown, gfin, *, tm, ff_chunk=1024):
    n = x.shape[0]
    kern = functools.partial(_mlp_kernel, ff_chunk=ff_chunk)
    return pl.pallas_call(
        kern,
        grid=(n // tm,),
        in_specs=[
            pl.BlockSpec((tm, D_MODEL), lambda i: (i, 0)),
            _const_spec((1, D_MODEL)),
            _const_spec((D_MODEL, D_FF)),
            _const_spec((D_FF, D_MODEL)),
            _const_spec((1, D_MODEL)),
        ],
        out_specs=pl.BlockSpec((tm, D_MODEL), lambda i: (i, 0)),
        out_shape=jax.ShapeDtypeStruct((n, D_MODEL), F32),
        compiler_params=pltpu.CompilerParams(
            dimension_semantics=("arbitrary",),
            vmem_limit_bytes=VMEM_LIMIT),
        name="channel_mlp",
    )(x, gmlp, wup, wdown, gfin)


def kernel(x_prompt, x_sample, state_pool, state_hgrn, meta_tokens, g_mix, w_in, w_pool, pool_scale,
           hgrn_lb_logits, g_onorm, w_a, w_b, w_out, g_mlp, w_up, w_down, g_final):
    B, L, _ = x_prompt.shape
    NS = x_sample.shape[0]
    assert g_mix.shape[0] == 1, "single-layer trunk"
    gmix = g_mix[0][None, :]
    win = w_in[0].astype(BF16)
    wpool = w_pool[0].astype(BF16)
    pscale = pool_scale[0][None, :]
    gon = g_onorm[0][None, :]
    wa = w_a[0].astype(BF16)
    wb = w_b[0].astype(BF16)
    wout = w_out[0].astype(BF16)
    gmlp = g_mlp[0][None, :]
    wup = w_up[0].astype(BF16)
    wdown = w_down[0].astype(BF16)
    gfin = g_final[None, :]

    x1_p, pool_p, hgrn_p = _prompt_mixer(
        x_prompt, meta_tokens, gmix, win, wpool, pscale, hgrn_lb_logits, gon, wa, wb, wout, tile=256)
    y_p = _mlp(x1_p.reshape(B * L, D_MODEL), gmlp, wup, wdown, gfin, tm=512)

    xs = x_sample.reshape(NS, D_MODEL)
    poolT = jnp.swapaxes(state_pool[0], 0, 1)
    x1_s, u_s, hgrn_s = _decode_mixer(
        xs, poolT, state_hgrn[0], gmix, win, wpool, pscale, hgrn_lb_logits, gon, wa, wb, wout, ts=16)
    y_s = _mlp(x1_s, gmlp, wup, wdown, gfin, tm=NS)
    pool_s = jnp.concatenate([state_pool[0][:, 1:, :], u_s[:, None, :]], axis=1)

    return (y_p.reshape(B, L, D_MODEL), y_s.reshape(NS, 1, D_MODEL),
            pool_p[None], hgrn_p[None], pool_s[None], hgrn_s[None])
```

```python
import functools

import jax
import jax.numpy as jnp
from jax import lax
from jax.experimental import pallas as pl
from jax.experimental.pallas import tpu as pltpu

D_MODEL = 1024
N_META = 16
POOL_WIDTH = 512
POOL_WINDOWS = (2, 4, 8, 16)
POOL_GROUP = 128
POOL_MAXW = 16
POOL_BUF = 15
HEADS = 4
DK = 128
DV = 128
KTOT = 512
HWIDTH = 512
D_FF = 4096
EPS = 1e-6
N_IN = 4608
O_U, O_Q, O_F, O_V, O_OG, O_GA, O_GB = 0, 512, 1024, 1536, 2048, 2560, 3584

SUB = 16
CHUNK = 128
GATE_PIECE = 256
LEVEL_HALVES = tuple(1 << i for i in range(CHUNK.bit_length() - 1))
VMEM_LIMIT = 56 * 1024 * 1024

F32 = jnp.float32
BF16 = jnp.bfloat16


def _rms(x, g):
    return x * lax.rsqrt(jnp.mean(x * x, axis=-1, keepdims=True) + EPS) * g


def _sigmoid(x):
    return 0.5 * jnp.tanh(0.5 * x) + 0.5


def _silu(x):
    hx = 0.5 * x
    return hx * jnp.tanh(hx) + hx


def _dot(a, b):
    return jnp.dot(a, b, preferred_element_type=F32)


def _dot_nt(a, b):
    return lax.dot_general(a, b, (((1,), (1,)), ((), ())), preferred_element_type=F32)


def _dot_tn(a, b):
    return lax.dot_general(a, b, (((0,), (0,)), ((), ())), preferred_element_type=F32)


def _lower_bound(lb_logits):
    m = jnp.max(lb_logits, axis=0, keepdims=True)
    e = jnp.exp(lb_logits - m)
    return e[0:1, :] / jnp.sum(e, axis=0, keepdims=True)


def _gates(zq, zf, lb):
    q = _silu(zq)
    fg = lb + (1.0 - lb) * _sigmoid(zf)
    return q, fg


def _cumsum_rows(g):
    n = g.shape[0]
    r = lax.broadcasted_iota(jnp.int32, (n, n), 0)
    c = lax.broadcasted_iota(jnp.int32, (n, n), 1)
    tril = (r >= c).astype(BF16)
    hi = g.astype(BF16)
    r1 = g - hi.astype(F32)
    mid = r1.astype(BF16)
    lo = (r1 - mid.astype(F32)).astype(BF16)
    return _dot(tril, hi) + _dot(tril, mid) + _dot(tril, lo)


def _level_map(n):
    t = lax.broadcasted_iota(jnp.int32, (n, n), 0)
    s = lax.broadcasted_iota(jnp.int32, (n, n), 1)
    x = t ^ s
    lvl = jnp.full((n, n), -1, jnp.int32)
    for li in range(n.bit_length() - 1):
        lvl = jnp.where((x >> li) == 1, li, lvl)
    return jnp.where(t > s, lvl, -1)


def _level_weights(m, r0, Gg, qg, kg, fgg, g_sc, row8, row16):
    if m == 1:
        return jnp.where((row16 & 1) != 0, qg * fgg, kg)
    if m >= SUB:
        blk = (r0 // (2 * m)) * (2 * m)
        ref = g_sc[blk + m - 1:blk + m, :]
        sel = qg if (r0 & m) else kg
    elif m == SUB // 2:
        ref = g_sc[r0 + m - 1:r0 + m, :]
        sel = jnp.where(row16 >= m, qg, kg)
    else:
        halves = []
        for rb in (r0, r0 + 8):
            if m == 4:
                halves.append(jnp.broadcast_to(g_sc[rb + 3:rb + 4, :], (8, KTOT)))
            else:
                halves.append(jnp.where(row8 >= 4, g_sc[rb + 5:rb + 6, :], g_sc[rb + 1:rb + 2, :]))
        ref = jnp.concatenate(halves, axis=0)
        sel = jnp.where((row16 & m) != 0, qg, kg)
    return jnp.exp(-jnp.abs(Gg - ref)) * sel


def _recurrence_chunk(base, z_sc, lb, lvl, g_sc, w_sc, qb_sc, kb_sc, o_sc, st_sc, interleave):
    rs = slice(base, base + CHUNK)
    q, fg = _gates(z_sc[rs, O_Q:O_Q + KTOT], z_sc[rs, O_F:O_F + KTOT], lb)
    k = 1.0 - fg
    G = _cumsum_rows(jnp.log(fg))
    g_sc[...] = G
    g_last = G[CHUNK - 1:CHUNK, :]
    v = z_sc[rs, O_V:O_V + HWIDTH]
    v_bf = v.astype(BF16)
    row8 = lax.broadcasted_iota(jnp.int32, (8, 1), 0)
    row16 = lax.broadcasted_iota(jnp.int32, (SUB, 1), 0)
    for j in range(CHUNK // SUB):
        r0 = j * SUB
        gs = slice(r0, r0 + SUB)
        Gg, qg, kg, fgg = G[gs], q[gs], k[gs], fg[gs]
        for li, m in enumerate(LEVEL_HALVES):
            w_sc[li, gs, :] = _level_weights(m, r0, Gg, qg, kg, fgg, g_sc, row8, row16).astype(BF16)
        qb_sc[gs, :] = (qg * jnp.exp(Gg)).astype(BF16)
        kb_sc[gs, :] = (kg * jnp.exp(g_last - Gg)).astype(BF16)
        interleave()
    dec = jnp.exp(g_last)
    qk = q * k
    for hd in range(HEADS):
        sl = slice(hd * DK, (hd + 1) * DK)
        A = jnp.zeros((CHUNK, CHUNK), F32)
        for li in range(len(LEVEL_HALVES)):
            w = w_sc[li, :, sl]
            A = jnp.where(lvl == li, _dot_nt(w, w), A)
        st = st_sc[hd]
        o = (_dot(A.astype(BF16), v_bf[:, sl])
             + _dot_nt(qb_sc[:, sl], st.astype(BF16))
             + jnp.sum(qk[:, sl], axis=-1, keepdims=True) * v[:, sl])
        o_sc[rs, sl] = o
        st_sc[hd] = st * dec[:, sl] + _dot_tn(v_bf[:, sl], kb_sc[:, sl])


def _prompt_mixer_kernel(x_ref, meta_ref, gmix_ref, win_ref, wpool_ref, pscale_ref, lbl_ref,
                         gon_ref, wa_ref, wb_ref, wout_ref,
                         x1_ref, pool_ref, hgrn_ref,
                         z_sc, ubuf, g_sc, w_sc, qb_sc, kb_sc, o_sc, st_sc, st_meta, u_meta, *, tile):
    b = pl.program_id(0)
    t = pl.program_id(1)
    nt = pl.num_programs(1)
    lb = _lower_bound(lbl_ref[...])
    gmix = gmix_ref[...]

    @pl.when((b == 0) & (t == 0))
    def _meta():
        hm = _rms(meta_ref[...], gmix).astype(BF16)
        zm = _dot(hm, win_ref[:, 0:O_OG])
        u_meta[...] = zm[:, O_U:O_U + POOL_WIDTH]
        _, fg = _gates(zm[:, O_Q:O_Q + KTOT], zm[:, O_F:O_F + KTOT], lb)
        G = _cumsum_rows(jnp.log(fg))
        kt = (1.0 - fg) * jnp.exp(G[N_META - 1:N_META, :] - G)
        v = zm[:, O_V:O_V + HWIDTH]
        for h in range(HEADS):
            sl = slice(h * DK, (h + 1) * DK)
            st_meta[h] = _dot_tn(v[:, sl].astype(BF16), kt[:, sl].astype(BF16))

    @pl.when(t == 0)
    def _init():
        st_sc[...] = st_meta[...]
        ubuf[0:POOL_MAXW, :] = u_meta[...]

    x = x_ref[0]
    h = _rms(x, gmix).astype(BF16)
    z_sc[:, 0:O_OG] = _dot(h, win_ref[:, 0:O_OG])

    def gate_piece(c0):
        z_sc[:, c0:c0 + GATE_PIECE] = _dot(h, win_ref[:, c0:c0 + GATE_PIECE])
    pending = [functools.partial(gate_piece, c0) for c0 in range(O_OG, N_IN, GATE_PIECE)]

    def interleave():
        if pending:
            pending.pop(0)()

    u = z_sc[:, O_U:O_U + POOL_WIDTH]
    ubuf[POOL_MAXW:POOL_MAXW + tile, :] = u
    ya_parts = []
    for gi, w in enumerate(POOL_WINDOWS):
        cs = slice(gi * POOL_GROUP, (gi + 1) * POOL_GROUP)
        wsum = ubuf[:, cs]
        span = 1
        while span < w:
            wsum = wsum + pltpu.roll(wsum, span, 0)
            span *= 2
        pooled = wsum[POOL_MAXW:, :] / float(w) - u[:, cs]
        ya_parts.append(_dot(pooled.astype(BF16), wpool_ref[gi]))
    ya_pre = jnp.concatenate(ya_parts, axis=-1) * pscale_ref[...]
    ya = _dot(ya_pre.astype(BF16), wa_ref[...])

    ubuf[0:POOL_MAXW, :] = ubuf[tile:tile + POOL_MAXW, :]

    lvl = _level_map(CHUNK)
    for c in range(tile // CHUNK):
        _recurrence_chunk(c * CHUNK, z_sc, lb, lvl, g_sc, w_sc, qb_sc, kb_sc, o_sc, st_sc, interleave)
    while pending:
        interleave()

    gon = gon_ref[...]
    o_parts = []
    for hd in range(HEADS):
        sl = slice(hd * DV, (hd + 1) * DV)
        og = z_sc[:, O_OG + hd * DV:O_OG + (hd + 1) * DV]
        o_parts.append(_rms(o_sc[:, sl], gon) * _silu(og))
    yb = _dot(jnp.concatenate(o_parts, axis=-1).astype(BF16), wb_ref[...])

    m = (_sigmoid(z_sc[:, O_GA:O_GA + D_MODEL]) * ya
         + _sigmoid(z_sc[:, O_GB:O_GB + D_MODEL]) * yb)
    x1_ref[0] = x + _dot(m.astype(BF16), wout_ref[...])

    @pl.when(t == nt - 1)
    def _state_out():
        pool_ref[0] = ubuf[1:POOL_MAXW, :]
        for hd in range(HEADS):
            hgrn_ref[0, hd] = st_sc[hd].T


def _const_spec(shape):
    nd = len(shape)
    return pl.BlockSpec(shape, lambda *_: (0,) * nd, pipeline_mode=pl.Buffered(1))


def _prompt_mixer(x, meta, gmix, win, wpool, pscale, lbl, gon, wa, wb, wout, *, tile):
    B, L, _ = x.shape
    nt = L // tile
    kern = functools.partial(_prompt_mixer_kernel, tile=tile)
    return pl.pallas_call(
        kern,
        grid=(B, nt),
        in_specs=[
            pl.BlockSpec((1, tile, D_MODEL), lambda b, t: (b, t, 0)),
            _const_spec((N_META, D_MODEL)),
            _const_spec((1, D_MODEL)),
            _const_spec((D_MODEL, N_IN)),
            _const_spec((len(POOL_WINDOWS), POOL_GROUP, POOL_GROUP)),
            _const_spec((1, POOL_WIDTH)),
            _const_spec((2, KTOT)),
            _const_spec((1, DV)),
            _const_spec((POOL_WIDTH, D_MODEL)),
            _const_spec((HWIDTH, D_MODEL)),
            _const_spec((D_MODEL, D_MODEL)),
        ],
        out_specs=[
            pl.BlockSpec((1, tile, D_MODEL), lambda b, t: (b, t, 0)),
            pl.BlockSpec((1, POOL_BUF, POOL_WIDTH), lambda b, t: (b, 0, 0)),
            pl.BlockSpec((1, HEADS, DK, DV), lambda b, t: (b, 0, 0, 0)),
        ],
        out_shape=[
            jax.ShapeDtypeStruct((B, L, D_MODEL), F32),
            jax.ShapeDtypeStruct((B, POOL_BUF, POOL_WIDTH), F32),
            jax.ShapeDtypeStruct((B, HEADS, DK, DV), F32),
        ],
        scratch_shapes=[
            pltpu.VMEM((tile, N_IN), F32),
            pltpu.VMEM((tile + POOL_MAXW, POOL_WIDTH), F32),
            pltpu.VMEM((CHUNK, KTOT), F32),
            pltpu.VMEM((len(LEVEL_HALVES), CHUNK, KTOT), BF16),
            pltpu.VMEM((CHUNK, KTOT), BF16),
            pltpu.VMEM((CHUNK, KTOT), BF16),
            pltpu.VMEM((tile, HWIDTH), F32),
            pltpu.VMEM((HEADS, DV, DK), F32),
            pltpu.VMEM((HEADS, DV, DK), F32),
            pltpu.VMEM((N_META, POOL_WIDTH), F32),
        ],
        compiler_params=pltpu.CompilerParams(
            dimension_semantics=("arbitrary", "arbitrary"),
            vmem_limit_bytes=VMEM_LIMIT),
        name="prompt_mixer",
    )(x, meta, gmix, win, wpool, pscale, lbl, gon, wa, wb, wout)


def _decode_mixer_kernel(x_ref, poolT_ref, s_ref, gmix_ref, win_ref, wpool_ref, pscale_ref,
                         lbl_ref, gon_ref, wa_ref, wb_ref, wout_ref,
                         x1_ref, unew_ref, snew_ref,
                         z_sc, o_sc, *, ts):
    i = pl.program_id(0)
    n = pl.num_programs(0)
    lb = _lower_bound(lbl_ref[...])

    @pl.when(i == 0)
    def _in_proj():
        h = _rms(x_ref[...], gmix_ref[...]).astype(BF16)
        z_sc[...] = _dot(h, win_ref[...])

    r = pl.ds(pl.multiple_of(i * ts, ts), ts)
    q, fg = _gates(z_sc[r, O_Q:O_Q + KTOT], z_sc[r, O_F:O_F + KTOT], lb)
    k = 1.0 - fg
    v = z_sc[r, O_V:O_V + HWIDTH]
    row = lax.broadcasted_iota(jnp.int32, (ts, 1), 0)
    for hd in range(HEADS):
        sl = slice(hd * DK, (hd + 1) * DK)
        fT = fg[:, sl].T
        kT = k[:, sl].T
        qT = q[:, sl].T
        o = jnp.zeros((ts, DV), F32)
        for s in range(ts):
            s_new = fT[:, s:s + 1] * s_ref[s, hd] + kT[:, s:s + 1] * v[s:s + 1, sl]
            snew_ref[s, hd] = s_new
            o_row = jnp.sum(qT[:, s:s + 1] * s_new, axis=0, keepdims=True)
            o = jnp.where(row == s, o_row, o)
        o_sc[r, sl] = o

    @pl.when(i == n - 1)
    def _out_proj():
        x = x_ref[...]
        u = z_sc[:, O_U:O_U + POOL_WIDTH]
        unew_ref[...] = u
        ya_parts = []
        for gi, w in enumerate(POOL_WINDOWS):
            cs = slice(gi * POOL_GROUP, (gi + 1) * POOL_GROUP)
            wsum = u[:, cs]
            for j in range(1, w):
                wsum = wsum + poolT_ref[POOL_BUF - j, :, cs]
            pooled = wsum / float(w) - u[:, cs]
            ya_parts.append(_dot(pooled.astype(BF16), wpool_ref[gi]))
        ya_pre = jnp.concatenate(ya_parts, axis=-1) * pscale_ref[...]
        ya = _dot(ya_pre.astype(BF16), wa_ref[...])
        gon = gon_ref[...]
        o_parts = []
        for hd in range(HEADS):
            sl = slice(hd * DV, (hd + 1) * DV)
            og = z_sc[:, O_OG + hd * DV:O_OG + (hd + 1) * DV]
            o_parts.append(_rms(o_sc[:, sl], gon) * _silu(og))
        yb = _dot(jnp.concatenate(o_parts, axis=-1).astype(BF16), wb_ref[...])
        m = (_sigmoid(z_sc[:, O_GA:O_GA + D_MODEL]) * ya
             + _sigmoid(z_sc[:, O_GB:O_GB + D_MODEL]) * yb)
        x1_ref[...] = x + _dot(m.astype(BF16), wout_ref[...])


def _decode_mixer(x, poolT, state, gmix, win, wpool, pscale, lbl, gon, wa, wb, wout, *, ts):
    n = x.shape[0]
    kern = functools.partial(_decode_mixer_kernel, ts=ts)
    return pl.pallas_call(
        kern,
        grid=(n // ts,),
        in_specs=[
            _const_spec((n, D_MODEL)),
            _const_spec((POOL_BUF, n, POOL_WIDTH)),
            pl.BlockSpec((ts, HEADS, DK, DV), lambda i: (i, 0, 0, 0)),
            _const_spec((1, D_MODEL)),
            _const_spec((D_MODEL, N_IN)),
            _const_spec((len(POOL_WINDOWS), POOL_GROUP, POOL_GROUP)),
            _const_spec((1, POOL_WIDTH)),
            _const_spec((2, KTOT)),
            _const_spec((1, DV)),
            _const_spec((POOL_WIDTH, D_MODEL)),
            _const_spec((HWIDTH, D_MODEL)),
            _const_spec((D_MODEL, D_MODEL)),
        ],
        out_specs=[
            pl.BlockSpec((n, D_MODEL), lambda i: (0, 0)),
            pl.BlockSpec((n, POOL_WIDTH), lambda i: (0, 0)),
            pl.BlockSpec((ts, HEADS, DK, DV), lambda i: (i, 0, 0, 0)),
        ],
        out_shape=[
            jax.ShapeDtypeStruct((n, D_MODEL), F32),
            jax.ShapeDtypeStruct((n, POOL_WIDTH), F32),
            jax.ShapeDtypeStruct((n, HEADS, DK, DV), F32),
        ],
        scratch_shapes=[
            pltpu.VMEM((n, N_IN), F32),
            pltpu.VMEM((n, HWIDTH), F32),
        ],
        compiler_params=pltpu.CompilerParams(
            dimension_semantics=("arbitrary",),
            vmem_limit_bytes=VMEM_LIMIT),
        name="decode_mixer",
    )(x, poolT, state, gmix, win, wpool, pscale, lbl, gon, wa, wb, wout)


def _mlp_kernel(x_ref, gmlp_ref, wup_ref, wdown_ref, gfin_ref, y_ref, *, ff_chunk):
    x = x_ref[...]
    h = _rms(x, gmlp_ref[...]).astype(BF16)
    acc = x
    for c in range(D_FF // ff_chunk):
        cs = slice(c * ff_chunk, (c + 1) * ff_chunk)
        a = jnp.maximum(_dot(h, wup_ref[:, cs]), 0.0)
        acc = acc + _dot((a * a).astype(BF16), wdown_ref[cs, :])
    y_ref[...] = _rms(acc, gfin_ref[...])


def _mlp(x, gmlp, wup, wdown, gfin, *, tm, ff_chunk=1024):
    n = x.shape[0]
    kern = functools.partial(_mlp_kernel, ff_chunk=ff_chunk)
    return pl.pallas_call(
        kern,
        grid=(n // tm,),
        in_specs=[
            pl.BlockSpec((tm, D_MODEL), lambda i: (i, 0)),
            _const_spec((1, D_MODEL)),
            _const_spec((D_MODEL, D_FF)),
            _const_spec((D_FF, D_MODEL)),
            _const_spec((1, D_MODEL)),
        ],
        out_specs=pl.BlockSpec((tm, D_MODEL), lambda i: (i, 0)),
        out_shape=jax.ShapeDtypeStruct((n, D_MODEL), F32),
        compiler_params=pltpu.CompilerParams(
            dimension_semantics=("arbitrary",),
            vmem_limit_bytes=VMEM_LIMIT),
        name="channel_mlp",
    )(x, gmlp, wup, wdown, gfin)


def kernel(x_prompt, x_sample, state_pool, state_hgrn, meta_tokens, g_mix, w_in, w_pool, pool_scale,
           hgrn_lb_logits, g_onorm, w_a, w_b, w_out, g_mlp, w_up, w_down, g_final):
    B, L, _ = x_prompt.shape
    NS = x_sample.shape[0]
    assert g_mix.shape[0] == 1, "single-layer trunk"
    gmix = g_mix[0][None, :]
    win = w_in[0].astype(BF16)
    wpool = w_pool[0].astype(BF16)
    pscale = pool_scale[0][None, :]
    gon = g_onorm[0][None, :]
    wa = w_a[0].astype(BF16)
    wb = w_b[0].astype(BF16)
    wout = w_out[0].astype(BF16)
    gmlp = g_mlp[0][None, :]
    wup = w_up[0].astype(BF16)
    wdown = w_down[0].astype(BF16)
    gfin = g_final[None, :]

    x1_p, pool_p, hgrn_p = _prompt_mixer(
        x_prompt, meta_tokens, gmix, win, wpool, pscale, hgrn_lb_logits, gon, wa, wb, wout, tile=256)
    y_p = _mlp(x1_p.reshape(B * L, D_MODEL), gmlp, wup, wdown, gfin, tm=512)

    xs = x_sample.reshape(NS, D_MODEL)
    poolT = jnp.swapaxes(state_pool[0], 0, 1)
    x1_s, u_s, hgrn_s = _decode_mixer(
        xs, poolT, state_hgrn[0], gmix, win, wpool, pscale, hgrn_lb_logits, gon, wa, wb, wout, ts=16)
    y_s = _mlp(x1_s, gmlp, wup, wdown, gfin, tm=NS)
    pool_s = jnp.concatenate([state_pool[0][:, 1:, :], u_s[:, None, :]], axis=1)

    return (y_p.reshape(B, L, D_MODEL), y_s.reshape(NS, 1, D_MODEL),
            pool_p[None], hgrn_p[None], pool_s[None], hgrn_s[None])
```

```python
import functools

import jax
import jax.numpy as jnp
from jax import lax
from jax.experimental import pallas as pl
from jax.experimental.pallas import tpu as pltpu

D_MODEL = 1024
N_META = 16
POOL_WIDTH = 512
POOL_WINDOWS = (2, 4, 8, 16)
POOL_GROUP = 128
POOL_MAXW = 16
POOL_BUF = 15
HEADS = 4
DK = 128
DV = 128
KTOT = 512
HWIDTH = 512
D_FF = 4096
EPS = 1e-6
N_IN = 4608
O_U, O_Q, O_F, O_V, O_OG, O_GA, O_GB = 0, 512, 1024, 1536, 2048, 2560, 3584

SUB = 16
CHUNK = 128
GATE_PIECE = 256
GATE_ROWS = 256
LEVEL_HALVES = tuple(1 << i for i in range(CHUNK.bit_length() - 1))
VMEM_LIMIT = 56 * 1024 * 1024

F32 = jnp.float32
BF16 = jnp.bfloat16


def _rms(x, g):
    return x * lax.rsqrt(jnp.mean(x * x, axis=-1, keepdims=True) + EPS) * g


def _sigmoid(x):
    return 0.5 * jnp.tanh(0.5 * x) + 0.5


def _silu(x):
    hx = 0.5 * x
    return hx * jnp.tanh(hx) + hx


def _dot(a, b):
    return jnp.dot(a, b, preferred_element_type=F32)


def _dot_nt(a, b):
    return lax.dot_general(a, b, (((1,), (1,)), ((), ())), preferred_element_type=F32)


def _dot_tn(a, b):
    return lax.dot_general(a, b, (((0,), (0,)), ((), ())), preferred_element_type=F32)


def _lower_bound(lb_logits):
    m = jnp.max(lb_logits, axis=0, keepdims=True)
    e = jnp.exp(lb_logits - m)
    return e[0:1, :] / jnp.sum(e, axis=0, keepdims=True)


def _gates(zq, zf, lb):
    q = _silu(zq)
    fg = lb + (1.0 - lb) * _sigmoid(zf)
    return q, fg


def _cumsum_rows(g):
    n = g.shape[0]
    r = lax.broadcasted_iota(jnp.int32, (n, n), 0)
    c = lax.broadcasted_iota(jnp.int32, (n, n), 1)
    tril = (r >= c).astype(BF16)
    hi = g.astype(BF16)
    r1 = g - hi.astype(F32)
    mid = r1.astype(BF16)
    lo = (r1 - mid.astype(F32)).astype(BF16)
    return _dot(tril, hi) + _dot(tril, mid) + _dot(tril, lo)


def _level_map(n):
    t = lax.broadcasted_iota(jnp.int32, (n, n), 0)
    s = lax.broadcasted_iota(jnp.int32, (n, n), 1)
    x = t ^ s
    lvl = jnp.full((n, n), -1, jnp.int32)
    for li in range(n.bit_length() - 1):
        lvl = jnp.where((x >> li) == 1, li, lvl)
    return jnp.where(t > s, lvl, -1)


def _level_weights(m, r0, Gg, qg, kg, fgg, g_sc, row8, row16):
    if m == 1:
        return jnp.where((row16 & 1) != 0, qg * fgg, kg)
    if m >= SUB:
        blk = (r0 // (2 * m)) * (2 * m)
        ref = g_sc[blk + m - 1:blk + m, :]
        sel = qg if (r0 & m) else kg
    elif m == SUB // 2:
        ref = g_sc[r0 + m - 1:r0 + m, :]
        sel = jnp.where(row16 >= m, qg, kg)
    else:
        halves = []
        for rb in (r0, r0 + 8):
            if m == 4:
                halves.append(jnp.broadcast_to(g_sc[rb + 3:rb + 4, :], (8, KTOT)))
            else:
                halves.append(jnp.where(row8 >= 4, g_sc[rb + 5:rb + 6, :], g_sc[rb + 1:rb + 2, :]))
        ref = jnp.concatenate(halves, axis=0)
        sel = jnp.where((row16 & m) != 0, qg, kg)
    return jnp.exp(-jnp.abs(Gg - ref)) * sel


def _gates_stage(base, par, z_sc, lb, q_sc, k_sc, fg_sc, g_sc, od_sc, dec_sc):
    rs = slice(base, base + CHUNK)
    q, fg = _gates(z_sc[rs, O_Q:O_Q + KTOT], z_sc[rs, O_F:O_F + KTOT], lb)
    k = 1.0 - fg
    G = _cumsum_rows(jnp.log(fg))
    q_sc[...] = q
    k_sc[...] = k
    fg_sc[...] = fg
    g_sc[...] = G
    v = z_sc[rs, O_V:O_V + HWIDTH]
    qk = q * k
    for hd in range(HEADS):
        sl = slice(hd * DK, (hd + 1) * DK)
        od_sc[par, :, sl] = jnp.sum(qk[:, sl], axis=-1, keepdims=True) * v[:, sl]
    dec_sc[par] = jnp.broadcast_to(jnp.exp(G[CHUNK - 1:CHUNK, :]), (8, KTOT))


def _group_stage(j, par, q_sc, k_sc, fg_sc, g_sc, w_sc, qb_sc, kb_sc):
    row8 = lax.broadcasted_iota(jnp.int32, (8, 1), 0)
    row16 = lax.broadcasted_iota(jnp.int32, (SUB, 1), 0)
    r0 = j * SUB
    gs = slice(r0, r0 + SUB)
    Gg, qg, kg, fgg = g_sc[gs, :], q_sc[gs, :], k_sc[gs, :], fg_sc[gs, :]
    g_last = g_sc[CHUNK - 1:CHUNK, :]
    for li, m in enumerate(LEVEL_HALVES):
        w_sc[par, li, gs, :] = _level_weights(m, r0, Gg, qg, kg, fgg, g_sc, row8, row16).astype(BF16)
    qb_sc[par, gs, :] = (qg * jnp.exp(Gg)).astype(BF16)
    kb_sc[par, gs, :] = (kg * jnp.exp(g_last - Gg)).astype(BF16)


def _head_stage(base, hd, par, z_sc, lvl, w_sc, qb_sc, kb_sc, od_sc, dec_sc, o_sc, st_sc):
    rs = slice(base, base + CHUNK)
    sl = slice(hd * DK, (hd + 1) * DK)
    A = jnp.zeros((CHUNK, CHUNK), F32)
    for li in range(len(LEVEL_HALVES)):
        w = w_sc[par, li, :, sl]
        A = jnp.where(lvl == li, _dot_nt(w, w), A)
    v_bf = z_sc[rs, O_V + hd * DV:O_V + (hd + 1) * DV].astype(BF16)
    st = st_sc[hd]
    o_sc[rs, sl] = (_dot(A.astype(BF16), v_bf)
                    + _dot_nt(qb_sc[par, :, sl], st.astype(BF16))
                    + od_sc[par, :, sl])
    st_sc[hd] = st * dec_sc[par, 0:1, sl] + _dot_tn(v_bf, kb_sc[par, :, sl])


def _prompt_mixer_kernel(x_ref, meta_ref, gmix_ref, win_ref, wpool_ref, pscale_ref, lbl_ref,
                         gon_ref, wa_ref, wb_ref, wout_ref,
                         x1_ref, pool_ref, hgrn_ref,
                         z_sc, h_sc, ya_sc, ubuf, q_sc, k_sc, fg_sc, g_sc, od_sc, dec_sc,
                         w_sc, qb_sc, kb_sc, o_sc, st_sc, st_meta,
                         u_meta, *, tile):
    b = pl.program_id(0)
    t = pl.program_id(1)
    nt = pl.num_programs(1)
    lb = _lower_bound(lbl_ref[...])
    gmix = gmix_ref[...]

    @pl.when((b == 0) & (t == 0))
    def _meta():
        hm = _rms(meta_ref[...], gmix).astype(BF16)
        zm = _dot(hm, win_ref[:, 0:O_OG])
        u_meta[...] = zm[:, O_U:O_U + POOL_WIDTH]
        _, fg = _gates(zm[:, O_Q:O_Q + KTOT], zm[:, O_F:O_F + KTOT], lb)
        G = _cumsum_rows(jnp.log(fg))
        kt = (1.0 - fg) * jnp.exp(G[N_META - 1:N_META, :] - G)
        v = zm[:, O_V:O_V + HWIDTH]
        for h in range(HEADS):
            sl = slice(h * DK, (h + 1) * DK)
            st_meta[h] = _dot_tn(v[:, sl].astype(BF16), kt[:, sl].astype(BF16))

    @pl.when(t == 0)
    def _init():
        st_sc[...] = st_meta[...]
        ubuf[0:POOL_MAXW, :] = u_meta[...]

    n_chunks = tile // CHUNK
    h_sc[...] = _rms(x_ref[0], gmix).astype(BF16)
    z_sc[:, 0:O_OG] = _dot(h_sc[...], win_ref[:, 0:O_OG])

    u = z_sc[:, O_U:O_U + POOL_WIDTH]
    ubuf[POOL_MAXW:POOL_MAXW + tile, :] = u
    ya_parts = []
    for gi, w in enumerate(POOL_WINDOWS):
        cs = slice(gi * POOL_GROUP, (gi + 1) * POOL_GROUP)
        wsum = ubuf[:, cs]
        span = 1
        while span < w:
            wsum = wsum + pltpu.roll(wsum, span, 0)
            span *= 2
        pooled = wsum[POOL_MAXW:, :] / float(w) - u[:, cs]
        ya_parts.append(_dot(pooled.astype(BF16), wpool_ref[gi]))
    ya_pre = jnp.concatenate(ya_parts, axis=-1) * pscale_ref[...]
    ya_sc[...] = _dot(ya_pre.astype(BF16), wa_ref[...])
    ubuf[0:POOL_MAXW, :] = ubuf[tile:tile + POOL_MAXW, :]

    def gate_piece(r0, c0):
        z_sc[r0:r0 + GATE_ROWS, c0:c0 + GATE_PIECE] = _dot(
            h_sc[r0:r0 + GATE_ROWS, :], win_ref[:, c0:c0 + GATE_PIECE])
    pending = [functools.partial(gate_piece, r0, c0)
               for c0 in range(O_OG, N_IN, GATE_PIECE) for r0 in range(0, tile, GATE_ROWS)]
    n_pieces = len(pending)
    n_groups = CHUNK // SUB
    n_slots = n_chunks * (n_groups + HEADS)
    slot = [0]

    def interleave():
        slot[0] += 1
        while len(pending) > n_pieces - (-(-slot[0] * n_pieces // n_slots)):
            pending.pop(0)()

    def gates(c):
        _gates_stage(c * CHUNK, c % 2, z_sc, lb, q_sc, k_sc, fg_sc, g_sc, od_sc, dec_sc)

    def group(c, j):
        _group_stage(j, c % 2, q_sc, k_sc, fg_sc, g_sc, w_sc, qb_sc, kb_sc)
        interleave()

    def head(c, hd):
        _head_stage(c * CHUNK, hd, c % 2, z_sc, lvl, w_sc, qb_sc, kb_sc, od_sc, dec_sc, o_sc, st_sc)
        interleave()

    lvl = _level_map(CHUNK)
    gates(0)
    for j in range(n_groups):
        group(0, j)
    for c in range(n_chunks):
        if c + 1 < n_chunks:
            gates(c + 1)
        for hd in range(HEADS):
            head(c, hd)
            if c + 1 < n_chunks:
                for j in range(hd * n_groups // HEADS, (hd + 1) * n_groups // HEADS):
                    group(c + 1, j)
    assert not pending and slot[0] == n_slots

    gon = gon_ref[...]
    o_parts = []
    for hd in range(HEADS):
        sl = slice(hd * DV, (hd + 1) * DV)
        og = z_sc[:, O_OG + hd * DV:O_OG + (hd + 1) * DV]
        o_parts.append(_rms(o_sc[:, sl], gon) * _silu(og))
    yb = _dot(jnp.concatenate(o_parts, axis=-1).astype(BF16), wb_ref[...])
    m = (_sigmoid(z_sc[:, O_GA:O_GA + D_MODEL]) * ya_sc[...]
         + _sigmoid(z_sc[:, O_GB:O_GB + D_MODEL]) * yb)
    x1_ref[0] = x_ref[0] + _dot(m.astype(BF16), wout_ref[...])

    @pl.when(t == nt - 1)
    def _state_out():
        pool_ref[0] = ubuf[1:POOL_MAXW, :]
        for hd in range(HEADS):
            hgrn_ref[0, hd] = st_sc[hd].T


def _const_spec(shape):
    nd = len(shape)
    return pl.BlockSpec(shape, lambda *_: (0,) * nd, pipeline_mode=pl.Buffered(1))


def _prompt_mixer(x, meta, gmix, win, wpool, pscale, lbl, gon, wa, wb, wout, *, tile):
    B, L, _ = x.shape
    nt = L // tile
    kern = functools.partial(_prompt_mixer_kernel, tile=tile)
    return pl.pallas_call(
        kern,
        grid=(B, nt),
        in_specs=[
            pl.BlockSpec((1, tile, D_MODEL), lambda b, t: (b, t, 0)),
            _const_spec((N_META, D_MODEL)),
            _const_spec((1, D_MODEL)),
            _const_spec((D_MODEL, N_IN)),
            _const_spec((len(POOL_WINDOWS), POOL_GROUP, POOL_GROUP)),
            _const_spec((1, POOL_WIDTH)),
            _const_spec((2, KTOT)),
            _const_spec((1, DV)),
            _const_spec((POOL_WIDTH, D_MODEL)),
            _const_spec((HWIDTH, D_MODEL)),
            _const_spec((D_MODEL, D_MODEL)),
        ],
        out_specs=[
            pl.BlockSpec((1, tile, D_MODEL), lambda b, t: (b, t, 0)),
            pl.BlockSpec((1, POOL_BUF, POOL_WIDTH), lambda b, t: (b, 0, 0)),
            pl.BlockSpec((1, HEADS, DK, DV), lambda b, t: (b, 0, 0, 0)),
        ],
        out_shape=[
            jax.ShapeDtypeStruct((B, L, D_MODEL), F32),
            jax.ShapeDtypeStruct((B, POOL_BUF, POOL_WIDTH), F32),
            jax.ShapeDtypeStruct((B, HEADS, DK, DV), F32),
        ],
        scratch_shapes=[
            pltpu.VMEM((tile, N_IN), F32),
            pltpu.VMEM((tile, D_MODEL), BF16),
            pltpu.VMEM((tile, D_MODEL), F32),
            pltpu.VMEM((tile + POOL_MAXW, POOL_WIDTH), F32),
            pltpu.VMEM((CHUNK, KTOT), F32),
            pltpu.VMEM((CHUNK, KTOT), F32),
            pltpu.VMEM((CHUNK, KTOT), F32),
            pltpu.VMEM((CHUNK, KTOT), F32),
            pltpu.VMEM((2, CHUNK, HWIDTH), F32),
            pltpu.VMEM((2, 8, KTOT), F32),
            pltpu.VMEM((2, len(LEVEL_HALVES), CHUNK, KTOT), BF16),
            pltpu.VMEM((2, CHUNK, KTOT), BF16),
            pltpu.VMEM((2, CHUNK, KTOT), BF16),
            pltpu.VMEM((tile, HWIDTH), F32),
            pltpu.VMEM((HEADS, DV, DK), F32),
            pltpu.VMEM((HEADS, DV, DK), F32),
            pltpu.VMEM((N_META, POOL_WIDTH), F32),
        ],
        compiler_params=pltpu.CompilerParams(
            dimension_semantics=("arbitrary", "arbitrary"),
            vmem_limit_bytes=VMEM_LIMIT),
        name="prompt_mixer",
    )(x, meta, gmix, win, wpool, pscale, lbl, gon, wa, wb, wout)


def _decode_mixer_kernel(x_ref, poolT_ref, s_ref, gmix_ref, win_ref, wpool_ref, pscale_ref,
                         lbl_ref, gon_ref, wa_ref, wb_ref, wout_ref,
                         x1_ref, unew_ref, snew_ref,
                         z_sc, o_sc, *, ts):
    i = pl.program_id(0)
    n = pl.num_programs(0)
    lb = _lower_bound(lbl_ref[...])

    @pl.when(i == 0)
    def _in_proj():
        h = _rms(x_ref[...], gmix_ref[...]).astype(BF16)
        z_sc[...] = _dot(h, win_ref[...])

    r = pl.ds(pl.multiple_of(i * ts, ts), ts)
    q, fg = _gates(z_sc[r, O_Q:O_Q + KTOT], z_sc[r, O_F:O_F + KTOT], lb)
    k = 1.0 - fg
    v = z_sc[r, O_V:O_V + HWIDTH]
    row = lax.broadcasted_iota(jnp.int32, (ts, 1), 0)
    for hd in range(HEADS):
        sl = slice(hd * DK, (hd + 1) * DK)
        fT = fg[:, sl].T
        kT = k[:, sl].T
        qT = q[:, sl].T
        o = jnp.zeros((ts, DV), F32)
        for s in range(ts):
            s_new = fT[:, s:s + 1] * s_ref[s, hd] + kT[:, s:s + 1] * v[s:s + 1, sl]
            snew_ref[s, hd] = s_new
            o_row = jnp.sum(qT[:, s:s + 1] * s_new, axis=0, keepdims=True)
            o = jnp.where(row == s, o_row, o)
        o_sc[r, sl] = o

    @pl.when(i == n - 1)
    def _out_proj():
        x = x_ref[...]
        u = z_sc[:, O_U:O_U + POOL_WIDTH]
        unew_ref[...] = u
        ya_parts = []
        for gi, w in enumerate(POOL_WINDOWS):
            cs = slice(gi * POOL_GROUP, (gi + 1) * POOL_GROUP)
            wsum = u[:, cs]
            for j in range(1, w):
                wsum = wsum + poolT_ref[POOL_BUF - j, :, cs]
            pooled = wsum / float(w) - u[:, cs]
            ya_parts.append(_dot(pooled.astype(BF16), wpool_ref[gi]))
        ya_pre = jnp.concatenate(ya_parts, axis=-1) * pscale_ref[...]
        ya = _dot(ya_pre.astype(BF16), wa_ref[...])
        gon = gon_ref[...]
        o_parts = []
        for hd in range(HEADS):
            sl = slice(hd * DV, (hd + 1) * DV)
            og = z_sc[:, O_OG + hd * DV:O_OG + (hd + 1) * DV]
            o_parts.append(_rms(o_sc[:, sl], gon) * _silu(og))
        yb = _dot(jnp.concatenate(o_parts, axis=-1).astype(BF16), wb_ref[...])
        m = (_sigmoid(z_sc[:, O_GA:O_GA + D_MODEL]) * ya
             + _sigmoid(z_sc[:, O_GB:O_GB + D_MODEL]) * yb)
        x1_ref[...] = x + _dot(m.astype(BF16), wout_ref[...])


def _decode_mixer(x, poolT, state, gmix, win, wpool, pscale, lbl, gon, wa, wb, wout, *, ts):
    n = x.shape[0]
    kern = functools.partial(_decode_mixer_kernel, ts=ts)
    return pl.pallas_call(
        kern,
        grid=(n // ts,),
        in_specs=[
            _const_spec((n, D_MODEL)),
            _const_spec((POOL_BUF, n, POOL_WIDTH)),
            pl.BlockSpec((ts, HEADS, DK, DV), lambda i: (i, 0, 0, 0)),
            _const_spec((1, D_MODEL)),
            _const_spec((D_MODEL, N_IN)),
            _const_spec((len(POOL_WINDOWS), POOL_GROUP, POOL_GROUP)),
            _const_spec((1, POOL_WIDTH)),
            _const_spec((2, KTOT)),
            _const_spec((1, DV)),
            _const_spec((POOL_WIDTH, D_MODEL)),
            _const_spec((HWIDTH, D_MODEL)),
            _const_spec((D_MODEL, D_MODEL)),
        ],
        out_specs=[
            pl.BlockSpec((n, D_MODEL), lambda i: (0, 0)),
            pl.BlockSpec((n, POOL_WIDTH), lambda i: (0, 0)),
            pl.BlockSpec((ts, HEADS, DK, DV), lambda i: (i, 0, 0, 0)),
        ],
        out_shape=[
            jax.ShapeDtypeStruct((n, D_MODEL), F32),
            jax.ShapeDtypeStruct((n, POOL_WIDTH), F32),
            jax.ShapeDtypeStruct((n, HEADS, DK, DV), F32),
        ],
        scratch_shapes=[
            pltpu.VMEM((n, N_IN), F32),
            pltpu.VMEM((n, HWIDTH), F32),
        ],
        compiler_params=pltpu.CompilerParams(
            dimension_semantics=("arbitrary",),
            vmem_limit_bytes=VMEM_LIMIT),
        name="decode_mixer",
    )(x, poolT, state, gmix, win, wpool, pscale, lbl, gon, wa, wb, wout)


def _mlp_kernel(x_ref, gmlp_ref, wup_ref, wdown_ref, gfin_ref, y_ref, *, ff_chunk):
    x = x_ref[...]
    h = _rms(x, gmlp_ref[...]).astype(BF16)
    acc = x
    for c in range(D_FF // ff_chunk):
        cs = slice(c * ff_chunk, (c + 1) * ff_chunk)
        a = jnp.maximum(_dot(h, wup_ref[:, cs]), 0.0)
        acc = acc + _dot((a * a).astype(BF16), wdown_ref[cs, :])
    y_ref[...] = _rms(acc, gfin_ref[...])


def _mlp(x, gmlp, wup, wdown, gfin, *, tm, ff_chunk=1024):
    n = x.shape[0]
    kern = functools.partial(_mlp_kernel, ff_chunk=ff_chunk)
    return pl.pallas_call(
        kern,
        grid=(n // tm,),
        in_specs=[
            pl.BlockSpec((tm, D_MODEL), lambda i: (i, 0)),
            _const_spec((1, D_MODEL)),
            _const_spec((D_MODEL, D_FF)),
            _const_spec((D_FF, D_MODEL)),
            _const_spec((1, D_MODEL)),
        ],
        out_specs=pl.BlockSpec((tm, D_MODEL), lambda i: (i, 0)),
        out_shape=jax.ShapeDtypeStruct((n, D_MODEL), F32),
        compiler_params=pltpu.CompilerParams(
            dimension_semantics=("arbitrary",),
            vmem_limit_bytes=VMEM_LIMIT),
        name="channel_mlp",
    )(x, gmlp, wup, wdown, gfin)


def kernel(x_prompt, x_sample, state_pool, state_hgrn, meta_tokens, g_mix, w_in, w_pool, pool_scale,
           hgrn_lb_logits, g_onorm, w_a, w_b, w_out, g_mlp, w_up, w_down, g_final):
    B, L, _ = x_prompt.shape
    NS = x_sample.shape[0]
    assert g_mix.shape[0] == 1, "single-layer trunk"
    gmix = g_mix[0][None, :]
    win = w_in[0].astype(BF16)
    wpool = w_pool[0].astype(BF16)
    pscale = pool_scale[0][None, :]
    gon = g_onorm[0][None, :]
    wa = w_a[0].astype(BF16)
    wb = w_b[0].astype(BF16)
    wout = w_out[0].astype(BF16)
    gmlp = g_mlp[0][None, :]
    wup = w_up[0].astype(BF16)
    wdown = w_down[0].astype(BF16)
    gfin = g_final[None, :]

    x1_p, pool_p, hgrn_p = _prompt_mixer(
        x_prompt, meta_tokens, gmix, win, wpool, pscale, hgrn_lb_logits, gon, wa, wb, wout, tile=512)
    y_p = _mlp(x1_p.reshape(B * L, D_MODEL), gmlp, wup, wdown, gfin, tm=512)

    xs = x_sample.reshape(NS, D_MODEL)
    poolT = jnp.swapaxes(state_pool[0], 0, 1)
    x1_s, u_s, hgrn_s = _decode_mixer(
        xs, poolT, state_hgrn[0], gmix, win, wpool, pscale, hgrn_lb_logits, gon, wa, wb, wout, ts=16)
    y_s = _mlp(x1_s, gmlp, wup, wdown, gfin, tm=NS)
    pool_s = jnp.concatenate([state_pool[0][:, 1:, :], u_s[:, None, :]], axis=1)

    return (y_p.reshape(B, L, D_MODEL), y_s.reshape(NS, 1, D_MODEL),
            pool_p[None], hgrn_p[None], pool_s[None], hgrn_s[None])
```

```python
import functools

import jax
import jax.numpy as jnp
from jax import lax
from jax.experimental import pallas as pl
from jax.experimental.pallas import tpu as pltpu

D_MODEL = 1024
N_META = 16
POOL_WIDTH = 512
POOL_WINDOWS = (2, 4, 8, 16)
POOL_GROUP = 128
POOL_MAXW = 16
POOL_BUF = 15
HEADS = 4
DK = 128
DV = 128
KTOT = 512
HWIDTH = 512
D_FF = 4096
EPS = 1e-6
N_IN = 4608
O_U, O_Q, O_F, O_V, O_OG, O_GA, O_GB = 0, 512, 1024, 1536, 2048, 2560, 3584

SUB = 16
CHUNK = 256
GATE_PIECE = 256
GATE_ROWS = 256
LEVEL_HALVES = tuple(1 << i for i in range(CHUNK.bit_length() - 1))
VMEM_LIMIT = 56 * 1024 * 1024

F32 = jnp.float32
BF16 = jnp.bfloat16


def _rms(x, g):
    return x * lax.rsqrt(jnp.mean(x * x, axis=-1, keepdims=True) + EPS) * g


def _sigmoid(x):
    return 0.5 * jnp.tanh(0.5 * x) + 0.5


def _silu(x):
    hx = 0.5 * x
    return hx * jnp.tanh(hx) + hx


def _dot(a, b):
    return jnp.dot(a, b, preferred_element_type=F32)


def _dot_nt(a, b):
    return lax.dot_general(a, b, (((1,), (1,)), ((), ())), preferred_element_type=F32)


def _dot_tn(a, b):
    return lax.dot_general(a, b, (((0,), (0,)), ((), ())), preferred_element_type=F32)


def _lower_bound(lb_logits):
    m = jnp.max(lb_logits, axis=0, keepdims=True)
    e = jnp.exp(lb_logits - m)
    return e[0:1, :] / jnp.sum(e, axis=0, keepdims=True)


def _gates(zq, zf, lb):
    q = _silu(zq)
    fg = lb + (1.0 - lb) * _sigmoid(zf)
    return q, fg


def _cumsum_rows(g):
    n = g.shape[0]
    r = lax.broadcasted_iota(jnp.int32, (n, n), 0)
    c = lax.broadcasted_iota(jnp.int32, (n, n), 1)
    tril = (r >= c).astype(BF16)
    hi = g.astype(BF16)
    r1 = g - hi.astype(F32)
    mid = r1.astype(BF16)
    lo = (r1 - mid.astype(F32)).astype(BF16)
    return _dot(tril, hi) + _dot(tril, mid) + _dot(tril, lo)


def _level_map(n):
    t = lax.broadcasted_iota(jnp.int32, (n, n), 0)
    s = lax.broadcasted_iota(jnp.int32, (n, n), 1)
    x = t ^ s
    lvl = jnp.full((n, n), -1, jnp.int32)
    for li in range(n.bit_length() - 1):
        lvl = jnp.where((x >> li) == 1, li, lvl)
    return jnp.where(t > s, lvl, -1)


def _level_weights(m, r0, Gg, qg, kg, fgg, g_sc, row8, row16):
    if m == 1:
        return jnp.where((row16 & 1) != 0, qg * fgg, kg)
    if m >= SUB:
        blk = (r0 // (2 * m)) * (2 * m)
        ref = g_sc[blk + m - 1:blk + m, :]
        sel = qg if (r0 & m) else kg
    elif m == SUB // 2:
        ref = g_sc[r0 + m - 1:r0 + m, :]
        sel = jnp.where(row16 >= m, qg, kg)
    else:
        halves = []
        for rb in (r0, r0 + 8):
            if m == 4:
                halves.append(jnp.broadcast_to(g_sc[rb + 3:rb + 4, :], (8, KTOT)))
            else:
                halves.append(jnp.where(row8 >= 4, g_sc[rb + 5:rb + 6, :], g_sc[rb + 1:rb + 2, :]))
        ref = jnp.concatenate(halves, axis=0)
        sel = jnp.where((row16 & m) != 0, qg, kg)
    return jnp.exp(-jnp.abs(Gg - ref)) * sel


def _gates_stage(base, par, z_sc, lb, q_sc, k_sc, fg_sc, g_sc, od_sc, dec_sc):
    rs = slice(base, base + CHUNK)
    q, fg = _gates(z_sc[rs, O_Q:O_Q + KTOT], z_sc[rs, O_F:O_F + KTOT], lb)
    k = 1.0 - fg
    G = _cumsum_rows(jnp.log(fg))
    q_sc[...] = q
    k_sc[...] = k
    fg_sc[...] = fg
    g_sc[...] = G
    v = z_sc[rs, O_V:O_V + HWIDTH]
    qk = q * k
    for hd in range(HEADS):
        sl = slice(hd * DK, (hd + 1) * DK)
        od_sc[par, :, sl] = jnp.sum(qk[:, sl], axis=-1, keepdims=True) * v[:, sl]
    dec_sc[par] = jnp.broadcast_to(jnp.exp(G[CHUNK - 1:CHUNK, :]), (8, KTOT))


def _group_stage(j, par, q_sc, k_sc, fg_sc, g_sc, w_sc, qb_sc, kb_sc):
    row8 = lax.broadcasted_iota(jnp.int32, (8, 1), 0)
    row16 = lax.broadcasted_iota(jnp.int32, (SUB, 1), 0)
    r0 = j * SUB
    gs = slice(r0, r0 + SUB)
    Gg, qg, kg, fgg = g_sc[gs, :], q_sc[gs, :], k_sc[gs, :], fg_sc[gs, :]
    g_last = g_sc[CHUNK - 1:CHUNK, :]
    for li, m in enumerate(LEVEL_HALVES):
        w_sc[par, li, gs, :] = _level_weights(m, r0, Gg, qg, kg, fgg, g_sc, row8, row16).astype(BF16)
    qb_sc[par, gs, :] = (qg * jnp.exp(Gg)).astype(BF16)
    kb_sc[par, gs, :] = (kg * jnp.exp(g_last - Gg)).astype(BF16)


def _head_stage(base, hd, par, z_sc, lvl, w_sc, qb_sc, kb_sc, od_sc, dec_sc, o_sc, st_sc):
    rs = slice(base, base + CHUNK)
    sl = slice(hd * DK, (hd + 1) * DK)
    half = CHUNK // 2
    a_lo = jnp.zeros((half, half), F32)
    a_hi = jnp.zeros((half, half), F32)
    zero = jnp.zeros((half, DK), BF16)
    for li in range(len(LEVEL_HALVES) - 1):
        w_lo = w_sc[par, li, :half, sl]
        w_hi = w_sc[par, li, half:, sl]
        x = jnp.concatenate([w_lo, w_hi], axis=1)
        y = jnp.concatenate([jnp.concatenate([w_lo, zero], axis=1),
                             jnp.concatenate([zero, w_hi], axis=1)], axis=0)
        p = _dot_nt(x, y)
        a_lo = jnp.where(lvl == li, p[:, :half], a_lo)
        a_hi = jnp.where(lvl == li, p[:, half:], a_hi)
    top = len(LEVEL_HALVES) - 1
    a_x = _dot_nt(w_sc[par, top, half:, sl], w_sc[par, top, :half, sl])
    A = jnp.concatenate([jnp.concatenate([a_lo, jnp.zeros((half, half), F32)], axis=1),
                         jnp.concatenate([a_x, a_hi], axis=1)], axis=0)
    v_bf = z_sc[rs, O_V + hd * DV:O_V + (hd + 1) * DV].astype(BF16)
    st = st_sc[hd]
    o_sc[rs, sl] = (_dot(A.astype(BF16), v_bf)
                    + _dot_nt(qb_sc[par, :, sl], st.astype(BF16))
                    + od_sc[par, :, sl])
    st_sc[hd] = st * dec_sc[par, 0:1, sl] + _dot_tn(v_bf, kb_sc[par, :, sl])


def _prompt_mixer_kernel(x_ref, meta_ref, gmix_ref, win_ref, wpool_ref, pscale_ref, lbl_ref,
                         gon_ref, wa_ref, wb_ref, wout_ref,
                         x1_ref, pool_ref, hgrn_ref,
                         z_sc, h_sc, ya_sc, ubuf, q_sc, k_sc, fg_sc, g_sc, od_sc, dec_sc,
                         w_sc, qb_sc, kb_sc, o_sc, st_sc, st_meta,
                         u_meta, *, tile):
    b = pl.program_id(0)
    t = pl.program_id(1)
    nt = pl.num_programs(1)
    lb = _lower_bound(lbl_ref[...])
    gmix = gmix_ref[...]

    @pl.when((b == 0) & (t == 0))
    def _meta():
        hm = _rms(meta_ref[...], gmix).astype(BF16)
        zm = _dot(hm, win_ref[:, 0:O_OG])
        u_meta[...] = zm[:, O_U:O_U + POOL_WIDTH]
        _, fg = _gates(zm[:, O_Q:O_Q + KTOT], zm[:, O_F:O_F + KTOT], lb)
        G = _cumsum_rows(jnp.log(fg))
        kt = (1.0 - fg) * jnp.exp(G[N_META - 1:N_META, :] - G)
        v = zm[:, O_V:O_V + HWIDTH]
        for h in range(HEADS):
            sl = slice(h * DK, (h + 1) * DK)
            st_meta[h] = _dot_tn(v[:, sl].astype(BF16), kt[:, sl].astype(BF16))

    @pl.when(t == 0)
    def _init():
        st_sc[...] = st_meta[...]
        ubuf[0:POOL_MAXW, :] = u_meta[...]

    n_chunks = tile // CHUNK
    h_sc[...] = _rms(x_ref[0], gmix).astype(BF16)
    z_sc[:, 0:O_OG] = _dot(h_sc[...], win_ref[:, 0:O_OG])

    u = z_sc[:, O_U:O_U + POOL_WIDTH]
    ubuf[POOL_MAXW:POOL_MAXW + tile, :] = u
    ya_parts = []
    for gi, w in enumerate(POOL_WINDOWS):
        cs = slice(gi * POOL_GROUP, (gi + 1) * POOL_GROUP)
        wsum = ubuf[:, cs]
        span = 1
        while span < w:
            wsum = wsum + pltpu.roll(wsum, span, 0)
            span *= 2
        pooled = wsum[POOL_MAXW:, :] / float(w) - u[:, cs]
        ya_parts.append(_dot(pooled.astype(BF16), wpool_ref[gi]))
    ya_pre = jnp.concatenate(ya_parts, axis=-1) * pscale_ref[...]
    ya_sc[...] = _dot(ya_pre.astype(BF16), wa_ref[...])
    ubuf[0:POOL_MAXW, :] = ubuf[tile:tile + POOL_MAXW, :]

    def gate_piece(r0, c0):
        z_sc[r0:r0 + GATE_ROWS, c0:c0 + GATE_PIECE] = _dot(
            h_sc[r0:r0 + GATE_ROWS, :], win_ref[:, c0:c0 + GATE_PIECE])
    pending = [functools.partial(gate_piece, r0, c0)
               for c0 in range(O_OG, N_IN, GATE_PIECE) for r0 in range(0, tile, GATE_ROWS)]
    n_pieces = len(pending)
    n_groups = CHUNK // SUB
    n_slots = n_chunks * (n_groups + HEADS)
    slot = [0]

    def interleave():
        slot[0] += 1
        while len(pending) > n_pieces - (-(-slot[0] * n_pieces // n_slots)):
            pending.pop(0)()

    def gates(c):
        _gates_stage(c * CHUNK, c % 2, z_sc, lb, q_sc, k_sc, fg_sc, g_sc, od_sc, dec_sc)

    def group(c, j):
        _group_stage(j, c % 2, q_sc, k_sc, fg_sc, g_sc, w_sc, qb_sc, kb_sc)
        interleave()

    def head(c, hd):
        _head_stage(c * CHUNK, hd, c % 2, z_sc, lvl, w_sc, qb_sc, kb_sc, od_sc, dec_sc, o_sc, st_sc)
        interleave()

    lvl = _level_map(CHUNK // 2)
    gates(0)
    for j in range(n_groups):
        group(0, j)
    for c in range(n_chunks):
        if c + 1 < n_chunks:
            gates(c + 1)
        for hd in range(HEADS):
            head(c, hd)
            if c + 1 < n_chunks:
                for j in range(hd * n_groups // HEADS, (hd + 1) * n_groups // HEADS):
                    group(c + 1, j)
    assert not pending and slot[0] == n_slots

    gon = gon_ref[...]
    o_parts = []
    for hd in range(HEADS):
        sl = slice(hd * DV, (hd + 1) * DV)
        og = z_sc[:, O_OG + hd * DV:O_OG + (hd + 1) * DV]
        o_parts.append(_rms(o_sc[:, sl], gon) * _silu(og))
    yb = _dot(jnp.concatenate(o_parts, axis=-1).astype(BF16), wb_ref[...])
    m = (_sigmoid(z_sc[:, O_GA:O_GA + D_MODEL]) * ya_sc[...]
         + _sigmoid(z_sc[:, O_GB:O_GB + D_MODEL]) * yb)
    x1_ref[0] = x_ref[0] + _dot(m.astype(BF16), wout_ref[...])

    @pl.when(t == nt - 1)
    def _state_out():
        pool_ref[0] = ubuf[1:POOL_MAXW, :]
        for hd in range(HEADS):
            hgrn_ref[0, hd] = st_sc[hd].T


def _const_spec(shape):
    nd = len(shape)
    return pl.BlockSpec(shape, lambda *_: (0,) * nd, pipeline_mode=pl.Buffered(1))


def _prompt_mixer(x, meta, gmix, win, wpool, pscale, lbl, gon, wa, wb, wout, *, tile):
    B, L, _ = x.shape
    nt = L // tile
    kern = functools.partial(_prompt_mixer_kernel, tile=tile)
    return pl.pallas_call(
        kern,
        grid=(B, nt),
        in_specs=[
            pl.BlockSpec((1, tile, D_MODEL), lambda b, t: (b, t, 0)),
            _const_spec((N_META, D_MODEL)),
            _const_spec((1, D_MODEL)),
            _const_spec((D_MODEL, N_IN)),
            _const_spec((len(POOL_WINDOWS), POOL_GROUP, POOL_GROUP)),
            _const_spec((1, POOL_WIDTH)),
            _const_spec((2, KTOT)),
            _const_spec((1, DV)),
            _const_spec((POOL_WIDTH, D_MODEL)),
            _const_spec((HWIDTH, D_MODEL)),
            _const_spec((D_MODEL, D_MODEL)),
        ],
        out_specs=[
            pl.BlockSpec((1, tile, D_MODEL), lambda b, t: (b, t, 0)),
            pl.BlockSpec((1, POOL_BUF, POOL_WIDTH), lambda b, t: (b, 0, 0)),
            pl.BlockSpec((1, HEADS, DK, DV), lambda b, t: (b, 0, 0, 0)),
        ],
        out_shape=[
            jax.ShapeDtypeStruct((B, L, D_MODEL), F32),
            jax.ShapeDtypeStruct((B, POOL_BUF, POOL_WIDTH), F32),
            jax.ShapeDtypeStruct((B, HEADS, DK, DV), F32),
        ],
        scratch_shapes=[
            pltpu.VMEM((tile, N_IN), F32),
            pltpu.VMEM((tile, D_MODEL), BF16),
            pltpu.VMEM((tile, D_MODEL), F32),
            pltpu.VMEM((tile + POOL_MAXW, POOL_WIDTH), F32),
            pltpu.VMEM((CHUNK, KTOT), F32),
            pltpu.VMEM((CHUNK, KTOT), F32),
            pltpu.VMEM((CHUNK, KTOT), F32),
            pltpu.VMEM((CHUNK, KTOT), F32),
            pltpu.VMEM((2, CHUNK, HWIDTH), F32),
            pltpu.VMEM((2, 8, KTOT), F32),
            pltpu.VMEM((2, len(LEVEL_HALVES), CHUNK, KTOT), BF16),
            pltpu.VMEM((2, CHUNK, KTOT), BF16),
            pltpu.VMEM((2, CHUNK, KTOT), BF16),
            pltpu.VMEM((tile, HWIDTH), F32),
            pltpu.VMEM((HEADS, DV, DK), F32),
            pltpu.VMEM((HEADS, DV, DK), F32),
            pltpu.VMEM((N_META, POOL_WIDTH), F32),
        ],
        compiler_params=pltpu.CompilerParams(
            dimension_semantics=("arbitrary", "arbitrary"),
            vmem_limit_bytes=VMEM_LIMIT),
        name="prompt_mixer",
    )(x, meta, gmix, win, wpool, pscale, lbl, gon, wa, wb, wout)


def _decode_mixer_kernel(x_ref, poolT_ref, s_ref, gmix_ref, win_ref, wpool_ref, pscale_ref,
                         lbl_ref, gon_ref, wa_ref, wb_ref, wout_ref,
                         x1_ref, unew_ref, snew_ref,
                         z_sc, o_sc, *, ts):
    i = pl.program_id(0)
    n = pl.num_programs(0)
    lb = _lower_bound(lbl_ref[...])

    @pl.when(i == 0)
    def _in_proj():
        h = _rms(x_ref[...], gmix_ref[...]).astype(BF16)
        z_sc[...] = _dot(h, win_ref[...])

    r = pl.ds(pl.multiple_of(i * ts, ts), ts)
    q, fg = _gates(z_sc[r, O_Q:O_Q + KTOT], z_sc[r, O_F:O_F + KTOT], lb)
    k = 1.0 - fg
    v = z_sc[r, O_V:O_V + HWIDTH]
    row = lax.broadcasted_iota(jnp.int32, (ts, 1), 0)
    for hd in range(HEADS):
        sl = slice(hd * DK, (hd + 1) * DK)
        fT = fg[:, sl].T
        kT = k[:, sl].T
        qT = q[:, sl].T
        o = jnp.zeros((ts, DV), F32)
        for s in range(ts):
            s_new = fT[:, s:s + 1] * s_ref[s, hd] + kT[:, s:s + 1] * v[s:s + 1, sl]
            snew_ref[s, hd] = s_new
            o_row = jnp.sum(qT[:, s:s + 1] * s_new, axis=0, keepdims=True)
            o = jnp.where(row == s, o_row, o)
        o_sc[r, sl] = o

    @pl.when(i == n - 1)
    def _out_proj():
        x = x_ref[...]
        u = z_sc[:, O_U:O_U + POOL_WIDTH]
        unew_ref[...] = u
        ya_parts = []
        for gi, w in enumerate(POOL_WINDOWS):
            cs = slice(gi * POOL_GROUP, (gi + 1) * POOL_GROUP)
            wsum = u[:, cs]
            for j in range(1, w):
                wsum = wsum + poolT_ref[POOL_BUF - j, :, cs]
            pooled = wsum / float(w) - u[:, cs]
            ya_parts.append(_dot(pooled.astype(BF16), wpool_ref[gi]))
        ya_pre = jnp.concatenate(ya_parts, axis=-1) * pscale_ref[...]
        ya = _dot(ya_pre.astype(BF16), wa_ref[...])
        gon = gon_ref[...]
        o_parts = []
        for hd in range(HEADS):
            sl = slice(hd * DV, (hd + 1) * DV)
            og = z_sc[:, O_OG + hd * DV:O_OG + (hd + 1) * DV]
            o_parts.append(_rms(o_sc[:, sl], gon) * _silu(og))
        yb = _dot(jnp.concatenate(o_parts, axis=-1).astype(BF16), wb_ref[...])
        m = (_sigmoid(z_sc[:, O_GA:O_GA + D_MODEL]) * ya
             + _sigmoid(z_sc[:, O_GB:O_GB + D_MODEL]) * yb)
        x1_ref[...] = x + _dot(m.astype(BF16), wout_ref[...])


def _decode_mixer(x, poolT, state, gmix, win, wpool, pscale, lbl, gon, wa, wb, wout, *, ts):
    n = x.shape[0]
    kern = functools.partial(_decode_mixer_kernel, ts=ts)
    return pl.pallas_call(
        kern,
        grid=(n // ts,),
        in_specs=[
            _const_spec((n, D_MODEL)),
            _const_spec((POOL_BUF, n, POOL_WIDTH)),
            pl.BlockSpec((ts, HEADS, DK, DV), lambda i: (i, 0, 0, 0)),
            _const_spec((1, D_MODEL)),
            _const_spec((D_MODEL, N_IN)),
            _const_spec((len(POOL_WINDOWS), POOL_GROUP, POOL_GROUP)),
            _const_spec((1, POOL_WIDTH)),
            _const_spec((2, KTOT)),
            _const_spec((1, DV)),
            _const_spec((POOL_WIDTH, D_MODEL)),
            _const_spec((HWIDTH, D_MODEL)),
            _const_spec((D_MODEL, D_MODEL)),
        ],
        out_specs=[
            pl.BlockSpec((n, D_MODEL), lambda i: (0, 0)),
            pl.BlockSpec((n, POOL_WIDTH), lambda i: (0, 0)),
            pl.BlockSpec((ts, HEADS, DK, DV), lambda i: (i, 0, 0, 0)),
        ],
        out_shape=[
            jax.ShapeDtypeStruct((n, D_MODEL), F32),
            jax.ShapeDtypeStruct((n, POOL_WIDTH), F32),
            jax.ShapeDtypeStruct((n, HEADS, DK, DV), F32),
        ],
        scratch_shapes=[
            pltpu.VMEM((n, N_IN), F32),
            pltpu.VMEM((n, HWIDTH), F32),
        ],
        compiler_params=pltpu.CompilerParams(
            dimension_semantics=("arbitrary",),
            vmem_limit_bytes=VMEM_LIMIT),
        name="decode_mixer",
    )(x, poolT, state, gmix, win, wpool, pscale, lbl, gon, wa, wb, wout)


def _mlp_kernel(x_ref, gmlp_ref, wup_ref, wdown_ref, gfin_ref, y_ref, *, ff_chunk):
    x = x_ref[...]
    h = _rms(x, gmlp_ref[...]).astype(BF16)
    acc = x
    for c in range(D_FF // ff_chunk):
        cs = slice(c * ff_chunk, (c + 1) * ff_chunk)
        a = jnp.maximum(_dot(h, wup_ref[:, cs]), 0.0)
        acc = acc + _dot((a * a).astype(BF16), wdown_ref[cs, :])
    y_ref[...] = _rms(acc, gfin_ref[...])


def _mlp(x, gmlp, wup, wdown, gfin, *, tm, ff_chunk=1024):
    n = x.shape[0]
    kern = functools.partial(_mlp_kernel, ff_chunk=ff_chunk)
    return pl.pallas_call(
        kern,
        grid=(n // tm,),
        in_specs=[
            pl.BlockSpec((tm, D_MODEL), lambda i: (i, 0)),
            _const_spec((1, D_MODEL)),
            _const_spec((D_MODEL, D_FF)),
            _const_spec((D_FF, D_MODEL)),
            _const_spec((1, D_MODEL)),
        ],
        out_specs=pl.BlockSpec((tm, D_MODEL), lambda i: (i, 0)),
        out_shape=jax.ShapeDtypeStruct((n, D_MODEL), F32),
        compiler_params=pltpu.CompilerParams(
            dimension_semantics=("arbitrary",),
            vmem_limit_bytes=VMEM_LIMIT),
        name="channel_mlp",
    )(x, gmlp, wup, wdown, gfin)


def kernel(x_prompt, x_sample, state_pool, state_hgrn, meta_tokens, g_mix, w_in, w_pool, pool_scale,
           hgrn_lb_logits, g_onorm, w_a, w_b, w_out, g_mlp, w_up, w_down, g_final):
    B, L, _ = x_prompt.shape
    NS = x_sample.shape[0]
    assert g_mix.shape[0] == 1, "single-layer trunk"
    gmix = g_mix[0][None, :]
    win = w_in[0].astype(BF16)
    wpool = w_pool[0].astype(BF16)
    pscale = pool_scale[0][None, :]
    gon = g_onorm[0][None, :]
    wa = w_a[0].astype(BF16)
    wb = w_b[0].astype(BF16)
    wout = w_out[0].astype(BF16)
    gmlp = g_mlp[0][None, :]
    wup = w_up[0].astype(BF16)
    wdown = w_down[0].astype(BF16)
    gfin = g_final[None, :]

    x1_p, pool_p, hgrn_p = _prompt_mixer(
        x_prompt, meta_tokens, gmix, win, wpool, pscale, hgrn_lb_logits, gon, wa, wb, wout, tile=512)
    y_p = _mlp(x1_p.reshape(B * L, D_MODEL), gmlp, wup, wdown, gfin, tm=1024)

    xs = x_sample.reshape(NS, D_MODEL)
    poolT = jnp.swapaxes(state_pool[0], 0, 1)
    x1_s, u_s, hgrn_s = _decode_mixer(
        xs, poolT, state_hgrn[0], gmix, win, wpool, pscale, hgrn_lb_logits, gon, wa, wb, wout, ts=16)
    y_s = _mlp(x1_s, gmlp, wup, wdown, gfin, tm=NS)
    pool_s = jnp.concatenate([state_pool[0][:, 1:, :], u_s[:, None, :]], axis=1)

    return (y_p.reshape(B, L, D_MODEL), y_s.reshape(NS, 1, D_MODEL),
            pool_p[None], hgrn_p[None], pool_s[None], hgrn_s[None])
```

```python
import functools

import jax
import jax.numpy as jnp
from jax import lax
from jax.experimental import pallas as pl
from jax.experimental.pallas import tpu as pltpu

D_MODEL = 1024
N_META = 16
POOL_WIDTH = 512
POOL_WINDOWS = (2, 4, 8, 16)
POOL_GROUP = 128
POOL_MAXW = 16
POOL_BUF = 15
HEADS = 4
DK = 128
DV = 128
KTOT = 512
HWIDTH = 512
D_FF = 4096
EPS = 1e-6
N_IN = 4608
O_U, O_Q, O_F, O_V, O_OG, O_GA, O_GB = 0, 512, 1024, 1536, 2048, 2560, 3584

SUB = 16
CHUNK = 256
GATE_PIECE = 256
GATE_ROWS = 256
LEVEL_HALVES = tuple(1 << i for i in range(CHUNK.bit_length() - 1))
VMEM_LIMIT = 56 * 1024 * 1024

F32 = jnp.float32
BF16 = jnp.bfloat16


def _rms(x, g):
    return x * lax.rsqrt(jnp.mean(x * x, axis=-1, keepdims=True) + EPS) * g


def _twice_sigmoid(hx):
    return jnp.tanh(hx) + 1.0


def _silu(hx):
    return hx * jnp.tanh(hx) + hx


def _dot(a, b):
    return jnp.dot(a, b, preferred_element_type=F32)


def _dot_nt(a, b):
    return lax.dot_general(a, b, (((1,), (1,)), ((), ())), preferred_element_type=F32)


def _dot_tn(a, b):
    return lax.dot_general(a, b, (((0,), (0,)), ((), ())), preferred_element_type=F32)


def _lower_bound(lb_logits):
    m = jnp.max(lb_logits, axis=0, keepdims=True)
    e = jnp.exp(lb_logits - m)
    return e[0:1, :] / jnp.sum(e, axis=0, keepdims=True)


def _gates(hq, hf, lb):
    q = _silu(hq)
    fg = 0.5 * (1.0 + lb) + (0.5 * (1.0 - lb)) * jnp.tanh(hf)
    return q, fg


def _cumsum_rows(g):
    n = g.shape[0]
    r = lax.broadcasted_iota(jnp.int32, (n, n), 0)
    c = lax.broadcasted_iota(jnp.int32, (n, n), 1)
    tril = (r >= c).astype(BF16)
    hi = g.astype(BF16)
    r1 = g - hi.astype(F32)
    mid = r1.astype(BF16)
    lo = (r1 - mid.astype(F32)).astype(BF16)
    return _dot(tril, hi) + _dot(tril, mid) + _dot(tril, lo)


def _level_map(n):
    t = lax.broadcasted_iota(jnp.int32, (n, n), 0)
    s = lax.broadcasted_iota(jnp.int32, (n, n), 1)
    x = t ^ s
    lvl = jnp.full((n, n), -1, jnp.int32)
    for li in range(n.bit_length() - 1):
        lvl = jnp.where((x >> li) == 1, li, lvl)
    return jnp.where(t > s, lvl, -1)


def _level_weights(m, r0, Gg, qg, kg, fgg, g_sc, row8, row16):
    if m == 1:
        return jnp.where((row16 & 1) != 0, qg * fgg, kg)
    if m >= SUB:
        blk = (r0 // (2 * m)) * (2 * m)
        ref = g_sc[blk + m - 1:blk + m, :]
        return (jnp.exp2(Gg - ref) * qg) if (r0 & m) else (jnp.exp2(ref - Gg) * kg)
    if m == SUB // 2:
        ref = g_sc[r0 + m - 1:r0 + m, :]
        return jnp.concatenate([jnp.exp2(ref - Gg[:m]) * kg[:m], jnp.exp2(Gg[m:] - ref) * qg[m:]], axis=0)
    halves = []
    for rb in (r0, r0 + 8):
        if m == 4:
            halves.append(jnp.broadcast_to(g_sc[rb + 3:rb + 4, :], (8, KTOT)))
        else:
            halves.append(jnp.where(row8 >= 4, g_sc[rb + 5:rb + 6, :], g_sc[rb + 1:rb + 2, :]))
    ref = jnp.concatenate(halves, axis=0)
    sel = jnp.where((row16 & m) != 0, qg, kg)
    return jnp.exp2(-jnp.abs(Gg - ref)) * sel


def _gates_stage(base, par, z_sc, lb, q_sc, k_sc, fg_sc, g_sc, od_sc, dec_sc):
    rs = slice(base, base + CHUNK)
    q, fg = _gates(z_sc[rs, O_Q:O_Q + KTOT], z_sc[rs, O_F:O_F + KTOT], lb)
    k = 1.0 - fg
    G = _cumsum_rows(jnp.log2(fg))
    q_sc[...] = q
    k_sc[...] = k
    fg_sc[...] = fg
    g_sc[...] = G
    v = z_sc[rs, O_V:O_V + HWIDTH]
    qk = q * k
    for hd in range(HEADS):
        sl = slice(hd * DK, (hd + 1) * DK)
        od_sc[par, :, sl] = jnp.sum(qk[:, sl], axis=-1, keepdims=True) * v[:, sl]
    dec_sc[par] = jnp.broadcast_to(jnp.exp2(G[CHUNK - 1:CHUNK, :]), (8, KTOT))


def _group_stage(j, par, q_sc, k_sc, fg_sc, g_sc, w_sc, qb_sc, kb_sc):
    row8 = lax.broadcasted_iota(jnp.int32, (8, 1), 0)
    row16 = lax.broadcasted_iota(jnp.int32, (SUB, 1), 0)
    r0 = j * SUB
    gs = slice(r0, r0 + SUB)
    Gg, qg, kg, fgg = g_sc[gs, :], q_sc[gs, :], k_sc[gs, :], fg_sc[gs, :]
    g_last = g_sc[CHUNK - 1:CHUNK, :]
    for li, m in enumerate(LEVEL_HALVES):
        w_sc[par, li, gs, :] = _level_weights(m, r0, Gg, qg, kg, fgg, g_sc, row8, row16).astype(BF16)
    qb_sc[par, gs, :] = (qg * jnp.exp2(Gg)).astype(BF16)
    kb_sc[par, gs, :] = (kg * jnp.exp2(g_last - Gg)).astype(BF16)


def _head_stage(base, hd, par, z_sc, lvl, w_sc, qb_sc, kb_sc, od_sc, dec_sc, o_sc, st_sc):
    rs = slice(base, base + CHUNK)
    sl = slice(hd * DK, (hd + 1) * DK)
    half = CHUNK // 2
    a_lo = jnp.zeros((half, half), F32)
    a_hi = jnp.zeros((half, half), F32)
    zero = jnp.zeros((half, DK), BF16)
    for li in range(len(LEVEL_HALVES) - 1):
        w_lo = w_sc[par, li, :half, sl]
        w_hi = w_sc[par, li, half:, sl]
        x = jnp.concatenate([w_lo, w_hi], axis=1)
        y = jnp.concatenate([jnp.concatenate([w_lo.T, zero], axis=1),
                             jnp.concatenate([zero, w_hi.T], axis=1)], axis=0)
        p = _dot(x, y)
        a_lo = jnp.where(lvl == li, p[:, :half], a_lo)
        a_hi = jnp.where(lvl == li, p[:, half:], a_hi)
    top = len(LEVEL_HALVES) - 1
    a_x = _dot_nt(w_sc[par, top, half:, sl], w_sc[par, top, :half, sl])
    A = jnp.concatenate([jnp.concatenate([a_lo, jnp.zeros((half, half), F32)], axis=1),
                         jnp.concatenate([a_x, a_hi], axis=1)], axis=0)
    v_bf = z_sc[rs, O_V + hd * DV:O_V + (hd + 1) * DV].astype(BF16)
    st = st_sc[hd]
    o_sc[rs, sl] = (_dot(A.astype(BF16), v_bf)
                    + _dot_nt(qb_sc[par, :, sl], st.astype(BF16))
                    + od_sc[par, :, sl])
    st_sc[hd] = st * dec_sc[par, 0:1, sl] + _dot_tn(v_bf, kb_sc[par, :, sl])


def _prompt_mixer_kernel(x_ref, meta_ref, gmix_ref, win_ref, wpool_ref, pscale_ref, lbl_ref,
                         gon_ref, wa_ref, wb_ref, wout_ref,
                         x1_ref, pool_ref, hgrn_ref,
                         z_sc, h_sc, ya_sc, ubuf, q_sc, k_sc, fg_sc, g_sc, od_sc, dec_sc,
                         w_sc, qb_sc, kb_sc, o_sc, st_sc, st_meta,
                         u_meta, *, tile):
    b = pl.program_id(0)
    t = pl.program_id(1)
    nt = pl.num_programs(1)
    lb = _lower_bound(lbl_ref[...])
    gmix = gmix_ref[...]

    @pl.when((b == 0) & (t == 0))
    def _meta():
        hm = _rms(meta_ref[...], gmix).astype(BF16)
        zm = _dot(hm, win_ref[:, 0:O_OG])
        u_meta[...] = zm[:, O_U:O_U + POOL_WIDTH]
        _, fg = _gates(zm[:, O_Q:O_Q + KTOT], zm[:, O_F:O_F + KTOT], lb)
        G = _cumsum_rows(jnp.log2(fg))
        kt = (1.0 - fg) * jnp.exp2(G[N_META - 1:N_META, :] - G)
        v = zm[:, O_V:O_V + HWIDTH]
        for h in range(HEADS):
            sl = slice(h * DK, (h + 1) * DK)
            st_meta[h] = _dot_tn(v[:, sl].astype(BF16), kt[:, sl].astype(BF16))

    @pl.when(t == 0)
    def _init():
        st_sc[...] = st_meta[...]
        ubuf[0:POOL_MAXW, :] = u_meta[...]

    n_chunks = tile // CHUNK
    h_sc[...] = _rms(x_ref[0], gmix).astype(BF16)
    z_sc[:, 0:O_OG] = _dot(h_sc[...], win_ref[:, 0:O_OG])

    u = z_sc[:, O_U:O_U + POOL_WIDTH]
    ubuf[POOL_MAXW:POOL_MAXW + tile, :] = u
    ya_parts = []
    for gi, w in enumerate(POOL_WINDOWS):
        cs = slice(gi * POOL_GROUP, (gi + 1) * POOL_GROUP)
        wsum = ubuf[:, cs]
        span = 1
        while span < w:
            wsum = wsum + pltpu.roll(wsum, span, 0)
            span *= 2
        pooled = wsum[POOL_MAXW:, :] / float(w) - u[:, cs]
        ya_parts.append(_dot(pooled.astype(BF16), wpool_ref[gi]))
    ya_pre = jnp.concatenate(ya_parts, axis=-1) * pscale_ref[...]
    ya_sc[...] = _dot(ya_pre.astype(BF16), wa_ref[...])
    ubuf[0:POOL_MAXW, :] = ubuf[tile:tile + POOL_MAXW, :]

    def gate_piece(r0, c0):
        z_sc[r0:r0 + GATE_ROWS, c0:c0 + GATE_PIECE] = _dot(
            h_sc[r0:r0 + GATE_ROWS, :], win_ref[:, c0:c0 + GATE_PIECE])
    pending = [functools.partial(gate_piece, r0, c0)
               for c0 in range(O_OG, N_IN, GATE_PIECE) for r0 in range(0, tile, GATE_ROWS)]
    n_pieces = len(pending)
    n_groups = CHUNK // SUB
    n_slots = n_chunks * (n_groups + HEADS)
    slot = [0]

    def interleave():
        slot[0] += 1
        while len(pending) > n_pieces - (-(-slot[0] * n_pieces // n_slots)):
            pending.pop(0)()

    def gates(c):
        _gates_stage(c * CHUNK, c % 2, z_sc, lb, q_sc, k_sc, fg_sc, g_sc, od_sc, dec_sc)

    def group(c, j):
        _group_stage(j, c % 2, q_sc, k_sc, fg_sc, g_sc, w_sc, qb_sc, kb_sc)
        interleave()

    def head(c, hd):
        _head_stage(c * CHUNK, hd, c % 2, z_sc, lvl, w_sc, qb_sc, kb_sc, od_sc, dec_sc, o_sc, st_sc)
        interleave()

    lvl = _level_map(CHUNK // 2)
    gates(0)
    for j in range(n_groups):
        group(0, j)
    for c in range(n_chunks):
        if c + 1 < n_chunks:
            gates(c + 1)
        for hd in range(HEADS):
            head(c, hd)
            if c + 1 < n_chunks:
                for j in range(hd * n_groups // HEADS, (hd + 1) * n_groups // HEADS):
                    group(c + 1, j)
    assert not pending and slot[0] == n_slots

    gon = gon_ref[...]
    o_parts = []
    for hd in range(HEADS):
        sl = slice(hd * DV, (hd + 1) * DV)
        og = z_sc[:, O_OG + hd * DV:O_OG + (hd + 1) * DV]
        o_parts.append(_rms(o_sc[:, sl], gon) * _silu(og))
    yb = _dot(jnp.concatenate(o_parts, axis=-1).astype(BF16), wb_ref[...])
    m = (_twice_sigmoid(z_sc[:, O_GA:O_GA + D_MODEL]) * ya_sc[...]
         + _twice_sigmoid(z_sc[:, O_GB:O_GB + D_MODEL]) * yb)
    x1_ref[0] = x_ref[0] + _dot(m.astype(BF16), wout_ref[...])

    @pl.when(t == nt - 1)
    def _state_out():
        pool_ref[0] = ubuf[1:POOL_MAXW, :]
        for hd in range(HEADS):
            hgrn_ref[0, hd] = st_sc[hd].T


def _const_spec(shape):
    nd = len(shape)
    return pl.BlockSpec(shape, lambda *_: (0,) * nd, pipeline_mode=pl.Buffered(1))


def _prompt_mixer(x, meta, gmix, win, wpool, pscale, lbl, gon, wa, wb, wout, *, tile):
    B, L, _ = x.shape
    nt = L // tile
    kern = functools.partial(_prompt_mixer_kernel, tile=tile)
    return pl.pallas_call(
        kern,
        grid=(B, nt),
        in_specs=[
            pl.BlockSpec((1, tile, D_MODEL), lambda b, t: (b, t, 0)),
            _const_spec((N_META, D_MODEL)),
            _const_spec((1, D_MODEL)),
            _const_spec((D_MODEL, N_IN)),
            _const_spec((len(POOL_WINDOWS), POOL_GROUP, POOL_GROUP)),
            _const_spec((1, POOL_WIDTH)),
            _const_spec((2, KTOT)),
            _const_spec((1, DV)),
            _const_spec((POOL_WIDTH, D_MODEL)),
            _const_spec((HWIDTH, D_MODEL)),
            _const_spec((D_MODEL, D_MODEL)),
        ],
        out_specs=[
            pl.BlockSpec((1, tile, D_MODEL), lambda b, t: (b, t, 0)),
            pl.BlockSpec((1, POOL_BUF, POOL_WIDTH), lambda b, t: (b, 0, 0)),
            pl.BlockSpec((1, HEADS, DK, DV), lambda b, t: (b, 0, 0, 0)),
        ],
        out_shape=[
            jax.ShapeDtypeStruct((B, L, D_MODEL), F32),
            jax.ShapeDtypeStruct((B, POOL_BUF, POOL_WIDTH), F32),
            jax.ShapeDtypeStruct((B, HEADS, DK, DV), F32),
        ],
        scratch_shapes=[
            pltpu.VMEM((tile, N_IN), F32),
            pltpu.VMEM((tile, D_MODEL), BF16),
            pltpu.VMEM((tile, D_MODEL), F32),
            pltpu.VMEM((tile + POOL_MAXW, POOL_WIDTH), F32),
            pltpu.VMEM((CHUNK, KTOT), F32),
            pltpu.VMEM((CHUNK, KTOT), F32),
            pltpu.VMEM((CHUNK, KTOT), F32),
            pltpu.VMEM((CHUNK, KTOT), F32),
            pltpu.VMEM((2, CHUNK, HWIDTH), F32),
            pltpu.VMEM((2, 8, KTOT), F32),
            pltpu.VMEM((2, len(LEVEL_HALVES), CHUNK, KTOT), BF16),
            pltpu.VMEM((2, CHUNK, KTOT), BF16),
            pltpu.VMEM((2, CHUNK, KTOT), BF16),
            pltpu.VMEM((tile, HWIDTH), F32),
            pltpu.VMEM((HEADS, DV, DK), F32),
            pltpu.VMEM((HEADS, DV, DK), F32),
            pltpu.VMEM((N_META, POOL_WIDTH), F32),
        ],
        compiler_params=pltpu.CompilerParams(
            dimension_semantics=("arbitrary", "arbitrary"),
            vmem_limit_bytes=VMEM_LIMIT),
        name="prompt_mixer",
    )(x, meta, gmix, win, wpool, pscale, lbl, gon, wa, wb, wout)


def _decode_mixer_kernel(x_ref, poolT_ref, s_ref, gmix_ref, win_ref, wpool_ref, pscale_ref,
                         lbl_ref, gon_ref, wa_ref, wb_ref, wout_ref,
                         x1_ref, unew_ref, snew_ref,
                         z_sc, o_sc, *, ts):
    i = pl.program_id(0)
    n = pl.num_programs(0)
    lb = _lower_bound(lbl_ref[...])

    @pl.when(i == 0)
    def _in_proj():
        h = _rms(x_ref[...], gmix_ref[...]).astype(BF16)
        z_sc[...] = _dot(h, win_ref[...])

    r = pl.ds(pl.multiple_of(i * ts, ts), ts)
    q, fg = _gates(z_sc[r, O_Q:O_Q + KTOT], z_sc[r, O_F:O_F + KTOT], lb)
    k = 1.0 - fg
    v = z_sc[r, O_V:O_V + HWIDTH]
    row = lax.broadcasted_iota(jnp.int32, (ts, 1), 0)
    for hd in range(HEADS):
        sl = slice(hd * DK, (hd + 1) * DK)
        fT = fg[:, sl].T
        kT = k[:, sl].T
        qT = q[:, sl].T
        o = jnp.zeros((ts, DV), F32)
        for s in range(ts):
            s_new = fT[:, s:s + 1] * s_ref[s, hd] + kT[:, s:s + 1] * v[s:s + 1, sl]
            snew_ref[s, hd] = s_new
            o_row = jnp.sum(qT[:, s:s + 1] * s_new, axis=0, keepdims=True)
            o = jnp.where(row == s, o_row, o)
        o_sc[r, sl] = o

    @pl.when(i == n - 1)
    def _out_proj():
        x = x_ref[...]
        u = z_sc[:, O_U:O_U + POOL_WIDTH]
        unew_ref[...] = u
        ya_parts = []
        for gi, w in enumerate(POOL_WINDOWS):
            cs = slice(gi * POOL_GROUP, (gi + 1) * POOL_GROUP)
            wsum = u[:, cs]
            for j in range(1, w):
                wsum = wsum + poolT_ref[POOL_BUF - j, :, cs]
            pooled = wsum / float(w) - u[:, cs]
            ya_parts.append(_dot(pooled.astype(BF16), wpool_ref[gi]))
        ya_pre = jnp.concatenate(ya_parts, axis=-1) * pscale_ref[...]
        ya = _dot(ya_pre.astype(BF16), wa_ref[...])
        gon = gon_ref[...]
        o_parts = []
        for hd in range(HEADS):
            sl = slice(hd * DV, (hd + 1) * DV)
            og = z_sc[:, O_OG + hd * DV:O_OG + (hd + 1) * DV]
            o_parts.append(_rms(o_sc[:, sl], gon) * _silu(og))
        yb = _dot(jnp.concatenate(o_parts, axis=-1).astype(BF16), wb_ref[...])
        m = (_twice_sigmoid(z_sc[:, O_GA:O_GA + D_MODEL]) * ya
             + _twice_sigmoid(z_sc[:, O_GB:O_GB + D_MODEL]) * yb)
        x1_ref[...] = x + _dot(m.astype(BF16), wout_ref[...])


def _decode_mixer(x, poolT, state, gmix, win, wpool, pscale, lbl, gon, wa, wb, wout, *, ts):
    n = x.shape[0]
    kern = functools.partial(_decode_mixer_kernel, ts=ts)
    return pl.pallas_call(
        kern,
        grid=(n // ts,),
        in_specs=[
            _const_spec((n, D_MODEL)),
            _const_spec((POOL_BUF, n, POOL_WIDTH)),
            pl.BlockSpec((ts, HEADS, DK, DV), lambda i: (i, 0, 0, 0)),
            _const_spec((1, D_MODEL)),
            _const_spec((D_MODEL, N_IN)),
            _const_spec((len(POOL_WINDOWS), POOL_GROUP, POOL_GROUP)),
            _const_spec((1, POOL_WIDTH)),
            _const_spec((2, KTOT)),
            _const_spec((1, DV)),
            _const_spec((POOL_WIDTH, D_MODEL)),
            _const_spec((HWIDTH, D_MODEL)),
            _const_spec((D_MODEL, D_MODEL)),
        ],
        out_specs=[
            pl.BlockSpec((n, D_MODEL), lambda i: (0, 0)),
            pl.BlockSpec((n, POOL_WIDTH), lambda i: (0, 0)),
            pl.BlockSpec((ts, HEADS, DK, DV), lambda i: (i, 0, 0, 0)),
        ],
        out_shape=[
            jax.ShapeDtypeStruct((n, D_MODEL), F32),
            jax.ShapeDtypeStruct((n, POOL_WIDTH), F32),
            jax.ShapeDtypeStruct((n, HEADS, DK, DV), F32),
        ],
        scratch_shapes=[
            pltpu.VMEM((n, N_IN), F32),
            pltpu.VMEM((n, HWIDTH), F32),
        ],
        compiler_params=pltpu.CompilerParams(
            dimension_semantics=("arbitrary",),
            vmem_limit_bytes=VMEM_LIMIT),
        name="decode_mixer",
    )(x, poolT, state, gmix, win, wpool, pscale, lbl, gon, wa, wb, wout)


def _mlp_kernel(x_ref, gmlp_ref, wup_ref, wdown_ref, gfin_ref, y_ref, *, ff_chunk):
    x = x_ref[...]
    h = _rms(x, gmlp_ref[...]).astype(BF16)
    acc = x
    for c in range(D_FF // ff_chunk):
        cs = slice(c * ff_chunk, (c + 1) * ff_chunk)
        a = jnp.maximum(_dot(h, wup_ref[:, cs]), 0.0)
        acc = acc + _dot((a * a).astype(BF16), wdown_ref[cs, :])
    y_ref[...] = _rms(acc, gfin_ref[...])


def _mlp(x, gmlp, wup, wdown, gfin, *, tm, ff_chunk=1024):
    n = x.shape[0]
    kern = functools.partial(_mlp_kernel, ff_chunk=ff_chunk)
    return pl.pallas_call(
        kern,
        grid=(n // tm,),
        in_specs=[
            pl.BlockSpec((tm, D_MODEL), lambda i: (i, 0)),
            _const_spec((1, D_MODEL)),
            _const_spec((D_MODEL, D_FF)),
            _const_spec((D_FF, D_MODEL)),
            _const_spec((1, D_MODEL)),
        ],
        out_specs=pl.BlockSpec((tm, D_MODEL), lambda i: (i, 0)),
        out_shape=jax.ShapeDtypeStruct((n, D_MODEL), F32),
        compiler_params=pltpu.CompilerParams(
            dimension_semantics=("arbitrary",),
            vmem_limit_bytes=VMEM_LIMIT),
        name="channel_mlp",
    )(x, gmlp, wup, wdown, gfin)


def kernel(x_prompt, x_sample, state_pool, state_hgrn, meta_tokens, g_mix, w_in, w_pool, pool_scale,
           hgrn_lb_logits, g_onorm, w_a, w_b, w_out, g_mlp, w_up, w_down, g_final):
    B, L, _ = x_prompt.shape
    NS = x_sample.shape[0]
    assert g_mix.shape[0] == 1, "single-layer trunk"
    gmix = g_mix[0][None, :]
    col = jnp.arange(N_IN)
    gate_col = ((col >= O_Q) & (col < O_V)) | (col >= O_OG)
    win = (w_in[0] * jnp.where(gate_col, 0.5, 1.0)[None, :]).astype(BF16)
    wpool = w_pool[0].astype(BF16)
    pscale = pool_scale[0][None, :]
    gon = g_onorm[0][None, :]
    wa = w_a[0].astype(BF16)
    wb = w_b[0].astype(BF16)
    wout = (0.5 * w_out[0]).astype(BF16)
    gmlp = g_mlp[0][None, :]
    wup = w_up[0].astype(BF16)
    wdown = w_down[0].astype(BF16)
    gfin = g_final[None, :]

    x1_p, pool_p, hgrn_p = _prompt_mixer(
        x_prompt, meta_tokens, gmix, win, wpool, pscale, hgrn_lb_logits, gon, wa, wb, wout, tile=512)
    y_p = _mlp(x1_p.reshape(B * L, D_MODEL), gmlp, wup, wdown, gfin, tm=1024)

    xs = x_sample.reshape(NS, D_MODEL)
    poolT = jnp.swapaxes(state_pool[0], 0, 1)
    x1_s, u_s, hgrn_s = _decode_mixer(
        xs, poolT, state_hgrn[0], gmix, win, wpool, pscale, hgrn_lb_logits, gon, wa, wb, wout, ts=16)
    y_s = _mlp(x1_s, gmlp, wup, wdown, gfin, tm=NS)
    pool_s = jnp.concatenate([state_pool[0][:, 1:, :], u_s[:, None, :]], axis=1)

    return (y_p.reshape(B, L, D_MODEL), y_s.reshape(NS, 1, D_MODEL),
            pool_p[None], hgrn_p[None], pool_s[None], hgrn_s[None])
```

```python
import functools

import jax
import jax.numpy as jnp
from jax import lax
from jax.experimental import pallas as pl
from jax.experimental.pallas import tpu as pltpu

D_MODEL = 1024
N_META = 16
POOL_WIDTH = 512
POOL_WINDOWS = (2, 4, 8, 16)
POOL_GROUP = 128
POOL_MAXW = 16
POOL_BUF = 15
HEADS = 4
DK = 128
DV = 128
KTOT = 512
HWIDTH = 512
D_FF = 4096
EPS = 1e-6
N_IN = 4608
O_U, O_Q, O_F, O_V, O_OG, O_GA, O_GB = 0, 512, 1024, 1536, 2048, 2560, 3584

SUB = 16
CHUNK = 256
GATE_PIECE = 256
GATE_ROWS = 256
LEVEL_HALVES = tuple(1 << i for i in range(CHUNK.bit_length() - 1))
VMEM_LIMIT = 56 * 1024 * 1024

F32 = jnp.float32
BF16 = jnp.bfloat16


def _rms(x, g):
    return x * lax.rsqrt(jnp.mean(x * x, axis=-1, keepdims=True) + EPS) * g


def _twice_sigmoid(hx):
    return jnp.tanh(hx) + 1.0


def _silu(hx):
    return hx * jnp.tanh(hx) + hx


def _dot(a, b):
    return jnp.dot(a, b, preferred_element_type=F32)


def _dot_nt(a, b):
    return lax.dot_general(a, b, (((1,), (1,)), ((), ())), preferred_element_type=F32)


def _dot_tn(a, b):
    return lax.dot_general(a, b, (((0,), (0,)), ((), ())), preferred_element_type=F32)


def _lower_bound(lb_logits):
    m = jnp.max(lb_logits, axis=0, keepdims=True)
    e = jnp.exp(lb_logits - m)
    return e[0:1, :] / jnp.sum(e, axis=0, keepdims=True)


def _gates(hq, hf, lb):
    q = _silu(hq)
    fg = 0.5 * (1.0 + lb) + (0.5 * (1.0 - lb)) * jnp.tanh(hf)
    return q, fg


def _cumsum_rows(g):
    n = g.shape[0]
    r = lax.broadcasted_iota(jnp.int32, (n, n), 0)
    c = lax.broadcasted_iota(jnp.int32, (n, n), 1)
    tril = (r >= c).astype(BF16)
    hi = g.astype(BF16)
    r1 = g - hi.astype(F32)
    mid = r1.astype(BF16)
    lo = (r1 - mid.astype(F32)).astype(BF16)
    return _dot(tril, hi) + _dot(tril, mid) + _dot(tril, lo)


def _level_map(n):
    t = lax.broadcasted_iota(jnp.int32, (n, n), 0)
    s = lax.broadcasted_iota(jnp.int32, (n, n), 1)
    x = t ^ s
    lvl = jnp.full((n, n), -1, jnp.int32)
    for li in range(n.bit_length() - 1):
        lvl = jnp.where((x >> li) == 1, li, lvl)
    return jnp.where(t > s, lvl, -1)


def _level_weights(m, r0, Gg, qg, kg, fgg, g_sc, row8, row16):
    if m == 1:
        return jnp.where((row16 & 1) != 0, qg * fgg, kg)
    if m >= SUB:
        blk = (r0 // (2 * m)) * (2 * m)
        ref = g_sc[blk + m - 1:blk + m, :]
        return (jnp.exp2(Gg - ref) * qg) if (r0 & m) else (jnp.exp2(ref - Gg) * kg)
    if m == SUB // 2:
        ref = g_sc[r0 + m - 1:r0 + m, :]
        return jnp.concatenate([jnp.exp2(ref - Gg[:m]) * kg[:m], jnp.exp2(Gg[m:] - ref) * qg[m:]], axis=0)
    halves = []
    for rb in (r0, r0 + 8):
        if m == 4:
            halves.append(jnp.broadcast_to(g_sc[rb + 3:rb + 4, :], (8, KTOT)))
        else:
            halves.append(jnp.where(row8 >= 4, g_sc[rb + 5:rb + 6, :], g_sc[rb + 1:rb + 2, :]))
    ref = jnp.concatenate(halves, axis=0)
    sel = jnp.where((row16 & m) != 0, qg, kg)
    return jnp.exp2(-jnp.abs(Gg - ref)) * sel


def _gates_stage(base, par, z_sc, lb, q_sc, k_sc, fg_sc, g_sc, od_sc, dec_sc):
    rs = slice(base, base + CHUNK)
    q, fg = _gates(z_sc[rs, O_Q:O_Q + KTOT], z_sc[rs, O_F:O_F + KTOT], lb)
    k = 1.0 - fg
    G = _cumsum_rows(jnp.log2(fg))
    q_sc[...] = q
    k_sc[...] = k
    fg_sc[...] = fg
    g_sc[...] = G
    v = z_sc[rs, O_V:O_V + HWIDTH]
    qk = q * k
    for hd in range(HEADS):
        sl = slice(hd * DK, (hd + 1) * DK)
        od_sc[par, :, sl] = jnp.sum(qk[:, sl], axis=-1, keepdims=True) * v[:, sl]
    dec_sc[par] = jnp.broadcast_to(jnp.exp2(G[CHUNK - 1:CHUNK, :]), (8, KTOT))


def _group_stage(j, par, q_sc, k_sc, fg_sc, g_sc, w_sc, qb_sc, kb_sc):
    row8 = lax.broadcasted_iota(jnp.int32, (8, 1), 0)
    row16 = lax.broadcasted_iota(jnp.int32, (SUB, 1), 0)
    r0 = j * SUB
    gs = slice(r0, r0 + SUB)
    Gg, qg, kg, fgg = g_sc[gs, :], q_sc[gs, :], k_sc[gs, :], fg_sc[gs, :]
    g_last = g_sc[CHUNK - 1:CHUNK, :]
    for li, m in enumerate(LEVEL_HALVES):
        w_sc[par, li, gs, :] = _level_weights(m, r0, Gg, qg, kg, fgg, g_sc, row8, row16).astype(BF16)
    qb_sc[par, gs, :] = (qg * jnp.exp2(Gg)).astype(BF16)
    kb_sc[par, gs, :] = (kg * jnp.exp2(g_last - Gg)).astype(BF16)


def _head_stage(base, hd, par, z_sc, lvl, w_sc, qb_sc, kb_sc, od_sc, dec_sc, o_sc, st_sc):
    rs = slice(base, base + CHUNK)
    sl = slice(hd * DK, (hd + 1) * DK)
    half = CHUNK // 2
    a_lo = jnp.zeros((half, half), F32)
    a_hi = jnp.zeros((half, half), F32)
    zero = jnp.zeros((half, DK), BF16)
    for li in range(len(LEVEL_HALVES) - 1):
        w_lo = w_sc[par, li, :half, sl]
        w_hi = w_sc[par, li, half:, sl]
        x = jnp.concatenate([w_lo, w_hi], axis=1)
        y = jnp.concatenate([jnp.concatenate([w_lo.T, zero], axis=1),
                             jnp.concatenate([zero, w_hi.T], axis=1)], axis=0)
        p = _dot(x, y)
        a_lo = jnp.where(lvl == li, p[:, :half], a_lo)
        a_hi = jnp.where(lvl == li, p[:, half:], a_hi)
    top = len(LEVEL_HALVES) - 1
    a_x = _dot_nt(w_sc[par, top, half:, sl], w_sc[par, top, :half, sl])
    A = jnp.concatenate([jnp.concatenate([a_lo, jnp.zeros((half, half), F32)], axis=1),
                         jnp.concatenate([a_x, a_hi], axis=1)], axis=0)
    v_bf = z_sc[rs, O_V + hd * DV:O_V + (hd + 1) * DV].astype(BF16)
    st = st_sc[hd]
    o_sc[rs, sl] = (_dot(A.astype(BF16), v_bf)
                    + _dot_nt(qb_sc[par, :, sl], st.astype(BF16))
                    + od_sc[par, :, sl])
    st_sc[hd] = st * dec_sc[par, 0:1, sl] + _dot_tn(v_bf, kb_sc[par, :, sl])


def _slice_copies(step, hbm_refs, vmem_refs, sems, to_hbm):
    copies = []
    for hbm, vmem, sem in zip(hbm_refs, vmem_refs, sems):
        rows = vmem.shape[0]
        window = hbm.at[pl.ds(pl.multiple_of(step * rows, rows), rows), :]
        copies.append(pltpu.make_async_copy(vmem, window, sem) if to_hbm
                      else pltpu.make_async_copy(window, vmem, sem))
    return copies


def _prompt_mixer_kernel(x_ref, meta_ref, gmix_ref, win_ref, wpool_ref, pscale_ref, lbl_ref,
                         gon_ref, wa_ref, wb_ref, wout_ref, wup_hbm, wdn_hbm,
                         x1_ref, pool_ref, hgrn_ref, wup16_hbm, wdn16_hbm,
                         z_sc, h_sc, ya_sc, ubuf, q_sc, k_sc, fg_sc, g_sc, od_sc, dec_sc,
                         w_sc, qb_sc, kb_sc, o_sc, st_sc, st_meta,
                         u_meta, up32, dn32, up16, dn16, cast_sems, *, tile):
    b = pl.program_id(0)
    t = pl.program_id(1)
    nt = pl.num_programs(1)
    lb = _lower_bound(lbl_ref[...])
    gmix = gmix_ref[...]

    step = b * nt + t
    last_step = pl.num_programs(0) * nt - 1
    in_sems, out_sems = (cast_sems.at[0], cast_sems.at[1]), (cast_sems.at[2], cast_sems.at[3])

    def cast_out(s):
        return _slice_copies(s, (wup16_hbm, wdn16_hbm), (up16, dn16), out_sems, to_hbm=True)

    cast_in = _slice_copies(step, (wup_hbm, wdn_hbm), (up32, dn32), in_sems, to_hbm=False)
    for cp in cast_in:
        cp.start()

    @pl.when((b == 0) & (t == 0))
    def _meta():
        hm = _rms(meta_ref[...], gmix).astype(BF16)
        zm = _dot(hm, win_ref[:, 0:O_OG])
        u_meta[...] = zm[:, O_U:O_U + POOL_WIDTH]
        _, fg = _gates(zm[:, O_Q:O_Q + KTOT], zm[:, O_F:O_F + KTOT], lb)
        G = _cumsum_rows(jnp.log2(fg))
        kt = (1.0 - fg) * jnp.exp2(G[N_META - 1:N_META, :] - G)
        v = zm[:, O_V:O_V + HWIDTH]
        for h in range(HEADS):
            sl = slice(h * DK, (h + 1) * DK)
            st_meta[h] = _dot_tn(v[:, sl].astype(BF16), kt[:, sl].astype(BF16))

    @pl.when(t == 0)
    def _init():
        st_sc[...] = st_meta[...]
        ubuf[0:POOL_MAXW, :] = u_meta[...]

    n_chunks = tile // CHUNK
    h_sc[...] = _rms(x_ref[0], gmix).astype(BF16)
    z_sc[:, 0:O_OG] = _dot(h_sc[...], win_ref[:, 0:O_OG])

    u = z_sc[:, O_U:O_U + POOL_WIDTH]
    ubuf[POOL_MAXW:POOL_MAXW + tile, :] = u
    ya_parts = []
    for gi, w in enumerate(POOL_WINDOWS):
        cs = slice(gi * POOL_GROUP, (gi + 1) * POOL_GROUP)
        wsum = ubuf[:, cs]
        span = 1
        while span < w:
            wsum = wsum + pltpu.roll(wsum, span, 0)
            span *= 2
        pooled = wsum[POOL_MAXW:, :] / float(w) - u[:, cs]
        ya_parts.append(_dot(pooled.astype(BF16), wpool_ref[gi]))
    ya_pre = jnp.concatenate(ya_parts, axis=-1) * pscale_ref[...]
    ya_sc[...] = _dot(ya_pre.astype(BF16), wa_ref[...])
    ubuf[0:POOL_MAXW, :] = ubuf[tile:tile + POOL_MAXW, :]

    def gate_piece(r0, c0):
        z_sc[r0:r0 + GATE_ROWS, c0:c0 + GATE_PIECE] = _dot(
            h_sc[r0:r0 + GATE_ROWS, :], win_ref[:, c0:c0 + GATE_PIECE])
    pending = [functools.partial(gate_piece, r0, c0)
               for c0 in range(O_OG, N_IN, GATE_PIECE) for r0 in range(0, tile, GATE_ROWS)]
    n_pieces = len(pending)
    n_groups = CHUNK // SUB
    n_slots = n_chunks * (n_groups + HEADS)
    slot = [0]

    def interleave():
        slot[0] += 1
        while len(pending) > n_pieces - (-(-slot[0] * n_pieces // n_slots)):
            pending.pop(0)()

    def gates(c):
        _gates_stage(c * CHUNK, c % 2, z_sc, lb, q_sc, k_sc, fg_sc, g_sc, od_sc, dec_sc)

    def group(c, j):
        _group_stage(j, c % 2, q_sc, k_sc, fg_sc, g_sc, w_sc, qb_sc, kb_sc)
        interleave()

    def head(c, hd):
        _head_stage(c * CHUNK, hd, c % 2, z_sc, lvl, w_sc, qb_sc, kb_sc, od_sc, dec_sc, o_sc, st_sc)
        interleave()

    lvl = _level_map(CHUNK // 2)
    gates(0)
    for j in range(n_groups):
        group(0, j)
    for c in range(n_chunks):
        if c + 1 < n_chunks:
            gates(c + 1)
        for hd in range(HEADS):
            head(c, hd)
            if c + 1 < n_chunks:
                for j in range(hd * n_groups // HEADS, (hd + 1) * n_groups // HEADS):
                    group(c + 1, j)
    assert not pending and slot[0] == n_slots

    gon = gon_ref[...]
    o_parts = []
    for hd in range(HEADS):
        sl = slice(hd * DV, (hd + 1) * DV)
        og = z_sc[:, O_OG + hd * DV:O_OG + (hd + 1) * DV]
        o_parts.append(_rms(o_sc[:, sl], gon) * _silu(og))
    yb = _dot(jnp.concatenate(o_parts, axis=-1).astype(BF16), wb_ref[...])
    m = (_twice_sigmoid(z_sc[:, O_GA:O_GA + D_MODEL]) * ya_sc[...]
         + _twice_sigmoid(z_sc[:, O_GB:O_GB + D_MODEL]) * yb)
    x1_ref[0] = x_ref[0] + _dot(m.astype(BF16), wout_ref[...])

    for cp in cast_in:
        cp.wait()

    @pl.when(step > 0)
    def _drain_previous():
        for cp in cast_out(step - 1):
            cp.wait()

    up16[...] = up32[...].astype(BF16)
    dn16[...] = dn32[...].astype(BF16)
    for cp in cast_out(step):
        cp.start()

    @pl.when(step == last_step)
    def _drain_last():
        for cp in cast_out(step):
            cp.wait()

    @pl.when(t == nt - 1)
    def _state_out():
        pool_ref[0] = ubuf[1:POOL_MAXW, :]
        for hd in range(HEADS):
            hgrn_ref[0, hd] = st_sc[hd].T


def _const_spec(shape):
    nd = len(shape)
    return pl.BlockSpec(shape, lambda *_: (0,) * nd, pipeline_mode=pl.Buffered(1))


def _prompt_mixer(x, meta, gmix, win, wpool, pscale, lbl, gon, wa, wb, wout, wup, wdown, *, tile):
    B, L, _ = x.shape
    nt = L // tile
    n_steps = B * nt
    assert wup.shape[0] % (SUB * n_steps) == 0 and wdown.shape[0] % (SUB * n_steps) == 0
    up_rows, dn_rows = wup.shape[0] // n_steps, wdown.shape[0] // n_steps
    kern = functools.partial(_prompt_mixer_kernel, tile=tile)
    return pl.pallas_call(
        kern,
        grid=(B, nt),
        in_specs=[
            pl.BlockSpec((1, tile, D_MODEL), lambda b, t: (b, t, 0)),
            _const_spec((N_META, D_MODEL)),
            _const_spec((1, D_MODEL)),
            _const_spec((D_MODEL, N_IN)),
            _const_spec((len(POOL_WINDOWS), POOL_GROUP, POOL_GROUP)),
            _const_spec((1, POOL_WIDTH)),
            _const_spec((2, KTOT)),
            _const_spec((1, DV)),
            _const_spec((POOL_WIDTH, D_MODEL)),
            _const_spec((HWIDTH, D_MODEL)),
            _const_spec((D_MODEL, D_MODEL)),
            pl.BlockSpec(memory_space=pl.ANY),
            pl.BlockSpec(memory_space=pl.ANY),
        ],
        out_specs=[
            pl.BlockSpec((1, tile, D_MODEL), lambda b, t: (b, t, 0)),
            pl.BlockSpec((1, POOL_BUF, POOL_WIDTH), lambda b, t: (b, 0, 0)),
            pl.BlockSpec((1, HEADS, DK, DV), lambda b, t: (b, 0, 0, 0)),
            pl.BlockSpec(memory_space=pl.ANY),
            pl.BlockSpec(memory_space=pl.ANY),
        ],
        out_shape=[
            jax.ShapeDtypeStruct((B, L, D_MODEL), F32),
            jax.ShapeDtypeStruct((B, POOL_BUF, POOL_WIDTH), F32),
            jax.ShapeDtypeStruct((B, HEADS, DK, DV), F32),
            jax.ShapeDtypeStruct(wup.shape, BF16),
            jax.ShapeDtypeStruct(wdown.shape, BF16),
        ],
        scratch_shapes=[
            pltpu.VMEM((tile, N_IN), F32),
            pltpu.VMEM((tile, D_MODEL), BF16),
            pltpu.VMEM((tile, D_MODEL), F32),
            pltpu.VMEM((tile + POOL_MAXW, POOL_WIDTH), F32),
            pltpu.VMEM((CHUNK, KTOT), F32),
            pltpu.VMEM((CHUNK, KTOT), F32),
            pltpu.VMEM((CHUNK, KTOT), F32),
            pltpu.VMEM((CHUNK, KTOT), F32),
            pltpu.VMEM((2, CHUNK, HWIDTH), F32),
            pltpu.VMEM((2, 8, KTOT), F32),
            pltpu.VMEM((2, len(LEVEL_HALVES), CHUNK, KTOT), BF16),
            pltpu.VMEM((2, CHUNK, KTOT), BF16),
            pltpu.VMEM((2, CHUNK, KTOT), BF16),
            pltpu.VMEM((tile, HWIDTH), F32),
            pltpu.VMEM((HEADS, DV, DK), F32),
            pltpu.VMEM((HEADS, DV, DK), F32),
            pltpu.VMEM((N_META, POOL_WIDTH), F32),
            pltpu.VMEM((up_rows, wup.shape[1]), F32),
            pltpu.VMEM((dn_rows, wdown.shape[1]), F32),
            pltpu.VMEM((up_rows, wup.shape[1]), BF16),
            pltpu.VMEM((dn_rows, wdown.shape[1]), BF16),
            pltpu.SemaphoreType.DMA((4,)),
        ],
        compiler_params=pltpu.CompilerParams(
            dimension_semantics=("arbitrary", "arbitrary"),
            vmem_limit_bytes=VMEM_LIMIT),
        name="prompt_mixer",
    )(x, meta, gmix, win, wpool, pscale, lbl, gon, wa, wb, wout, wup, wdown)


def _decode_mixer_kernel(x_ref, poolT_ref, s_ref, gmix_ref, win_ref, wpool_ref, pscale_ref,
                         lbl_ref, gon_ref, wa_ref, wb_ref, wout_ref,
                         x1_ref, unew_ref, snew_ref,
                         z_sc, o_sc, *, ts):
    i = pl.program_id(0)
    n = pl.num_programs(0)
    lb = _lower_bound(lbl_ref[...])

    @pl.when(i == 0)
    def _in_proj():
        h = _rms(x_ref[...], gmix_ref[...]).astype(BF16)
        z_sc[...] = _dot(h, win_ref[...])

    r = pl.ds(pl.multiple_of(i * ts, ts), ts)
    q, fg = _gates(z_sc[r, O_Q:O_Q + KTOT], z_sc[r, O_F:O_F + KTOT], lb)
    k = 1.0 - fg
    v = z_sc[r, O_V:O_V + HWIDTH]
    row = lax.broadcasted_iota(jnp.int32, (ts, 1), 0)
    for hd in range(HEADS):
        sl = slice(hd * DK, (hd + 1) * DK)
        fT = fg[:, sl].T
        kT = k[:, sl].T
        qT = q[:, sl].T
        o = jnp.zeros((ts, DV), F32)
        for s in range(ts):
            s_new = fT[:, s:s + 1] * s_ref[s, hd] + kT[:, s:s + 1] * v[s:s + 1, sl]
            snew_ref[s, hd] = s_new
            o_row = jnp.sum(qT[:, s:s + 1] * s_new, axis=0, keepdims=True)
            o = jnp.where(row == s, o_row, o)
        o_sc[r, sl] = o

    @pl.when(i == n - 1)
    def _out_proj():
        x = x_ref[...]
        u = z_sc[:, O_U:O_U + POOL_WIDTH]
        unew_ref[...] = u
        ya_parts = []
        for gi, w in enumerate(POOL_WINDOWS):
            cs = slice(gi * POOL_GROUP, (gi + 1) * POOL_GROUP)
            wsum = u[:, cs]
            for j in range(1, w):
                wsum = wsum + poolT_ref[POOL_BUF - j, :, cs]
            pooled = wsum / float(w) - u[:, cs]
            ya_parts.append(_dot(pooled.astype(BF16), wpool_ref[gi]))
        ya_pre = jnp.concatenate(ya_parts, axis=-1) * pscale_ref[...]
        ya = _dot(ya_pre.astype(BF16), wa_ref[...])
        gon = gon_ref[...]
        o_parts = []
        for hd in range(HEADS):
            sl = slice(hd * DV, (hd + 1) * DV)
            og = z_sc[:, O_OG + hd * DV:O_OG + (hd + 1) * DV]
            o_parts.append(_rms(o_sc[:, sl], gon) * _silu(og))
        yb = _dot(jnp.concatenate(o_parts, axis=-1).astype(BF16), wb_ref[...])
        m = (_twice_sigmoid(z_sc[:, O_GA:O_GA + D_MODEL]) * ya
             + _twice_sigmoid(z_sc[:, O_GB:O_GB + D_MODEL]) * yb)
        x1_ref[...] = x + _dot(m.astype(BF16), wout_ref[...])


def _decode_mixer(x, poolT, state, gmix, win, wpool, pscale, lbl, gon, wa, wb, wout, *, ts):
    n = x.shape[0]
    kern = functools.partial(_decode_mixer_kernel, ts=ts)
    return pl.pallas_call(
        kern,
        grid=(n // ts,),
        in_specs=[
            _const_spec((n, D_MODEL)),
            _const_spec((POOL_BUF, n, POOL_WIDTH)),
            pl.BlockSpec((ts, HEADS, DK, DV), lambda i: (i, 0, 0, 0)),
            _const_spec((1, D_MODEL)),
            _const_spec((D_MODEL, N_IN)),
            _const_spec((len(POOL_WINDOWS), POOL_GROUP, POOL_GROUP)),
            _const_spec((1, POOL_WIDTH)),
            _const_spec((2, KTOT)),
            _const_spec((1, DV)),
            _const_spec((POOL_WIDTH, D_MODEL)),
            _const_spec((HWIDTH, D_MODEL)),
            _const_spec((D_MODEL, D_MODEL)),
        ],
        out_specs=[
            pl.BlockSpec((n, D_MODEL), lambda i: (0, 0)),
            pl.BlockSpec((n, POOL_WIDTH), lambda i: (0, 0)),
            pl.BlockSpec((ts, HEADS, DK, DV), lambda i: (i, 0, 0, 0)),
        ],
        out_shape=[
            jax.ShapeDtypeStruct((n, D_MODEL), F32),
            jax.ShapeDtypeStruct((n, POOL_WIDTH), F32),
            jax.ShapeDtypeStruct((n, HEADS, DK, DV), F32),
        ],
        scratch_shapes=[
            pltpu.VMEM((n, N_IN), F32),
            pltpu.VMEM((n, HWIDTH), F32),
        ],
        compiler_params=pltpu.CompilerParams(
            dimension_semantics=("arbitrary",),
            vmem_limit_bytes=VMEM_LIMIT),
        name="decode_mixer",
    )(x, poolT, state, gmix, win, wpool, pscale, lbl, gon, wa, wb, wout)


def _mlp_kernel(x_ref, gmlp_ref, wup_ref, wdown_ref, gfin_ref, y_ref, *, ff_chunk):
    x = x_ref[...]
    h = _rms(x, gmlp_ref[...]).astype(BF16)
    acc = x
    for c in range(D_FF // ff_chunk):
        cs = slice(c * ff_chunk, (c + 1) * ff_chunk)
        a = jnp.maximum(_dot(h, wup_ref[:, cs]), 0.0)
        acc = acc + _dot((a * a).astype(BF16), wdown_ref[cs, :])
    y_ref[...] = _rms(acc, gfin_ref[...])


def _mlp(x, gmlp, wup, wdown, gfin, *, tm, ff_chunk=1024):
    n = x.shape[0]
    kern = functools.partial(_mlp_kernel, ff_chunk=ff_chunk)
    return pl.pallas_call(
        kern,
        grid=(n // tm,),
        in_specs=[
            pl.BlockSpec((tm, D_MODEL), lambda i: (i, 0)),
            _const_spec((1, D_MODEL)),
            _const_spec((D_MODEL, D_FF)),
            _const_spec((D_FF, D_MODEL)),
            _const_spec((1, D_MODEL)),
        ],
        out_specs=pl.BlockSpec((tm, D_MODEL), lambda i: (i, 0)),
        out_shape=jax.ShapeDtypeStruct((n, D_MODEL), F32),
        compiler_params=pltpu.CompilerParams(
            dimension_semantics=("arbitrary",),
            vmem_limit_bytes=VMEM_LIMIT),
        name="channel_mlp",
    )(x, gmlp, wup, wdown, gfin)


def kernel(x_prompt, x_sample, state_pool, state_hgrn, meta_tokens, g_mix, w_in, w_pool, pool_scale,
           hgrn_lb_logits, g_onorm, w_a, w_b, w_out, g_mlp, w_up, w_down, g_final):
    B, L, _ = x_prompt.shape
    NS = x_sample.shape[0]
    assert g_mix.shape[0] == 1, "single-layer trunk"
    gmix = g_mix[0][None, :]
    col = jnp.arange(N_IN)
    gate_col = ((col >= O_Q) & (col < O_V)) | (col >= O_OG)
    win = (w_in[0] * jnp.where(gate_col, 0.5, 1.0)[None, :]).astype(BF16)
    wpool = w_pool[0].astype(BF16)
    pscale = pool_scale[0][None, :]
    gon = g_onorm[0][None, :]
    wa = w_a[0].astype(BF16)
    wb = w_b[0].astype(BF16)
    wout = (0.5 * w_out[0]).astype(BF16)
    gmlp = g_mlp[0][None, :]
    gfin = g_final[None, :]

    x1_p, pool_p, hgrn_p, wup, wdown = _prompt_mixer(
        x_prompt, meta_tokens, gmix, win, wpool, pscale, hgrn_lb_logits, gon, wa, wb, wout,
        w_up[0], w_down[0], tile=512)
    y_p = _mlp(x1_p.reshape(B * L, D_MODEL), gmlp, wup, wdown, gfin, tm=512)

    xs = x_sample.reshape(NS, D_MODEL)
    poolT = jnp.swapaxes(state_pool[0], 0, 1)
    x1_s, u_s, hgrn_s = _decode_mixer(
        xs, poolT, state_hgrn[0], gmix, win, wpool, pscale, hgrn_lb_logits, gon, wa, wb, wout, ts=16)
    y_s = _mlp(x1_s, gmlp, wup, wdown, gfin, tm=NS)
    pool_s = jnp.concatenate([state_pool[0][:, 1:, :], u_s[:, None, :]], axis=1)

    return (y_p.reshape(B, L, D_MODEL), y_s.reshape(NS, 1, D_MODEL),
            pool_p[None], hgrn_p[None], pool_s[None], hgrn_s[None])
```

```python
import functools

import jax
import jax.numpy as jnp
from jax import lax
from jax.experimental import pallas as pl
from jax.experimental.pallas import tpu as pltpu

D_MODEL = 1024
N_META = 16
POOL_WIDTH = 512
POOL_WINDOWS = (2, 4, 8, 16)
POOL_GROUP = 128
POOL_MAXW = 16
POOL_BUF = 15
HEADS = 4
DK = 128
DV = 128
KTOT = 512
HWIDTH = 512
D_FF = 4096
EPS = 1e-6
N_IN = 4608
O_U, O_Q, O_F, O_V, O_OG, O_GA, O_GB = 0, 512, 1024, 1536, 2048, 2560, 3584

SUB = 16
CHUNK = 256
GATE_PIECE = 256
GATE_ROWS = 256
LEVEL_HALVES = tuple(1 << i for i in range(CHUNK.bit_length() - 1))
VMEM_LIMIT = 56 * 1024 * 1024

F32 = jnp.float32
BF16 = jnp.bfloat16


def _rms(x, g):
    return x * lax.rsqrt(jnp.mean(x * x, axis=-1, keepdims=True) + EPS) * g


def _twice_sigmoid(hx):
    return jnp.tanh(hx) + 1.0


def _silu(hx):
    return hx * jnp.tanh(hx) + hx


def _dot(a, b):
    return jnp.dot(a, b, preferred_element_type=F32)


def _dot_nt(a, b):
    return lax.dot_general(a, b, (((1,), (1,)), ((), ())), preferred_element_type=F32)


def _dot_tn(a, b):
    return lax.dot_general(a, b, (((0,), (0,)), ((), ())), preferred_element_type=F32)


def _group_maps(pooled_groups, wpool_ref):
    outs = []
    for p in range(len(pooled_groups) // 2):
        pair = jnp.concatenate(pooled_groups[2 * p:2 * p + 2], axis=-1)
        outs.append(_dot(pair, wpool_ref[p]))
    return jnp.concatenate(outs, axis=-1)


def _lower_bound(lb_logits):
    m = jnp.max(lb_logits, axis=0, keepdims=True)
    e = jnp.exp(lb_logits - m)
    return e[0:1, :] / jnp.sum(e, axis=0, keepdims=True)


def _gates(hq, hf, lb):
    q = _silu(hq)
    fg = 0.5 * (1.0 + lb) + (0.5 * (1.0 - lb)) * jnp.tanh(hf)
    return q, fg


def _cumsum_rows(g):
    n = g.shape[0]
    r = lax.broadcasted_iota(jnp.int32, (n, n), 0)
    c = lax.broadcasted_iota(jnp.int32, (n, n), 1)
    tril = (r >= c).astype(BF16)
    hi = g.astype(BF16)
    r1 = g - hi.astype(F32)
    mid = r1.astype(BF16)
    lo = (r1 - mid.astype(F32)).astype(BF16)
    return _dot(tril, hi) + _dot(tril, mid) + _dot(tril, lo)


def _level_map(n):
    t = lax.broadcasted_iota(jnp.int32, (n, n), 0)
    s = lax.broadcasted_iota(jnp.int32, (n, n), 1)
    x = t ^ s
    lvl = jnp.full((n, n), -1, jnp.int32)
    for li in range(n.bit_length() - 1):
        lvl = jnp.where((x >> li) == 1, li, lvl)
    return jnp.where(t > s, lvl, -1)


def _level_weights(m, r0, Gg, qg, kg, fgg, g_sc, row8, row16):
    if m == 1:
        return jnp.where((row16 & 1) != 0, qg * fgg, kg)
    if m >= SUB:
        blk = (r0 // (2 * m)) * (2 * m)
        ref = g_sc[blk + m - 1:blk + m, :]
        return (jnp.exp2(Gg - ref) * qg) if (r0 & m) else (jnp.exp2(ref - Gg) * kg)
    if m == SUB // 2:
        ref = g_sc[r0 + m - 1:r0 + m, :]
        return jnp.concatenate([jnp.exp2(ref - Gg[:m]) * kg[:m], jnp.exp2(Gg[m:] - ref) * qg[m:]], axis=0)
    halves = []
    for rb in (r0, r0 + 8):
        if m == 4:
            halves.append(jnp.broadcast_to(g_sc[rb + 3:rb + 4, :], (8, KTOT)))
        else:
            halves.append(jnp.where(row8 >= 4, g_sc[rb + 5:rb + 6, :], g_sc[rb + 1:rb + 2, :]))
    ref = jnp.concatenate(halves, axis=0)
    sel = jnp.where((row16 & m) != 0, qg, kg)
    return jnp.exp2(-jnp.abs(Gg - ref)) * sel


def _gates_stage(base, par, z_sc, lb, q_sc, k_sc, fg_sc, g_sc, od_sc, dec_sc):
    rs = slice(base, base + CHUNK)
    q, fg = _gates(z_sc[rs, O_Q:O_Q + KTOT], z_sc[rs, O_F:O_F + KTOT], lb)
    k = 1.0 - fg
    G = _cumsum_rows(jnp.log2(fg))
    q_sc[...] = q
    k_sc[...] = k
    fg_sc[...] = fg
    g_sc[...] = G
    v = z_sc[rs, O_V:O_V + HWIDTH]
    qk = q * k
    for hd in range(HEADS):
        sl = slice(hd * DK, (hd + 1) * DK)
        od_sc[par, :, sl] = jnp.sum(qk[:, sl], axis=-1, keepdims=True) * v[:, sl]
    dec_sc[par] = jnp.broadcast_to(jnp.exp2(G[CHUNK - 1:CHUNK, :]), (8, KTOT))


def _group_stage(j, par, q_sc, k_sc, fg_sc, g_sc, w_sc, qb_sc, kb_sc):
    row8 = lax.broadcasted_iota(jnp.int32, (8, 1), 0)
    row16 = lax.broadcasted_iota(jnp.int32, (SUB, 1), 0)
    r0 = j * SUB
    gs = slice(r0, r0 + SUB)
    Gg, qg, kg, fgg = g_sc[gs, :], q_sc[gs, :], k_sc[gs, :], fg_sc[gs, :]
    g_last = g_sc[CHUNK - 1:CHUNK, :]
    for li, m in enumerate(LEVEL_HALVES):
        w_sc[par, li, gs, :] = _level_weights(m, r0, Gg, qg, kg, fgg, g_sc, row8, row16).astype(BF16)
    qb_sc[par, gs, :] = (qg * jnp.exp2(Gg)).astype(BF16)
    kb_sc[par, gs, :] = (kg * jnp.exp2(g_last - Gg)).astype(BF16)


def _head_stage(base, hd, par, z_sc, lvl, w_sc, qb_sc, kb_sc, od_sc, dec_sc, o_sc, st_sc):
    rs = slice(base, base + CHUNK)
    sl = slice(hd * DK, (hd + 1) * DK)
    half = CHUNK // 2
    a_lo = jnp.zeros((half, half), F32)
    a_hi = jnp.zeros((half, half), F32)
    zero = jnp.zeros((half, DK), BF16)
    for li in range(len(LEVEL_HALVES) - 1):
        w_lo = w_sc[par, li, :half, sl]
        w_hi = w_sc[par, li, half:, sl]
        x = jnp.concatenate([w_lo, w_hi], axis=1)
        y = jnp.concatenate([jnp.concatenate([w_lo.T, zero], axis=1),
                             jnp.concatenate([zero, w_hi.T], axis=1)], axis=0)
        p = _dot(x, y)
        a_lo = jnp.where(lvl == li, p[:, :half], a_lo)
        a_hi = jnp.where(lvl == li, p[:, half:], a_hi)
    top = len(LEVEL_HALVES) - 1
    a_x = _dot_nt(w_sc[par, top, half:, sl], w_sc[par, top, :half, sl])
    A = jnp.concatenate([jnp.concatenate([a_lo, jnp.zeros((half, half), F32)], axis=1),
                         jnp.concatenate([a_x, a_hi], axis=1)], axis=0)
    v_bf = z_sc[rs, O_V + hd * DV:O_V + (hd + 1) * DV].astype(BF16)
    st = st_sc[hd]
    o_sc[rs, sl] = (_dot(A.astype(BF16), v_bf)
                    + _dot_nt(qb_sc[par, :, sl], st.astype(BF16))
                    + od_sc[par, :, sl])
    st_sc[hd] = st * dec_sc[par, 0:1, sl] + _dot_tn(v_bf, kb_sc[par, :, sl])


def _slice_copies(step, hbm_refs, vmem_refs, sems, to_hbm):
    copies = []
    for hbm, vmem, sem in zip(hbm_refs, vmem_refs, sems):
        rows = vmem.shape[0]
        window = hbm.at[pl.ds(pl.multiple_of(step * rows, rows), rows), :]
        copies.append(pltpu.make_async_copy(vmem, window, sem) if to_hbm
                      else pltpu.make_async_copy(window, vmem, sem))
    return copies


def _prompt_mixer_kernel(x_ref, meta_ref, gmix_ref, win_ref, wpool_ref, pscale_ref, lbl_ref,
                         gon_ref, wa_ref, wb_ref, wout_ref, wup_hbm, wdn_hbm,
                         x1_ref, pool_ref, hgrn_ref, wup16_hbm, wdn16_hbm,
                         z_sc, h_sc, ya_sc, ubuf, q_sc, k_sc, fg_sc, g_sc, od_sc, dec_sc,
                         w_sc, qb_sc, kb_sc, o_sc, st_sc, st_meta,
                         u_meta, up32, dn32, up16, dn16, cast_sems, *, tile):
    b = pl.program_id(0)
    t = pl.program_id(1)
    nt = pl.num_programs(1)
    lb = _lower_bound(lbl_ref[...])
    gmix = gmix_ref[...]

    step = b * nt + t
    last_step = pl.num_programs(0) * nt - 1
    in_sems, out_sems = (cast_sems.at[0], cast_sems.at[1]), (cast_sems.at[2], cast_sems.at[3])

    def cast_out(s):
        return _slice_copies(s, (wup16_hbm, wdn16_hbm), (up16, dn16), out_sems, to_hbm=True)

    cast_in = _slice_copies(step, (wup_hbm, wdn_hbm), (up32, dn32), in_sems, to_hbm=False)
    for cp in cast_in:
        cp.start()

    @pl.when((b == 0) & (t == 0))
    def _meta():
        hm = _rms(meta_ref[...], gmix).astype(BF16)
        zm = _dot(hm, win_ref[:, 0:O_OG])
        u_meta[...] = zm[:, O_U:O_U + POOL_WIDTH]
        _, fg = _gates(zm[:, O_Q:O_Q + KTOT], zm[:, O_F:O_F + KTOT], lb)
        G = _cumsum_rows(jnp.log2(fg))
        kt = (1.0 - fg) * jnp.exp2(G[N_META - 1:N_META, :] - G)
        v = zm[:, O_V:O_V + HWIDTH]
        for h in range(HEADS):
            sl = slice(h * DK, (h + 1) * DK)
            st_meta[h] = _dot_tn(v[:, sl].astype(BF16), kt[:, sl].astype(BF16))

    @pl.when(t == 0)
    def _init():
        st_sc[...] = st_meta[...]
        ubuf[0:POOL_MAXW, :] = u_meta[...]

    n_chunks = tile // CHUNK
    h_sc[...] = _rms(x_ref[0], gmix).astype(BF16)
    z_sc[:, 0:O_OG] = _dot(h_sc[...], win_ref[:, 0:O_OG])

    u = z_sc[:, O_U:O_U + POOL_WIDTH]
    ubuf[POOL_MAXW:POOL_MAXW + tile, :] = u
    ya_parts = []
    for gi, w in enumerate(POOL_WINDOWS):
        cs = slice(gi * POOL_GROUP, (gi + 1) * POOL_GROUP)
        wsum = ubuf[:, cs]
        span = 1
        while span < w:
            wsum = wsum + pltpu.roll(wsum, span, 0)
            span *= 2
        pooled = wsum[POOL_MAXW:, :] / float(w) - u[:, cs]
        ya_parts.append(pooled.astype(BF16))
    ya_pre = _group_maps(ya_parts, wpool_ref) * pscale_ref[...]
    ya_sc[...] = _dot(ya_pre.astype(BF16), wa_ref[...])
    ubuf[0:POOL_MAXW, :] = ubuf[tile:tile + POOL_MAXW, :]

    def gate_piece(r0, c0):
        z_sc[r0:r0 + GATE_ROWS, c0:c0 + GATE_PIECE] = _dot(
            h_sc[r0:r0 + GATE_ROWS, :], win_ref[:, c0:c0 + GATE_PIECE])
    pending = [functools.partial(gate_piece, r0, c0)
               for c0 in range(O_OG, N_IN, GATE_PIECE) for r0 in range(0, tile, GATE_ROWS)]
    n_pieces = len(pending)
    n_groups = CHUNK // SUB
    n_slots = n_chunks * (n_groups + HEADS)
    slot = [0]

    def interleave():
        slot[0] += 1
        while len(pending) > n_pieces - (-(-slot[0] * n_pieces // n_slots)):
            pending.pop(0)()

    def gates(c):
        _gates_stage(c * CHUNK, c % 2, z_sc, lb, q_sc, k_sc, fg_sc, g_sc, od_sc, dec_sc)

    def group(c, j):
        _group_stage(j, c % 2, q_sc, k_sc, fg_sc, g_sc, w_sc, qb_sc, kb_sc)
        interleave()

    def head(c, hd):
        _head_stage(c * CHUNK, hd, c % 2, z_sc, lvl, w_sc, qb_sc, kb_sc, od_sc, dec_sc, o_sc, st_sc)
        interleave()

    lvl = _level_map(CHUNK // 2)
    gates(0)
    for j in range(n_groups):
        group(0, j)
    for c in range(n_chunks):
        if c + 1 < n_chunks:
            gates(c + 1)
        for hd in range(HEADS):
            head(c, hd)
            if c + 1 < n_chunks:
                for j in range(hd * n_groups // HEADS, (hd + 1) * n_groups // HEADS):
                    group(c + 1, j)
    assert not pending and slot[0] == n_slots

    gon = gon_ref[...]
    o_parts = []
    for hd in range(HEADS):
        sl = slice(hd * DV, (hd + 1) * DV)
        og = z_sc[:, O_OG + hd * DV:O_OG + (hd + 1) * DV]
        o_parts.append(_rms(o_sc[:, sl], gon) * _silu(og))
    yb = _dot(jnp.concatenate(o_parts, axis=-1).astype(BF16), wb_ref[...])
    m = (_twice_sigmoid(z_sc[:, O_GA:O_GA + D_MODEL]) * ya_sc[...]
         + _twice_sigmoid(z_sc[:, O_GB:O_GB + D_MODEL]) * yb)
    x1_ref[0] = x_ref[0] + _dot(m.astype(BF16), wout_ref[...])

    for cp in cast_in:
        cp.wait()

    @pl.when(step > 0)
    def _drain_previous():
        for cp in cast_out(step - 1):
            cp.wait()

    up16[...] = up32[...].astype(BF16)
    dn16[...] = dn32[...].astype(BF16)
    for cp in cast_out(step):
        cp.start()

    @pl.when(step == last_step)
    def _drain_last():
        for cp in cast_out(step):
            cp.wait()

    @pl.when(t == nt - 1)
    def _state_out():
        pool_ref[0] = ubuf[1:POOL_MAXW, :]
        for hd in range(HEADS):
            hgrn_ref[0, hd] = st_sc[hd].T


def _const_spec(shape):
    nd = len(shape)
    return pl.BlockSpec(shape, lambda *_: (0,) * nd, pipeline_mode=pl.Buffered(1))


def _prompt_mixer(x, meta, gmix, win, wpool, pscale, lbl, gon, wa, wb, wout, wup, wdown, *, tile):
    B, L, _ = x.shape
    nt = L // tile
    n_steps = B * nt
    assert wup.shape[0] % (SUB * n_steps) == 0 and wdown.shape[0] % (SUB * n_steps) == 0
    up_rows, dn_rows = wup.shape[0] // n_steps, wdown.shape[0] // n_steps
    kern = functools.partial(_prompt_mixer_kernel, tile=tile)
    return pl.pallas_call(
        kern,
        grid=(B, nt),
        in_specs=[
            pl.BlockSpec((1, tile, D_MODEL), lambda b, t: (b, t, 0)),
            _const_spec((N_META, D_MODEL)),
            _const_spec((1, D_MODEL)),
            _const_spec((D_MODEL, N_IN)),
            _const_spec((len(POOL_WINDOWS) // 2, 2 * POOL_GROUP, 2 * POOL_GROUP)),
            _const_spec((1, POOL_WIDTH)),
            _const_spec((2, KTOT)),
            _const_spec((1, DV)),
            _const_spec((POOL_WIDTH, D_MODEL)),
            _const_spec((HWIDTH, D_MODEL)),
            _const_spec((D_MODEL, D_MODEL)),
            pl.BlockSpec(memory_space=pl.ANY),
            pl.BlockSpec(memory_space=pl.ANY),
        ],
        out_specs=[
            pl.BlockSpec((1, tile, D_MODEL), lambda b, t: (b, t, 0)),
            pl.BlockSpec((1, POOL_BUF, POOL_WIDTH), lambda b, t: (b, 0, 0)),
            pl.BlockSpec((1, HEADS, DK, DV), lambda b, t: (b, 0, 0, 0)),
            pl.BlockSpec(memory_space=pl.ANY),
            pl.BlockSpec(memory_space=pl.ANY),
        ],
        out_shape=[
            jax.ShapeDtypeStruct((B, L, D_MODEL), F32),
            jax.ShapeDtypeStruct((B, POOL_BUF, POOL_WIDTH), F32),
            jax.ShapeDtypeStruct((B, HEADS, DK, DV), F32),
            jax.ShapeDtypeStruct(wup.shape, BF16),
            jax.ShapeDtypeStruct(wdown.shape, BF16),
        ],
        scratch_shapes=[
            pltpu.VMEM((tile, N_IN), F32),
            pltpu.VMEM((tile, D_MODEL), BF16),
            pltpu.VMEM((tile, D_MODEL), F32),
            pltpu.VMEM((tile + POOL_MAXW, POOL_WIDTH), F32),
            pltpu.VMEM((CHUNK, KTOT), F32),
            pltpu.VMEM((CHUNK, KTOT), F32),
            pltpu.VMEM((CHUNK, KTOT), F32),
            pltpu.VMEM((CHUNK, KTOT), F32),
            pltpu.VMEM((2, CHUNK, HWIDTH), F32),
            pltpu.VMEM((2, 8, KTOT), F32),
            pltpu.VMEM((2, len(LEVEL_HALVES), CHUNK, KTOT), BF16),
            pltpu.VMEM((2, CHUNK, KTOT), BF16),
            pltpu.VMEM((2, CHUNK, KTOT), BF16),
            pltpu.VMEM((tile, HWIDTH), F32),
            pltpu.VMEM((HEADS, DV, DK), F32),
            pltpu.VMEM((HEADS, DV, DK), F32),
            pltpu.VMEM((N_META, POOL_WIDTH), F32),
            pltpu.VMEM((up_rows, wup.shape[1]), F32),
            pltpu.VMEM((dn_rows, wdown.shape[1]), F32),
            pltpu.VMEM((up_rows, wup.shape[1]), BF16),
            pltpu.VMEM((dn_rows, wdown.shape[1]), BF16),
            pltpu.SemaphoreType.DMA((4,)),
        ],
        compiler_params=pltpu.CompilerParams(
            dimension_semantics=("arbitrary", "arbitrary"),
            vmem_limit_bytes=VMEM_LIMIT),
        name="prompt_mixer",
    )(x, meta, gmix, win, wpool, pscale, lbl, gon, wa, wb, wout, wup, wdown)


def _decode_mixer_kernel(x_ref, poolT_ref, s_ref, gmix_ref, win_ref, wpool_ref, pscale_ref,
                         lbl_ref, gon_ref, wa_ref, wb_ref, wout_ref,
                         x1_ref, unew_ref, snew_ref,
                         z_sc, o_sc, *, ts):
    i = pl.program_id(0)
    n = pl.num_programs(0)
    lb = _lower_bound(lbl_ref[...])

    @pl.when(i == 0)
    def _in_proj():
        h = _rms(x_ref[...], gmix_ref[...]).astype(BF16)
        z_sc[...] = _dot(h, win_ref[...])

    r = pl.ds(pl.multiple_of(i * ts, ts), ts)
    q, fg = _gates(z_sc[r, O_Q:O_Q + KTOT], z_sc[r, O_F:O_F + KTOT], lb)
    k = 1.0 - fg
    v = z_sc[r, O_V:O_V + HWIDTH]
    row = lax.broadcasted_iota(jnp.int32, (ts, 1), 0)
    for hd in range(HEADS):
        sl = slice(hd * DK, (hd + 1) * DK)
        fT = fg[:, sl].T
        kT = k[:, sl].T
        qf = (q[:, sl] * fg[:, sl]).astype(BF16)
        qk = jnp.sum(q[:, sl] * k[:, sl], axis=-1, keepdims=True)
        o = qk * v[:, sl]
        for s in range(ts):
            s_old = s_ref[s, hd]
            snew_ref[s, hd] = fT[:, s:s + 1] * s_old + kT[:, s:s + 1] * v[s:s + 1, sl]
            o_row = _dot(qf[s:s + 1, :], s_old.astype(BF16))
            o = o + jnp.where(row == s, o_row, 0.0)
        o_sc[r, sl] = o

    @pl.when(i == n - 1)
    def _out_proj():
        x = x_ref[...]
        u = z_sc[:, O_U:O_U + POOL_WIDTH]
        unew_ref[...] = u
        ya_parts = []
        for gi, w in enumerate(POOL_WINDOWS):
            cs = slice(gi * POOL_GROUP, (gi + 1) * POOL_GROUP)
            wsum = u[:, cs]
            for j in range(1, w):
                wsum = wsum + poolT_ref[POOL_BUF - j, :, cs]
            pooled = wsum / float(w) - u[:, cs]
            ya_parts.append(pooled.astype(BF16))
        ya_pre = _group_maps(ya_parts, wpool_ref) * pscale_ref[...]
        ya = _dot(ya_pre.astype(BF16), wa_ref[...])
        gon = gon_ref[...]
        o_parts = []
        for hd in range(HEADS):
            sl = slice(hd * DV, (hd + 1) * DV)
            og = z_sc[:, O_OG + hd * DV:O_OG + (hd + 1) * DV]
            o_parts.append(_rms(o_sc[:, sl], gon) * _silu(og))
        yb = _dot(jnp.concatenate(o_parts, axis=-1).astype(BF16), wb_ref[...])
        m = (_twice_sigmoid(z_sc[:, O_GA:O_GA + D_MODEL]) * ya
             + _twice_sigmoid(z_sc[:, O_GB:O_GB + D_MODEL]) * yb)
        x1_ref[...] = x + _dot(m.astype(BF16), wout_ref[...])


def _decode_mixer(x, poolT, state, gmix, win, wpool, pscale, lbl, gon, wa, wb, wout, *, ts):
    n = x.shape[0]
    kern = functools.partial(_decode_mixer_kernel, ts=ts)
    return pl.pallas_call(
        kern,
        grid=(n // ts,),
        in_specs=[
            _const_spec((n, D_MODEL)),
            _const_spec((POOL_BUF, n, POOL_WIDTH)),
            pl.BlockSpec((ts, HEADS, DK, DV), lambda i: (i, 0, 0, 0)),
            _const_spec((1, D_MODEL)),
            _const_spec((D_MODEL, N_IN)),
            _const_spec((len(POOL_WINDOWS) // 2, 2 * POOL_GROUP, 2 * POOL_GROUP)),
            _const_spec((1, POOL_WIDTH)),
            _const_spec((2, KTOT)),
            _const_spec((1, DV)),
            _const_spec((POOL_WIDTH, D_MODEL)),
            _const_spec((HWIDTH, D_MODEL)),
            _const_spec((D_MODEL, D_MODEL)),
        ],
        out_specs=[
            pl.BlockSpec((n, D_MODEL), lambda i: (0, 0)),
            pl.BlockSpec((n, POOL_WIDTH), lambda i: (0, 0)),
            pl.BlockSpec((ts, HEADS, DK, DV), lambda i: (i, 0, 0, 0)),
        ],
        out_shape=[
            jax.ShapeDtypeStruct((n, D_MODEL), F32),
            jax.ShapeDtypeStruct((n, POOL_WIDTH), F32),
            jax.ShapeDtypeStruct((n, HEADS, DK, DV), F32),
        ],
        scratch_shapes=[
            pltpu.VMEM((n, N_IN), F32),
            pltpu.VMEM((n, HWIDTH), F32),
        ],
        compiler_params=pltpu.CompilerParams(
            dimension_semantics=("arbitrary",),
            vmem_limit_bytes=VMEM_LIMIT),
        name="decode_mixer",
    )(x, poolT, state, gmix, win, wpool, pscale, lbl, gon, wa, wb, wout)


def _mlp_kernel(x_ref, gmlp_ref, wup_ref, wdown_ref, gfin_ref, y_ref, *, ff_chunk):
    x = x_ref[...]
    h = _rms(x, gmlp_ref[...]).astype(BF16)
    acc = x
    for c in range(D_FF // ff_chunk):
        cs = slice(c * ff_chunk, (c + 1) * ff_chunk)
        a = jnp.maximum(_dot(h, wup_ref[:, cs]), 0.0)
        acc = acc + _dot((a * a).astype(BF16), wdown_ref[cs, :])
    y_ref[...] = _rms(acc, gfin_ref[...])


def _mlp(x, gmlp, wup, wdown, gfin, *, tm, ff_chunk=1024):
    n = x.shape[0]
    kern = functools.partial(_mlp_kernel, ff_chunk=ff_chunk)
    return pl.pallas_call(
        kern,
        grid=(n // tm,),
        in_specs=[
            pl.BlockSpec((tm, D_MODEL), lambda i: (i, 0)),
            _const_spec((1, D_MODEL)),
            _const_spec((D_MODEL, D_FF)),
            _const_spec((D_FF, D_MODEL)),
            _const_spec((1, D_MODEL)),
        ],
        out_specs=pl.BlockSpec((tm, D_MODEL), lambda i: (i, 0)),
        out_shape=jax.ShapeDtypeStruct((n, D_MODEL), F32),
        compiler_params=pltpu.CompilerParams(
            dimension_semantics=("arbitrary",),
            vmem_limit_bytes=VMEM_LIMIT),
        name="channel_mlp",
    )(x, gmlp, wup, wdown, gfin)


def kernel(x_prompt, x_sample, state_pool, state_hgrn, meta_tokens, g_mix, w_in, w_pool, pool_scale,
           hgrn_lb_logits, g_onorm, w_a, w_b, w_out, g_mlp, w_up, w_down, g_final):
    B, L, _ = x_prompt.shape
    NS = x_sample.shape[0]
    assert g_mix.shape[0] == 1, "single-layer trunk"
    gmix = g_mix[0][None, :]
    col = jnp.arange(N_IN)
    gate_col = ((col >= O_Q) & (col < O_V)) | (col >= O_OG)
    win = (w_in[0] * jnp.where(gate_col, 0.5, 1.0)[None, :]).astype(BF16)
    wp = w_pool[0].astype(BF16).reshape(len(POOL_WINDOWS) // 2, 2, POOL_GROUP, POOL_GROUP)
    zp = jnp.zeros_like(wp[:, 0])
    wpool = jnp.concatenate([jnp.concatenate([wp[:, 0], zp], axis=2),
                             jnp.concatenate([zp, wp[:, 1]], axis=2)], axis=1)
    pscale = pool_scale[0][None, :]
    gon = g_onorm[0][None, :]
    wa = w_a[0].astype(BF16)
    wb = w_b[0].astype(BF16)
    wout = (0.5 * w_out[0]).astype(BF16)
    gmlp = g_mlp[0][None, :]
    gfin = g_final[None, :]

    x1_p, pool_p, hgrn_p, wup, wdown = _prompt_mixer(
        x_prompt, meta_tokens, gmix, win, wpool, pscale, hgrn_lb_logits, gon, wa, wb, wout,
        w_up[0], w_down[0], tile=512)
    y_p = _mlp(x1_p.reshape(B * L, D_MODEL), gmlp, wup, wdown, gfin, tm=512)

    xs = x_sample.reshape(NS, D_MODEL)
    poolT = jnp.swapaxes(state_pool[0], 0, 1)
    x1_s, u_s, hgrn_s = _decode_mixer(
        xs, poolT, state_hgrn[0], gmix, win, wpool, pscale, hgrn_lb_logits, gon, wa, wb, wout, ts=16)
    y_s = _mlp(x1_s, gmlp, wup, wdown, gfin, tm=NS)
    pool_s = jnp.concatenate([state_pool[0][:, 1:, :], u_s[:, None, :]], axis=1)

    return (y_p.reshape(B, L, D_MODEL), y_s.reshape(NS, 1, D_MODEL),
            pool_p[None], hgrn_p[None], pool_s[None], hgrn_s[None])
```

```python
import functools

import jax
import jax.numpy as jnp
from jax import lax
from jax.experimental import pallas as pl
from jax.experimental.pallas import tpu as pltpu

D_MODEL = 1024
N_META = 16
POOL_WIDTH = 512
POOL_WINDOWS = (2, 4, 8, 16)
POOL_GROUP = 128
POOL_MAXW = 16
POOL_BUF = 15
HEADS = 4
DK = 128
DV = 128
KTOT = 512
HWIDTH = 512
D_FF = 4096
EPS = 1e-6
N_IN = 4608
O_U, O_Q, O_F, O_V, O_OG, O_GA, O_GB = 0, 512, 1024, 1536, 2048, 2560, 3584

SUB = 16
CHUNK = 256
GATE_PIECE = 256
GATE_ROWS = 256
LEVEL_HALVES = tuple(1 << i for i in range(CHUNK.bit_length() - 1))
VMEM_LIMIT = 56 * 1024 * 1024

F32 = jnp.float32
BF16 = jnp.bfloat16


def _rms(x, g):
    return x * lax.rsqrt(jnp.mean(x * x, axis=-1, keepdims=True) + EPS) * g


def _twice_sigmoid(hx):
    return jnp.tanh(hx) + 1.0


def _silu(hx):
    return hx * jnp.tanh(hx) + hx


def _dot(a, b):
    return jnp.dot(a, b, preferred_element_type=F32)


def _dot_nt(a, b):
    return lax.dot_general(a, b, (((1,), (1,)), ((), ())), preferred_element_type=F32)


def _dot_tn(a, b):
    return lax.dot_general(a, b, (((0,), (0,)), ((), ())), preferred_element_type=F32)


def _group_maps(pooled_groups, wpool_ref):
    outs = []
    for p in range(len(pooled_groups) // 2):
        pair = jnp.concatenate(pooled_groups[2 * p:2 * p + 2], axis=-1)
        outs.append(_dot(pair, wpool_ref[p]))
    return jnp.concatenate(outs, axis=-1)


def _lower_bound(lb_logits):
    m = jnp.max(lb_logits, axis=0, keepdims=True)
    e = jnp.exp(lb_logits - m)
    return e[0:1, :] / jnp.sum(e, axis=0, keepdims=True)


def _gates(hq, hf, lb):
    q = _silu(hq)
    fg = 0.5 * (1.0 + lb) + (0.5 * (1.0 - lb)) * jnp.tanh(hf)
    return q, fg


def _cumsum_rows(g):
    n = g.shape[0]
    r = lax.broadcasted_iota(jnp.int32, (n, n), 0)
    c = lax.broadcasted_iota(jnp.int32, (n, n), 1)
    tril = (r >= c).astype(BF16)
    hi = g.astype(BF16)
    r1 = g - hi.astype(F32)
    mid = r1.astype(BF16)
    lo = (r1 - mid.astype(F32)).astype(BF16)
    return _dot(tril, hi) + _dot(tril, mid) + _dot(tril, lo)


def _level_map(n):
    t = lax.broadcasted_iota(jnp.int32, (n, n), 0)
    s = lax.broadcasted_iota(jnp.int32, (n, n), 1)
    x = t ^ s
    lvl = jnp.full((n, n), -1, jnp.int32)
    for li in range(n.bit_length() - 1):
        lvl = jnp.where((x >> li) == 1, li, lvl)
    return jnp.where(t > s, lvl, -1)


def _level_weights(m, r0, Gg, qg, kg, fgg, g_sc, row8, row16):
    if m == 1:
        return jnp.where((row16 & 1) != 0, qg * fgg, kg)
    if m >= SUB:
        blk = (r0 // (2 * m)) * (2 * m)
        ref = g_sc[blk + m - 1:blk + m, :]
        return (jnp.exp2(Gg - ref) * qg) if (r0 & m) else (jnp.exp2(ref - Gg) * kg)
    if m == SUB // 2:
        ref = g_sc[r0 + m - 1:r0 + m, :]
        return jnp.concatenate([jnp.exp2(ref - Gg[:m]) * kg[:m], jnp.exp2(Gg[m:] - ref) * qg[m:]], axis=0)
    halves = []
    for rb in (r0, r0 + 8):
        if m == 4:
            halves.append(jnp.broadcast_to(g_sc[rb + 3:rb + 4, :], (8, KTOT)))
        else:
            halves.append(jnp.where(row8 >= 4, g_sc[rb + 5:rb + 6, :], g_sc[rb + 1:rb + 2, :]))
    ref = jnp.concatenate(halves, axis=0)
    sel = jnp.where((row16 & m) != 0, qg, kg)
    return jnp.exp2(-jnp.abs(Gg - ref)) * sel


def _gates_stage(base, par, z_sc, lb, q_sc, k_sc, fg_sc, g_sc, od_sc, dec_sc):
    rs = slice(base, base + CHUNK)
    q, fg = _gates(z_sc[rs, O_Q:O_Q + KTOT], z_sc[rs, O_F:O_F + KTOT], lb)
    k = 1.0 - fg
    G = _cumsum_rows(jnp.log2(fg))
    q_sc[...] = q
    k_sc[...] = k
    fg_sc[...] = fg
    g_sc[...] = G
    v = z_sc[rs, O_V:O_V + HWIDTH]
    qk = q * k
    for hd in range(HEADS):
        sl = slice(hd * DK, (hd + 1) * DK)
        od_sc[par, :, sl] = jnp.sum(qk[:, sl], axis=-1, keepdims=True) * v[:, sl]
    dec_sc[par] = jnp.broadcast_to(jnp.exp2(G[CHUNK - 1:CHUNK, :]), (8, KTOT))


def _group_stage(j, par, q_sc, k_sc, fg_sc, g_sc, w_sc, qb_sc, kb_sc):
    row8 = lax.broadcasted_iota(jnp.int32, (8, 1), 0)
    row16 = lax.broadcasted_iota(jnp.int32, (SUB, 1), 0)
    r0 = j * SUB
    gs = slice(r0, r0 + SUB)
    Gg, qg, kg, fgg = g_sc[gs, :], q_sc[gs, :], k_sc[gs, :], fg_sc[gs, :]
    g_last = g_sc[CHUNK - 1:CHUNK, :]
    for li, m in enumerate(LEVEL_HALVES):
        w_sc[par, li, gs, :] = _level_weights(m, r0, Gg, qg, kg, fgg, g_sc, row8, row16).astype(BF16)
    qb_sc[par, gs, :] = (qg * jnp.exp2(Gg)).astype(BF16)
    kb_sc[par, gs, :] = (kg * jnp.exp2(g_last - Gg)).astype(BF16)


def _head_stage(base, hd, par, z_sc, lvl, w_sc, qb_sc, kb_sc, od_sc, dec_sc, o_sc, st_sc):
    rs = slice(base, base + CHUNK)
    sl = slice(hd * DK, (hd + 1) * DK)
    half = CHUNK // 2
    a_lo = jnp.zeros((half, half), F32)
    a_hi = jnp.zeros((half, half), F32)
    zero = jnp.zeros((half, DK), BF16)
    for li in range(len(LEVEL_HALVES) - 1):
        w_lo = w_sc[par, li, :half, sl]
        w_hi = w_sc[par, li, half:, sl]
        x = jnp.concatenate([w_lo, w_hi], axis=1)
        y = jnp.concatenate([jnp.concatenate([w_lo.T, zero], axis=1),
                             jnp.concatenate([zero, w_hi.T], axis=1)], axis=0)
        p = _dot(x, y)
        a_lo = jnp.where(lvl == li, p[:, :half], a_lo)
        a_hi = jnp.where(lvl == li, p[:, half:], a_hi)
    top = len(LEVEL_HALVES) - 1
    a_x = _dot_nt(w_sc[par, top, half:, sl], w_sc[par, top, :half, sl])
    A = jnp.concatenate([jnp.concatenate([a_lo, jnp.zeros((half, half), F32)], axis=1),
                         jnp.concatenate([a_x, a_hi], axis=1)], axis=0)
    v_bf = z_sc[rs, O_V + hd * DV:O_V + (hd + 1) * DV].astype(BF16)
    st = st_sc[hd]
    o_sc[rs, sl] = (_dot(A.astype(BF16), v_bf)
                    + _dot_nt(qb_sc[par, :, sl], st.astype(BF16))
                    + od_sc[par, :, sl])
    st_sc[hd] = st * dec_sc[par, 0:1, sl] + _dot_tn(v_bf, kb_sc[par, :, sl])


def _slice_copies(step, hbm_refs, vmem_refs, sems, to_hbm):
    copies = []
    for hbm, vmem, sem in zip(hbm_refs, vmem_refs, sems):
        rows = vmem.shape[0]
        window = hbm.at[pl.ds(pl.multiple_of(step * rows, rows), rows), :]
        copies.append(pltpu.make_async_copy(vmem, window, sem) if to_hbm
                      else pltpu.make_async_copy(window, vmem, sem))
    return copies


def _prompt_mixer_kernel(x_ref, meta_ref, gmix_ref, win_ref, wpool_ref, pscale_ref, lbl_ref,
                         gon_ref, wa_ref, wb_ref, wout_ref, wup_hbm, wdn_hbm,
                         x1_ref, pool_ref, hgrn_ref, wup16_hbm, wdn16_hbm,
                         z_sc, h_sc, ya_sc, ubuf, q_sc, k_sc, fg_sc, g_sc, od_sc, dec_sc,
                         w_sc, qb_sc, kb_sc, o_sc, st_sc, st_meta,
                         u_meta, up32, dn32, up16, dn16, cast_sems, *, tile):
    b = pl.program_id(0)
    t = pl.program_id(1)
    nt = pl.num_programs(1)
    lb = _lower_bound(lbl_ref[...])
    gmix = gmix_ref[...]

    step = b * nt + t
    last_step = pl.num_programs(0) * nt - 1
    in_sems, out_sems = (cast_sems.at[0], cast_sems.at[1]), (cast_sems.at[2], cast_sems.at[3])

    def cast_out(s):
        return _slice_copies(s, (wup16_hbm, wdn16_hbm), (up16, dn16), out_sems, to_hbm=True)

    cast_in = _slice_copies(step, (wup_hbm, wdn_hbm), (up32, dn32), in_sems, to_hbm=False)
    for cp in cast_in:
        cp.start()

    @pl.when((b == 0) & (t == 0))
    def _meta():
        hm = _rms(meta_ref[...], gmix).astype(BF16)
        zm = _dot(hm, win_ref[:, 0:O_OG])
        u_meta[...] = zm[:, O_U:O_U + POOL_WIDTH]
        _, fg = _gates(zm[:, O_Q:O_Q + KTOT], zm[:, O_F:O_F + KTOT], lb)
        G = _cumsum_rows(jnp.log2(fg))
        kt = (1.0 - fg) * jnp.exp2(G[N_META - 1:N_META, :] - G)
        v = zm[:, O_V:O_V + HWIDTH]
        for h in range(HEADS):
            sl = slice(h * DK, (h + 1) * DK)
            st_meta[h] = _dot_tn(v[:, sl].astype(BF16), kt[:, sl].astype(BF16))

    @pl.when(t == 0)
    def _init():
        st_sc[...] = st_meta[...]
        ubuf[0:POOL_MAXW, :] = u_meta[...]

    n_chunks = tile // CHUNK
    h_sc[...] = _rms(x_ref[0], gmix).astype(BF16)
    z_sc[:, 0:O_OG] = _dot(h_sc[...], win_ref[:, 0:O_OG])

    u = z_sc[:, O_U:O_U + POOL_WIDTH]
    ubuf[POOL_MAXW:POOL_MAXW + tile, :] = u
    ya_parts = []
    for gi, w in enumerate(POOL_WINDOWS):
        cs = slice(gi * POOL_GROUP, (gi + 1) * POOL_GROUP)
        wsum = ubuf[:, cs]
        span = 1
        while span < w:
            wsum = wsum + pltpu.roll(wsum, span, 0)
            span *= 2
        pooled = wsum[POOL_MAXW:, :] / float(w) - u[:, cs]
        ya_parts.append(pooled.astype(BF16))
    ya_pre = _group_maps(ya_parts, wpool_ref) * pscale_ref[...]
    ya_sc[...] = _dot(ya_pre.astype(BF16), wa_ref[...])
    ubuf[0:POOL_MAXW, :] = ubuf[tile:tile + POOL_MAXW, :]

    def gate_piece(r0, c0):
        z_sc[r0:r0 + GATE_ROWS, c0:c0 + GATE_PIECE] = _dot(
            h_sc[r0:r0 + GATE_ROWS, :], win_ref[:, c0:c0 + GATE_PIECE])
    pending = [functools.partial(gate_piece, r0, c0)
               for c0 in range(O_OG, N_IN, GATE_PIECE) for r0 in range(0, tile, GATE_ROWS)]
    n_pieces = len(pending)
    n_groups = CHUNK // SUB
    n_slots = n_chunks * (n_groups + HEADS)
    slot = [0]

    def interleave():
        slot[0] += 1
        while len(pending) > n_pieces - (-(-slot[0] * n_pieces // n_slots)):
            pending.pop(0)()

    def gates(c):
        _gates_stage(c * CHUNK, c % 2, z_sc, lb, q_sc, k_sc, fg_sc, g_sc, od_sc, dec_sc)

    def group(c, j):
        _group_stage(j, c % 2, q_sc, k_sc, fg_sc, g_sc, w_sc, qb_sc, kb_sc)
        interleave()

    def head(c, hd):
        _head_stage(c * CHUNK, hd, c % 2, z_sc, lvl, w_sc, qb_sc, kb_sc, od_sc, dec_sc, o_sc, st_sc)
        interleave()

    lvl = _level_map(CHUNK // 2)
    gates(0)
    for j in range(n_groups):
        group(0, j)
    for c in range(n_chunks):
        if c + 1 < n_chunks:
            gates(c + 1)
        for hd in range(HEADS):
            head(c, hd)
            if c + 1 < n_chunks:
                for j in range(hd * n_groups // HEADS, (hd + 1) * n_groups // HEADS):
                    group(c + 1, j)
    assert not pending and slot[0] == n_slots

    gon = gon_ref[...]
    o_parts = []
    for hd in range(HEADS):
        sl = slice(hd * DV, (hd + 1) * DV)
        og = z_sc[:, O_OG + hd * DV:O_OG + (hd + 1) * DV]
        o_parts.append(_rms(o_sc[:, sl], gon) * _silu(og))
    yb = _dot(jnp.concatenate(o_parts, axis=-1).astype(BF16), wb_ref[...])
    m = (_twice_sigmoid(z_sc[:, O_GA:O_GA + D_MODEL]) * ya_sc[...]
         + _twice_sigmoid(z_sc[:, O_GB:O_GB + D_MODEL]) * yb)
    x1_ref[0] = x_ref[0] + _dot(m.astype(BF16), wout_ref[...])

    for cp in cast_in:
        cp.wait()

    @pl.when(step > 0)
    def _drain_previous():
        for cp in cast_out(step - 1):
            cp.wait()

    up16[...] = up32[...].astype(BF16)
    dn16[...] = dn32[...].astype(BF16)
    for cp in cast_out(step):
        cp.start()

    @pl.when(step == last_step)
    def _drain_last():
        for cp in cast_out(step):
            cp.wait()

    @pl.when(t == nt - 1)
    def _state_out():
        pool_ref[0] = ubuf[1:POOL_MAXW, :]
        for hd in range(HEADS):
            hgrn_ref[0, hd] = st_sc[hd].T


def _const_spec(shape):
    nd = len(shape)
    return pl.BlockSpec(shape, lambda *_: (0,) * nd, pipeline_mode=pl.Buffered(1))


def _prompt_mixer(x, meta, gmix, win, wpool, pscale, lbl, gon, wa, wb, wout, wup, wdown, *, tile):
    B, L, _ = x.shape
    nt = L // tile
    n_steps = B * nt
    assert wup.shape[0] % (SUB * n_steps) == 0 and wdown.shape[0] % (SUB * n_steps) == 0
    up_rows, dn_rows = wup.shape[0] // n_steps, wdown.shape[0] // n_steps
    kern = functools.partial(_prompt_mixer_kernel, tile=tile)
    return pl.pallas_call(
        kern,
        grid=(B, nt),
        in_specs=[
            pl.BlockSpec((1, tile, D_MODEL), lambda b, t: (b, t, 0)),
            _const_spec((N_META, D_MODEL)),
            _const_spec((1, D_MODEL)),
            _const_spec((D_MODEL, N_IN)),
            _const_spec((len(POOL_WINDOWS) // 2, 2 * POOL_GROUP, 2 * POOL_GROUP)),
            _const_spec((1, POOL_WIDTH)),
            _const_spec((2, KTOT)),
            _const_spec((1, DV)),
            _const_spec((POOL_WIDTH, D_MODEL)),
            _const_spec((HWIDTH, D_MODEL)),
            _const_spec((D_MODEL, D_MODEL)),
            pl.BlockSpec(memory_space=pl.ANY),
            pl.BlockSpec(memory_space=pl.ANY),
        ],
        out_specs=[
            pl.BlockSpec((1, tile, D_MODEL), lambda b, t: (b, t, 0)),
            pl.BlockSpec((1, POOL_BUF, POOL_WIDTH), lambda b, t: (b, 0, 0)),
            pl.BlockSpec((1, HEADS, DK, DV), lambda b, t: (b, 0, 0, 0)),
            pl.BlockSpec(memory_space=pl.ANY),
            pl.BlockSpec(memory_space=pl.ANY),
        ],
        out_shape=[
            jax.ShapeDtypeStruct((B, L, D_MODEL), F32),
            jax.ShapeDtypeStruct((B, POOL_BUF, POOL_WIDTH), F32),
            jax.ShapeDtypeStruct((B, HEADS, DK, DV), F32),
            jax.ShapeDtypeStruct(wup.shape, BF16),
            jax.ShapeDtypeStruct(wdown.shape, BF16),
        ],
        scratch_shapes=[
            pltpu.VMEM((tile, N_IN), F32),
            pltpu.VMEM((tile, D_MODEL), BF16),
            pltpu.VMEM((tile, D_MODEL), F32),
            pltpu.VMEM((tile + POOL_MAXW, POOL_WIDTH), F32),
            pltpu.VMEM((CHUNK, KTOT), F32),
            pltpu.VMEM((CHUNK, KTOT), F32),
            pltpu.VMEM((CHUNK, KTOT), F32),
            pltpu.VMEM((CHUNK, KTOT), F32),
            pltpu.VMEM((2, CHUNK, HWIDTH), F32),
            pltpu.VMEM((2, 8, KTOT), F32),
            pltpu.VMEM((2, len(LEVEL_HALVES), CHUNK, KTOT), BF16),
            pltpu.VMEM((2, CHUNK, KTOT), BF16),
            pltpu.VMEM((2, CHUNK, KTOT), BF16),
            pltpu.VMEM((tile, HWIDTH), F32),
            pltpu.VMEM((HEADS, DV, DK), F32),
            pltpu.VMEM((HEADS, DV, DK), F32),
            pltpu.VMEM((N_META, POOL_WIDTH), F32),
            pltpu.VMEM((up_rows, wup.shape[1]), F32),
            pltpu.VMEM((dn_rows, wdown.shape[1]), F32),
            pltpu.VMEM((up_rows, wup.shape[1]), BF16),
            pltpu.VMEM((dn_rows, wdown.shape[1]), BF16),
            pltpu.SemaphoreType.DMA((4,)),
        ],
        compiler_params=pltpu.CompilerParams(
            dimension_semantics=("arbitrary", "arbitrary"),
            vmem_limit_bytes=VMEM_LIMIT),
        name="prompt_mixer",
    )(x, meta, gmix, win, wpool, pscale, lbl, gon, wa, wb, wout, wup, wdown)


def _decode_mixer_kernel(x_ref, pool_ref, s_ref, gmix_ref, win_ref, wpool_ref, pscale_ref,
                         lbl_ref, gon_ref, wa_ref, wb_ref, wout_ref,
                         x1_ref, pnew_ref, snew_ref,
                         z_sc, o_sc, *, ts):
    i = pl.program_id(0)
    n = pl.num_programs(0)
    lb = _lower_bound(lbl_ref[...])

    @pl.when(i == 0)
    def _in_proj():
        h = _rms(x_ref[...], gmix_ref[...]).astype(BF16)
        z_sc[...] = _dot(h, win_ref[...])

    r = pl.ds(pl.multiple_of(i * ts, ts), ts)
    q, fg = _gates(z_sc[r, O_Q:O_Q + KTOT], z_sc[r, O_F:O_F + KTOT], lb)
    k = 1.0 - fg
    v = z_sc[r, O_V:O_V + HWIDTH]
    row = lax.broadcasted_iota(jnp.int32, (ts, 1), 0)
    for hd in range(HEADS):
        sl = slice(hd * DK, (hd + 1) * DK)
        fT = fg[:, sl].T
        kT = k[:, sl].T
        qf = (q[:, sl] * fg[:, sl]).astype(BF16)
        qk = jnp.sum(q[:, sl] * k[:, sl], axis=-1, keepdims=True)
        o = qk * v[:, sl]
        for s in range(ts):
            s_old = s_ref[s, hd]
            snew_ref[s, hd] = fT[:, s:s + 1] * s_old + kT[:, s:s + 1] * v[s:s + 1, sl]
            o_row = _dot(qf[s:s + 1, :], s_old.astype(BF16))
            o = o + jnp.where(row == s, o_row, 0.0)
        o_sc[r, sl] = o

    @pl.when(i == n - 1)
    def _out_proj():
        x = x_ref[...]
        u = z_sc[:, O_U:O_U + POOL_WIDTH]
        pnew_ref[:, 0:POOL_BUF - 1, :] = pool_ref[:, 1:POOL_BUF, :]
        pnew_ref[:, POOL_BUF - 1, :] = u
        ya_parts = []
        for gi, w in enumerate(POOL_WINDOWS):
            cs = slice(gi * POOL_GROUP, (gi + 1) * POOL_GROUP)
            wsum = u[:, cs]
            for j in range(1, w):
                wsum = wsum + pool_ref[:, POOL_BUF - j, cs]
            pooled = wsum / float(w) - u[:, cs]
            ya_parts.append(pooled.astype(BF16))
        ya_pre = _group_maps(ya_parts, wpool_ref) * pscale_ref[...]
        ya = _dot(ya_pre.astype(BF16), wa_ref[...])
        gon = gon_ref[...]
        o_parts = []
        for hd in range(HEADS):
            sl = slice(hd * DV, (hd + 1) * DV)
            og = z_sc[:, O_OG + hd * DV:O_OG + (hd + 1) * DV]
            o_parts.append(_rms(o_sc[:, sl], gon) * _silu(og))
        yb = _dot(jnp.concatenate(o_parts, axis=-1).astype(BF16), wb_ref[...])
        m = (_twice_sigmoid(z_sc[:, O_GA:O_GA + D_MODEL]) * ya
             + _twice_sigmoid(z_sc[:, O_GB:O_GB + D_MODEL]) * yb)
        x1_ref[...] = x + _dot(m.astype(BF16), wout_ref[...])


def _decode_mixer(x, pool, state, gmix, win, wpool, pscale, lbl, gon, wa, wb, wout, *, ts):
    n = x.shape[0]
    kern = functools.partial(_decode_mixer_kernel, ts=ts)
    return pl.pallas_call(
        kern,
        grid=(n // ts,),
        in_specs=[
            _const_spec((n, D_MODEL)),
            _const_spec((n, POOL_BUF, POOL_WIDTH)),
            pl.BlockSpec((ts, HEADS, DK, DV), lambda i: (i, 0, 0, 0)),
            _const_spec((1, D_MODEL)),
            _const_spec((D_MODEL, N_IN)),
            _const_spec((len(POOL_WINDOWS) // 2, 2 * POOL_GROUP, 2 * POOL_GROUP)),
            _const_spec((1, POOL_WIDTH)),
            _const_spec((2, KTOT)),
            _const_spec((1, DV)),
            _const_spec((POOL_WIDTH, D_MODEL)),
            _const_spec((HWIDTH, D_MODEL)),
            _const_spec((D_MODEL, D_MODEL)),
        ],
        out_specs=[
            pl.BlockSpec((n, D_MODEL), lambda i: (0, 0)),
            pl.BlockSpec((n, POOL_BUF, POOL_WIDTH), lambda i: (0, 0, 0)),
            pl.BlockSpec((ts, HEADS, DK, DV), lambda i: (i, 0, 0, 0)),
        ],
        out_shape=[
            jax.ShapeDtypeStruct((n, D_MODEL), F32),
            jax.ShapeDtypeStruct((n, POOL_BUF, POOL_WIDTH), F32),
            jax.ShapeDtypeStruct((n, HEADS, DK, DV), F32),
        ],
        scratch_shapes=[
            pltpu.VMEM((n, N_IN), F32),
            pltpu.VMEM((n, HWIDTH), F32),
        ],
        compiler_params=pltpu.CompilerParams(
            dimension_semantics=("arbitrary",),
            vmem_limit_bytes=VMEM_LIMIT),
        name="decode_mixer",
    )(x, pool, state, gmix, win, wpool, pscale, lbl, gon, wa, wb, wout)


def _mlp_kernel(x_ref, xs_ref, gmlp_ref, wup_ref, wdown_ref, gfin_ref, y_ref, ys_ref, *, ff_chunk):
    i = pl.program_id(0)
    last = pl.num_programs(0) - 1

    def rows(src_ref, dst_ref):
        x = src_ref[...]
        h = _rms(x, gmlp_ref[...]).astype(BF16)
        acc = x
        for c in range(D_FF // ff_chunk):
            cs = slice(c * ff_chunk, (c + 1) * ff_chunk)
            a = jnp.maximum(_dot(h, wup_ref[:, cs]), 0.0)
            acc = acc + _dot((a * a).astype(BF16), wdown_ref[cs, :])
        dst_ref[...] = _rms(acc, gfin_ref[...])

    @pl.when(i < last)
    def _prompt_tile():
        rows(x_ref, y_ref)

    @pl.when(i == last)
    def _sample_rows():
        rows(xs_ref, ys_ref)


def _mlp(x, xs, gmlp, wup, wdown, gfin, *, tm, ff_chunk=1024):
    n, ns = x.shape[0], xs.shape[0]
    n_tiles = n // tm
    kern = functools.partial(_mlp_kernel, ff_chunk=ff_chunk)

    def tile(i):
        return jnp.minimum(i, n_tiles - 1), 0

    return pl.pallas_call(
        kern,
        grid=(n_tiles + 1,),
        in_specs=[
            pl.BlockSpec((tm, D_MODEL), tile),
            _const_spec((ns, D_MODEL)),
            _const_spec((1, D_MODEL)),
            _const_spec((D_MODEL, D_FF)),
            _const_spec((D_FF, D_MODEL)),
            _const_spec((1, D_MODEL)),
        ],
        out_specs=[
            pl.BlockSpec((tm, D_MODEL), tile),
            pl.BlockSpec((ns, D_MODEL), lambda i: (0, 0)),
        ],
        out_shape=[
            jax.ShapeDtypeStruct((n, D_MODEL), F32),
            jax.ShapeDtypeStruct((ns, D_MODEL), F32),
        ],
        compiler_params=pltpu.CompilerParams(
            dimension_semantics=("arbitrary",),
            vmem_limit_bytes=VMEM_LIMIT),
        name="channel_mlp",
    )(x, xs, gmlp, wup, wdown, gfin)


def kernel(x_prompt, x_sample, state_pool, state_hgrn, meta_tokens, g_mix, w_in, w_pool, pool_scale,
           hgrn_lb_logits, g_onorm, w_a, w_b, w_out, g_mlp, w_up, w_down, g_final):
    B, L, _ = x_prompt.shape
    NS = x_sample.shape[0]
    assert g_mix.shape[0] == 1, "single-layer trunk"
    gmix = g_mix[0][None, :]
    col = jnp.arange(N_IN)
    gate_col = ((col >= O_Q) & (col < O_V)) | (col >= O_OG)
    win = (w_in[0] * jnp.where(gate_col, 0.5, 1.0)[None, :]).astype(BF16)
    wp = w_pool[0].astype(BF16).reshape(len(POOL_WINDOWS) // 2, 2, POOL_GROUP, POOL_GROUP)
    zp = jnp.zeros_like(wp[:, 0])
    wpool = jnp.concatenate([jnp.concatenate([wp[:, 0], zp], axis=2),
                             jnp.concatenate([zp, wp[:, 1]], axis=2)], axis=1)
    pscale = pool_scale[0][None, :]
    gon = g_onorm[0][None, :]
    wa = w_a[0].astype(BF16)
    wb = w_b[0].astype(BF16)
    wout = (0.5 * w_out[0]).astype(BF16)
    gmlp = g_mlp[0][None, :]
    gfin = g_final[None, :]

    x1_p, pool_p, hgrn_p, wup, wdown = _prompt_mixer(
        x_prompt, meta_tokens, gmix, win, wpool, pscale, hgrn_lb_logits, gon, wa, wb, wout,
        w_up[0], w_down[0], tile=512)

    xs = x_sample.reshape(NS, D_MODEL)
    x1_s, pool_s, hgrn_s = _decode_mixer(
        xs, state_pool[0], state_hgrn[0], gmix, win, wpool, pscale, hgrn_lb_logits, gon, wa, wb, wout,
        ts=16)

    y_p, y_s = _mlp(x1_p.reshape(B * L, D_MODEL), x1_s, gmlp, wup, wdown, gfin, tm=512)

    return (y_p.reshape(B, L, D_MODEL), y_s.reshape(NS, 1, D_MODEL),
            pool_p[None], hgrn_p[None], pool_s[None], hgrn_s[None])
```

```python
import functools

import jax
import jax.numpy as jnp
from jax import lax
from jax.experimental import pallas as pl
from jax.experimental.pallas import tpu as pltpu

D_MODEL = 1024
N_META = 16
POOL_WIDTH = 512
POOL_WINDOWS = (2, 4, 8, 16)
POOL_GROUP = 128
POOL_MAXW = 16
POOL_BUF = 15
HEADS = 4
DK = 128
DV = 128
KTOT = 512
HWIDTH = 512
D_FF = 4096
EPS = 1e-6
N_IN = 4608
O_U, O_Q, O_F, O_V, O_OG, O_GA, O_GB = 0, 512, 1024, 1536, 2048, 2560, 3584

SUB = 16
CHUNK = 256
GATE_PIECE = 256
GATE_ROWS = 256
LEVEL_HALVES = tuple(1 << i for i in range(CHUNK.bit_length() - 1))
VMEM_LIMIT = 56 * 1024 * 1024

F32 = jnp.float32
BF16 = jnp.bfloat16


def _rms(x, g):
    return x * lax.rsqrt(jnp.mean(x * x, axis=-1, keepdims=True) + EPS) * g


def _twice_sigmoid(hx):
    return jnp.tanh(hx) + 1.0


def _silu(hx):
    return hx * jnp.tanh(hx) + hx


def _dot(a, b):
    return jnp.dot(a, b, preferred_element_type=F32)


def _dot_nt(a, b):
    return lax.dot_general(a, b, (((1,), (1,)), ((), ())), preferred_element_type=F32)


def _dot_tn(a, b):
    return lax.dot_general(a, b, (((0,), (0,)), ((), ())), preferred_element_type=F32)


def _group_maps(pooled_groups, wpool_ref):
    outs = []
    for p in range(len(pooled_groups) // 2):
        pair = jnp.concatenate(pooled_groups[2 * p:2 * p + 2], axis=-1)
        outs.append(_dot(pair, wpool_ref[p]))
    return jnp.concatenate(outs, axis=-1)


def _lower_bound(lb_logits):
    m = jnp.max(lb_logits, axis=0, keepdims=True)
    e = jnp.exp(lb_logits - m)
    return e[0:1, :] / jnp.sum(e, axis=0, keepdims=True)


def _gates(hq, hf, lb):
    q = _silu(hq)
    fg = 0.5 * (1.0 + lb) + (0.5 * (1.0 - lb)) * jnp.tanh(hf)
    return q, fg


def _cumsum_rows(g):
    n = g.shape[0]
    r = lax.broadcasted_iota(jnp.int32, (n, n), 0)
    c = lax.broadcasted_iota(jnp.int32, (n, n), 1)
    tril = (r >= c).astype(BF16)
    hi = g.astype(BF16)
    r1 = g - hi.astype(F32)
    mid = r1.astype(BF16)
    lo = (r1 - mid.astype(F32)).astype(BF16)
    return _dot(tril, hi) + _dot(tril, mid) + _dot(tril, lo)


def _level_map(n):
    t = lax.broadcasted_iota(jnp.int32, (n, n), 0)
    s = lax.broadcasted_iota(jnp.int32, (n, n), 1)
    x = t ^ s
    lvl = jnp.full((n, n), -1, jnp.int32)
    for li in range(n.bit_length() - 1):
        lvl = jnp.where((x >> li) == 1, li, lvl)
    return jnp.where(t > s, lvl, -1)


def _level_weights(m, r0, Gg, qg, kg, fgg, g_sc, row8, row16):
    if m == 1:
        return jnp.where((row16 & 1) != 0, qg * fgg, kg)
    if m >= SUB:
        blk = (r0 // (2 * m)) * (2 * m)
        ref = g_sc[blk + m - 1:blk + m, :]
        return (jnp.exp2(Gg - ref) * qg) if (r0 & m) else (jnp.exp2(ref - Gg) * kg)
    if m == SUB // 2:
        ref = g_sc[r0 + m - 1:r0 + m, :]
        return jnp.concatenate([jnp.exp2(ref - Gg[:m]) * kg[:m], jnp.exp2(Gg[m:] - ref) * qg[m:]], axis=0)
    halves = []
    for rb in (r0, r0 + 8):
        if m == 4:
            halves.append(jnp.broadcast_to(g_sc[rb + 3:rb + 4, :], (8, KTOT)))
        else:
            halves.append(jnp.where(row8 >= 4, g_sc[rb + 5:rb + 6, :], g_sc[rb + 1:rb + 2, :]))
    ref = jnp.concatenate(halves, axis=0)
    sel = jnp.where((row16 & m) != 0, qg, kg)
    return jnp.exp2(-jnp.abs(Gg - ref)) * sel


def _gates_stage(base, par, z_sc, lb, q_sc, k_sc, fg_sc, g_sc, od_sc, dec_sc):
    rs = slice(base, base + CHUNK)
    q, fg = _gates(z_sc[rs, O_Q:O_Q + KTOT], z_sc[rs, O_F:O_F + KTOT], lb)
    k = 1.0 - fg
    G = _cumsum_rows(jnp.log2(fg))
    q_sc[...] = q
    k_sc[...] = k
    fg_sc[...] = fg
    g_sc[...] = G
    v = z_sc[rs, O_V:O_V + HWIDTH]
    qk = q * k
    for hd in range(HEADS):
        sl = slice(hd * DK, (hd + 1) * DK)
        od_sc[par, :, sl] = jnp.sum(qk[:, sl], axis=-1, keepdims=True) * v[:, sl]
    dec_sc[par] = jnp.broadcast_to(jnp.exp2(G[CHUNK - 1:CHUNK, :]), (8, KTOT))


def _group_stage(j, par, q_sc, k_sc, fg_sc, g_sc, w_sc, qb_sc, kb_sc):
    row8 = lax.broadcasted_iota(jnp.int32, (8, 1), 0)
    row16 = lax.broadcasted_iota(jnp.int32, (SUB, 1), 0)
    r0 = j * SUB
    gs = slice(r0, r0 + SUB)
    Gg, qg, kg, fgg = g_sc[gs, :], q_sc[gs, :], k_sc[gs, :], fg_sc[gs, :]
    g_last = g_sc[CHUNK - 1:CHUNK, :]
    for li, m in enumerate(LEVEL_HALVES):
        w_sc[par, li, gs, :] = _level_weights(m, r0, Gg, qg, kg, fgg, g_sc, row8, row16).astype(BF16)
    qb_sc[par, gs, :] = (qg * jnp.exp2(Gg)).astype(BF16)
    kb_sc[par, gs, :] = (kg * jnp.exp2(g_last - Gg)).astype(BF16)


def _head_stage(base, hd, par, z_sc, lvl, w_sc, qb_sc, kb_sc, od_sc, dec_sc, o_sc, st_sc):
    rs = slice(base, base + CHUNK)
    sl = slice(hd * DK, (hd + 1) * DK)
    half = CHUNK // 2
    a_lo = jnp.zeros((half, half), F32)
    a_hi = jnp.zeros((half, half), F32)
    zero = jnp.zeros((half, DK), BF16)
    for li in range(len(LEVEL_HALVES) - 1):
        w_lo = w_sc[par, li, :half, sl]
        w_hi = w_sc[par, li, half:, sl]
        x = jnp.concatenate([w_lo, w_hi], axis=1)
        y = jnp.concatenate([jnp.concatenate([w_lo.T, zero], axis=1),
                             jnp.concatenate([zero, w_hi.T], axis=1)], axis=0)
        p = _dot(x, y)
        a_lo = jnp.where(lvl == li, p[:, :half], a_lo)
        a_hi = jnp.where(lvl == li, p[:, half:], a_hi)
    top = len(LEVEL_HALVES) - 1
    a_x = _dot_nt(w_sc[par, top, half:, sl], w_sc[par, top, :half, sl])
    A = jnp.concatenate([jnp.concatenate([a_lo, jnp.zeros((half, half), F32)], axis=1),
                         jnp.concatenate([a_x, a_hi], axis=1)], axis=0)
    v_bf = z_sc[rs, O_V + hd * DV:O_V + (hd + 1) * DV].astype(BF16)
    st = st_sc[hd]
    o_sc[rs, sl] = (_dot(A.astype(BF16), v_bf)
                    + _dot_nt(qb_sc[par, :, sl], st.astype(BF16))
                    + od_sc[par, :, sl])
    st_sc[hd] = st * dec_sc[par, 0:1, sl] + _dot_tn(v_bf, kb_sc[par, :, sl])


def _slice_copies(step, hbm_refs, vmem_refs, sems, to_hbm):
    copies = []
    for hbm, vmem, sem in zip(hbm_refs, vmem_refs, sems):
        rows = vmem.shape[0]
        window = hbm.at[pl.ds(pl.multiple_of(step * rows, rows), rows), :]
        copies.append(pltpu.make_async_copy(vmem, window, sem) if to_hbm
                      else pltpu.make_async_copy(window, vmem, sem))
    return copies


def _prompt_mixer_kernel(x_ref, meta_ref, gmix_ref, win_ref, wpool_ref, pscale_ref, lbl_ref,
                         gon_ref, wa_ref, wb_ref, wout_ref, wup_hbm, wdn_hbm,
                         x1_ref, pool_ref, hgrn_ref, wup16_hbm, wdn16_hbm,
                         z_sc, h_sc, ya_sc, ubuf, q_sc, k_sc, fg_sc, g_sc, od_sc, dec_sc,
                         w_sc, qb_sc, kb_sc, o_sc, st_sc, st_meta,
                         u_meta, up32, dn32, up16, dn16, cast_sems, *, tile):
    b = pl.program_id(0)
    t = pl.program_id(1)
    nt = pl.num_programs(1)
    lb = _lower_bound(lbl_ref[...])
    gmix = gmix_ref[...]

    step = b * nt + t
    last_step = pl.num_programs(0) * nt - 1
    in_sems, out_sems = (cast_sems.at[0], cast_sems.at[1]), (cast_sems.at[2], cast_sems.at[3])

    def cast_out(s):
        return _slice_copies(s, (wup16_hbm, wdn16_hbm), (up16, dn16), out_sems, to_hbm=True)

    cast_in = _slice_copies(step, (wup_hbm, wdn_hbm), (up32, dn32), in_sems, to_hbm=False)
    for cp in cast_in:
        cp.start()

    @pl.when((b == 0) & (t == 0))
    def _meta():
        hm = _rms(meta_ref[...], gmix).astype(BF16)
        zm = _dot(hm, win_ref[:, 0:O_OG])
        u_meta[...] = zm[:, O_U:O_U + POOL_WIDTH]
        _, fg = _gates(zm[:, O_Q:O_Q + KTOT], zm[:, O_F:O_F + KTOT], lb)
        G = _cumsum_rows(jnp.log2(fg))
        kt = (1.0 - fg) * jnp.exp2(G[N_META - 1:N_META, :] - G)
        v = zm[:, O_V:O_V + HWIDTH]
        for h in range(HEADS):
            sl = slice(h * DK, (h + 1) * DK)
            st_meta[h] = _dot_tn(v[:, sl].astype(BF16), kt[:, sl].astype(BF16))

    @pl.when(t == 0)
    def _init():
        st_sc[...] = st_meta[...]
        ubuf[0:POOL_MAXW, :] = u_meta[...]

    n_chunks = tile // CHUNK
    h_sc[...] = _rms(x_ref[0], gmix).astype(BF16)
    z_sc[:, 0:O_OG] = _dot(h_sc[...], win_ref[:, 0:O_OG])

    u = z_sc[:, O_U:O_U + POOL_WIDTH]
    ubuf[POOL_MAXW:POOL_MAXW + tile, :] = u
    ya_parts = []
    for gi, w in enumerate(POOL_WINDOWS):
        cs = slice(gi * POOL_GROUP, (gi + 1) * POOL_GROUP)
        wsum = ubuf[:, cs]
        span = 1
        while span < w:
            wsum = wsum + pltpu.roll(wsum, span, 0)
            span *= 2
        pooled = wsum[POOL_MAXW:, :] / float(w) - u[:, cs]
        ya_parts.append(pooled.astype(BF16))
    ya_pre = _group_maps(ya_parts, wpool_ref) * pscale_ref[...]
    ya_sc[...] = _dot(ya_pre.astype(BF16), wa_ref[...])
    ubuf[0:POOL_MAXW, :] = ubuf[tile:tile + POOL_MAXW, :]

    def gate_piece(r0, c0):
        z_sc[r0:r0 + GATE_ROWS, c0:c0 + GATE_PIECE] = _dot(
            h_sc[r0:r0 + GATE_ROWS, :], win_ref[:, c0:c0 + GATE_PIECE])
    pending = [functools.partial(gate_piece, r0, c0)
               for c0 in range(O_OG, N_IN, GATE_PIECE) for r0 in range(0, tile, GATE_ROWS)]
    n_pieces = len(pending)
    n_groups = CHUNK // SUB
    n_slots = n_chunks * (n_groups + HEADS)
    slot = [0]

    def interleave():
        slot[0] += 1
        while len(pending) > n_pieces - (-(-slot[0] * n_pieces // n_slots)):
            pending.pop(0)()

    def gates(c):
        _gates_stage(c * CHUNK, c % 2, z_sc, lb, q_sc, k_sc, fg_sc, g_sc, od_sc, dec_sc)

    def group(c, j):
        _group_stage(j, c % 2, q_sc, k_sc, fg_sc, g_sc, w_sc, qb_sc, kb_sc)
        interleave()

    def head(c, hd):
        _head_stage(c * CHUNK, hd, c % 2, z_sc, lvl, w_sc, qb_sc, kb_sc, od_sc, dec_sc, o_sc, st_sc)
        interleave()

    lvl = _level_map(CHUNK // 2)
    gates(0)
    for j in range(n_groups):
        group(0, j)
    for c in range(n_chunks):
        if c + 1 < n_chunks:
            gates(c + 1)
        for hd in range(HEADS):
            head(c, hd)
            if c + 1 < n_chunks:
                for j in range(hd * n_groups // HEADS, (hd + 1) * n_groups // HEADS):
                    group(c + 1, j)
    assert not pending and slot[0] == n_slots

    gon = gon_ref[...]
    o_parts = []
    for hd in range(HEADS):
        sl = slice(hd * DV, (hd + 1) * DV)
        og = z_sc[:, O_OG + hd * DV:O_OG + (hd + 1) * DV]
        o_parts.append(_rms(o_sc[:, sl], gon) * _silu(og))
    yb = _dot(jnp.concatenate(o_parts, axis=-1).astype(BF16), wb_ref[...])
    m = (_twice_sigmoid(z_sc[:, O_GA:O_GA + D_MODEL]) * ya_sc[...]
         + _twice_sigmoid(z_sc[:, O_GB:O_GB + D_MODEL]) * yb)
    x1_ref[0] = x_ref[0] + _dot(m.astype(BF16), wout_ref[...])

    for cp in cast_in:
        cp.wait()

    @pl.when(step > 0)
    def _drain_previous():
        for cp in cast_out(step - 1):
            cp.wait()

    up16[...] = up32[...].astype(BF16)
    dn16[...] = dn32[...].astype(BF16)
    for cp in cast_out(step):
        cp.start()

    @pl.when(step == last_step)
    def _drain_last():
        for cp in cast_out(step):
            cp.wait()

    @pl.when(t == nt - 1)
    def _state_out():
        pool_ref[0] = ubuf[1:POOL_MAXW, :]
        for hd in range(HEADS):
            hgrn_ref[0, hd] = st_sc[hd].T


def _const_spec(shape):
    nd = len(shape)
    return pl.BlockSpec(shape, lambda *_: (0,) * nd, pipeline_mode=pl.Buffered(1))


def _prompt_mixer(x, meta, gmix, win, wpool, pscale, lbl, gon, wa, wb, wout, wup, wdown, *, tile):
    B, L, _ = x.shape
    nt = L // tile
    n_steps = B * nt
    assert wup.shape[0] % (SUB * n_steps) == 0 and wdown.shape[0] % (SUB * n_steps) == 0
    up_rows, dn_rows = wup.shape[0] // n_steps, wdown.shape[0] // n_steps
    kern = functools.partial(_prompt_mixer_kernel, tile=tile)
    return pl.pallas_call(
        kern,
        grid=(B, nt),
        in_specs=[
            pl.BlockSpec((1, tile, D_MODEL), lambda b, t: (b, t, 0)),
            _const_spec((N_META, D_MODEL)),
            _const_spec((1, D_MODEL)),
            _const_spec((D_MODEL, N_IN)),
            _const_spec((len(POOL_WINDOWS) // 2, 2 * POOL_GROUP, 2 * POOL_GROUP)),
            _const_spec((1, POOL_WIDTH)),
            _const_spec((2, KTOT)),
            _const_spec((1, DV)),
            _const_spec((POOL_WIDTH, D_MODEL)),
            _const_spec((HWIDTH, D_MODEL)),
            _const_spec((D_MODEL, D_MODEL)),
            pl.BlockSpec(memory_space=pl.ANY),
            pl.BlockSpec(memory_space=pl.ANY),
        ],
        out_specs=[
            pl.BlockSpec((1, tile, D_MODEL), lambda b, t: (b, t, 0)),
            pl.BlockSpec((1, POOL_BUF, POOL_WIDTH), lambda b, t: (b, 0, 0)),
            pl.BlockSpec((1, HEADS, DK, DV), lambda b, t: (b, 0, 0, 0)),
            pl.BlockSpec(memory_space=pl.ANY),
            pl.BlockSpec(memory_space=pl.ANY),
        ],
        out_shape=[
            jax.ShapeDtypeStruct((B, L, D_MODEL), F32),
            jax.ShapeDtypeStruct((B, POOL_BUF, POOL_WIDTH), F32),
            jax.ShapeDtypeStruct((B, HEADS, DK, DV), F32),
            jax.ShapeDtypeStruct(wup.shape, BF16),
            jax.ShapeDtypeStruct(wdown.shape, BF16),
        ],
        scratch_shapes=[
            pltpu.VMEM((tile, N_IN), F32),
            pltpu.VMEM((tile, D_MODEL), BF16),
            pltpu.VMEM((tile, D_MODEL), F32),
            pltpu.VMEM((tile + POOL_MAXW, POOL_WIDTH), F32),
            pltpu.VMEM((CHUNK, KTOT), F32),
            pltpu.VMEM((CHUNK, KTOT), F32),
            pltpu.VMEM((CHUNK, KTOT), F32),
            pltpu.VMEM((CHUNK, KTOT), F32),
            pltpu.VMEM((2, CHUNK, HWIDTH), F32),
            pltpu.VMEM((2, 8, KTOT), F32),
            pltpu.VMEM((2, len(LEVEL_HALVES), CHUNK, KTOT), BF16),
            pltpu.VMEM((2, CHUNK, KTOT), BF16),
            pltpu.VMEM((2, CHUNK, KTOT), BF16),
            pltpu.VMEM((tile, HWIDTH), F32),
            pltpu.VMEM((HEADS, DV, DK), F32),
            pltpu.VMEM((HEADS, DV, DK), F32),
            pltpu.VMEM((N_META, POOL_WIDTH), F32),
            pltpu.VMEM((up_rows, wup.shape[1]), F32),
            pltpu.VMEM((dn_rows, wdown.shape[1]), F32),
            pltpu.VMEM((up_rows, wup.shape[1]), BF16),
            pltpu.VMEM((dn_rows, wdown.shape[1]), BF16),
            pltpu.SemaphoreType.DMA((4,)),
        ],
        compiler_params=pltpu.CompilerParams(
            dimension_semantics=("arbitrary", "arbitrary"),
            vmem_limit_bytes=VMEM_LIMIT),
        name="prompt_mixer",
    )(x, meta, gmix, win, wpool, pscale, lbl, gon, wa, wb, wout, wup, wdown)


def _decode_mixer_kernel(x_ref, poolT_ref, s_ref, gmix_ref, win_ref, wpool_ref, pscale_ref,
                         lbl_ref, gon_ref, wa_ref, wb_ref, wout_ref,
                         x1_ref, unew_ref, snew_ref,
                         z_sc, o_sc, *, ts):
    i = pl.program_id(0)
    n = pl.num_programs(0)
    lb = _lower_bound(lbl_ref[...])

    @pl.when(i == 0)
    def _in_proj():
        h = _rms(x_ref[...], gmix_ref[...]).astype(BF16)
        z_sc[...] = _dot(h, win_ref[...])

    r = pl.ds(pl.multiple_of(i * ts, ts), ts)
    q, fg = _gates(z_sc[r, O_Q:O_Q + KTOT], z_sc[r, O_F:O_F + KTOT], lb)
    k = 1.0 - fg
    v = z_sc[r, O_V:O_V + HWIDTH]
    row = lax.broadcasted_iota(jnp.int32, (ts, 1), 0)
    for hd in range(HEADS):
        sl = slice(hd * DK, (hd + 1) * DK)
        fT = fg[:, sl].T
        kT = k[:, sl].T
        qf = (q[:, sl] * fg[:, sl]).astype(BF16)
        qk = jnp.sum(q[:, sl] * k[:, sl], axis=-1, keepdims=True)
        o = qk * v[:, sl]
        for s in range(ts):
            s_old = s_ref[s, hd]
            snew_ref[s, hd] = fT[:, s:s + 1] * s_old + kT[:, s:s + 1] * v[s:s + 1, sl]
            o_row = _dot(qf[s:s + 1, :], s_old.astype(BF16))
            o = o + jnp.where(row == s, o_row, 0.0)
        o_sc[r, sl] = o

    @pl.when(i == n - 1)
    def _out_proj():
        x = x_ref[...]
        u = z_sc[:, O_U:O_U + POOL_WIDTH]
        unew_ref[...] = u
        ya_parts = []
        for gi, w in enumerate(POOL_WINDOWS):
            cs = slice(gi * POOL_GROUP, (gi + 1) * POOL_GROUP)
            wsum = u[:, cs]
            for j in range(1, w):
                wsum = wsum + poolT_ref[POOL_BUF - j, :, cs]
            pooled = wsum / float(w) - u[:, cs]
            ya_parts.append(pooled.astype(BF16))
        ya_pre = _group_maps(ya_parts, wpool_ref) * pscale_ref[...]
        ya = _dot(ya_pre.astype(BF16), wa_ref[...])
        gon = gon_ref[...]
        o_parts = []
        for hd in range(HEADS):
            sl = slice(hd * DV, (hd + 1) * DV)
            og = z_sc[:, O_OG + hd * DV:O_OG + (hd + 1) * DV]
            o_parts.append(_rms(o_sc[:, sl], gon) * _silu(og))
        yb = _dot(jnp.concatenate(o_parts, axis=-1).astype(BF16), wb_ref[...])
        m = (_twice_sigmoid(z_sc[:, O_GA:O_GA + D_MODEL]) * ya
             + _twice_sigmoid(z_sc[:, O_GB:O_GB + D_MODEL]) * yb)
        x1_ref[...] = x + _dot(m.astype(BF16), wout_ref[...])


def _decode_mixer(x, poolT, state, gmix, win, wpool, pscale, lbl, gon, wa, wb, wout, *, ts):
    n = x.shape[0]
    kern = functools.partial(_decode_mixer_kernel, ts=ts)
    return pl.pallas_call(
        kern,
        grid=(n // ts,),
        in_specs=[
            _const_spec((n, D_MODEL)),
            _const_spec((POOL_BUF, n, POOL_WIDTH)),
            pl.BlockSpec((ts, HEADS, DK, DV), lambda i: (i, 0, 0, 0)),
            _const_spec((1, D_MODEL)),
            _const_spec((D_MODEL, N_IN)),
            _const_spec((len(POOL_WINDOWS) // 2, 2 * POOL_GROUP, 2 * POOL_GROUP)),
            _const_spec((1, POOL_WIDTH)),
            _const_spec((2, KTOT)),
            _const_spec((1, DV)),
            _const_spec((POOL_WIDTH, D_MODEL)),
            _const_spec((HWIDTH, D_MODEL)),
            _const_spec((D_MODEL, D_MODEL)),
        ],
        out_specs=[
            pl.BlockSpec((n, D_MODEL), lambda i: (0, 0)),
            pl.BlockSpec((n, POOL_WIDTH), lambda i: (0, 0)),
            pl.BlockSpec((ts, HEADS, DK, DV), lambda i: (i, 0, 0, 0)),
        ],
        out_shape=[
            jax.ShapeDtypeStruct((n, D_MODEL), F32),
            jax.ShapeDtypeStruct((n, POOL_WIDTH), F32),
            jax.ShapeDtypeStruct((n, HEADS, DK, DV), F32),
        ],
        scratch_shapes=[
            pltpu.VMEM((n, N_IN), F32),
            pltpu.VMEM((n, HWIDTH), F32),
        ],
        compiler_params=pltpu.CompilerParams(
            dimension_semantics=("arbitrary",),
            vmem_limit_bytes=VMEM_LIMIT),
        name="decode_mixer",
    )(x, poolT, state, gmix, win, wpool, pscale, lbl, gon, wa, wb, wout)


def _mlp_kernel(x_ref, xs_ref, gmlp_ref, wup_ref, wdown_ref, gfin_ref, y_ref, ys_ref, *, ff_chunk):
    i = pl.program_id(0)
    last = pl.num_programs(0) - 1

    def rows(src_ref, dst_ref):
        x = src_ref[...]
        h = _rms(x, gmlp_ref[...]).astype(BF16)
        acc = x
        for c in range(D_FF // ff_chunk):
            cs = slice(c * ff_chunk, (c + 1) * ff_chunk)
            a = jnp.maximum(_dot(h, wup_ref[:, cs]), 0.0)
            acc = acc + _dot((a * a).astype(BF16), wdown_ref[cs, :])
        dst_ref[...] = _rms(acc, gfin_ref[...])

    @pl.when(i < last)
    def _prompt_tile():
        rows(x_ref, y_ref)

    @pl.when(i == last)
    def _sample_rows():
        rows(xs_ref, ys_ref)


def _mlp(x, xs, gmlp, wup, wdown, gfin, *, tm, ff_chunk=1024):
    n, ns = x.shape[0], xs.shape[0]
    n_tiles = n // tm
    kern = functools.partial(_mlp_kernel, ff_chunk=ff_chunk)

    def tile(i):
        return jnp.minimum(i, n_tiles - 1), 0

    return pl.pallas_call(
        kern,
        grid=(n_tiles + 1,),
        in_specs=[
            pl.BlockSpec((tm, D_MODEL), tile),
            _const_spec((ns, D_MODEL)),
            _const_spec((1, D_MODEL)),
            _const_spec((D_MODEL, D_FF)),
            _const_spec((D_FF, D_MODEL)),
            _const_spec((1, D_MODEL)),
        ],
        out_specs=[
            pl.BlockSpec((tm, D_MODEL), tile),
            pl.BlockSpec((ns, D_MODEL), lambda i: (0, 0)),
        ],
        out_shape=[
            jax.ShapeDtypeStruct((n, D_MODEL), F32),
            jax.ShapeDtypeStruct((ns, D_MODEL), F32),
        ],
        compiler_params=pltpu.CompilerParams(
            dimension_semantics=("arbitrary",),
            vmem_limit_bytes=VMEM_LIMIT),
        name="channel_mlp",
    )(x, xs, gmlp, wup, wdown, gfin)


def kernel(x_prompt, x_sample, state_pool, state_hgrn, meta_tokens, g_mix, w_in, w_pool, pool_scale,
           hgrn_lb_logits, g_onorm, w_a, w_b, w_out, g_mlp, w_up, w_down, g_final):
    B, L, _ = x_prompt.shape
    NS = x_sample.shape[0]
    assert g_mix.shape[0] == 1, "single-layer trunk"
    gmix = g_mix[0][None, :]
    col = jnp.arange(N_IN)
    gate_col = ((col >= O_Q) & (col < O_V)) | (col >= O_OG)
    win = (w_in[0] * jnp.where(gate_col, 0.5, 1.0)[None, :]).astype(BF16)
    wp = w_pool[0].astype(BF16).reshape(len(POOL_WINDOWS) // 2, 2, POOL_GROUP, POOL_GROUP)
    zp = jnp.zeros_like(wp[:, 0])
    wpool = jnp.concatenate([jnp.concatenate([wp[:, 0], zp], axis=2),
                             jnp.concatenate([zp, wp[:, 1]], axis=2)], axis=1)
    pscale = pool_scale[0][None, :]
    gon = g_onorm[0][None, :]
    wa = w_a[0].astype(BF16)
    wb = w_b[0].astype(BF16)
    wout = (0.5 * w_out[0]).astype(BF16)
    gmlp = g_mlp[0][None, :]
    gfin = g_final[None, :]

    x1_p, pool_p, hgrn_p, wup, wdown = _prompt_mixer(
        x_prompt, meta_tokens, gmix, win, wpool, pscale, hgrn_lb_logits, gon, wa, wb, wout,
        w_up[0], w_down[0], tile=512)

    xs = x_sample.reshape(NS, D_MODEL)
    poolT = jnp.swapaxes(state_pool[0], 0, 1)
    x1_s, u_s, hgrn_s = _decode_mixer(
        xs, poolT, state_hgrn[0], gmix, win, wpool, pscale, hgrn_lb_logits, gon, wa, wb, wout, ts=16)

    y_p, y_s = _mlp(x1_p.reshape(B * L, D_MODEL), x1_s, gmlp, wup, wdown, gfin, tm=512)
    pool_s = jnp.concatenate([state_pool[0][:, 1:, :], u_s[:, None, :]], axis=1)

    return (y_p.reshape(B, L, D_MODEL), y_s.reshape(NS, 1, D_MODEL),
            pool_p[None], hgrn_p[None], pool_s[None], hgrn_s[None])
```

```python
import functools

import jax
import jax.numpy as jnp
from jax import lax
from jax.experimental import pallas as pl
from jax.experimental.pallas import tpu as pltpu

D_MODEL = 1024
N_META = 16
POOL_WIDTH = 512
POOL_WINDOWS = (2, 4, 8, 16)
POOL_GROUP = 128
POOL_MAXW = 16
POOL_BUF = 15
HEADS = 4
DK = 128
DV = 128
KTOT = 512
HWIDTH = 512
D_FF = 4096
EPS = 1e-6
N_IN = 4608
O_U, O_Q, O_F, O_V, O_OG, O_GA, O_GB = 0, 512, 1024, 1536, 2048, 2560, 3584

SUB = 16
CHUNK = 256
GATE_PIECE = 256
GATE_ROWS = 256
STAGE_ROWS = 256
LEVEL_HALVES = tuple(1 << i for i in range(CHUNK.bit_length() - 1))

V7X_VMEM_BYTES = 64 * 1024 * 1024
VMEM_LIMIT = V7X_VMEM_BYTES - 8 * 1024 * 1024
MIXER_TILE = 2 * CHUNK
MLP_TILE = 512
DECODE_TILE = SUB

F32 = jnp.float32
BF16 = jnp.bfloat16
F32_TINY = 1.1754944e-38


def _rms(x, g):
    return x * lax.rsqrt(jnp.mean(x * x, axis=-1, keepdims=True) + EPS) * g


def _twice_sigmoid(hx):
    return jnp.tanh(hx) + 1.0


def _silu(hx):
    return hx * jnp.tanh(hx) + hx


def _dot(a, b):
    return jnp.dot(a, b, preferred_element_type=F32)


def _dot_nt(a, b):
    return lax.dot_general(a, b, (((1,), (1,)), ((), ())), preferred_element_type=F32)


def _dot_tn(a, b):
    return lax.dot_general(a, b, (((0,), (0,)), ((), ())), preferred_element_type=F32)


def _group_maps(pooled_groups, wpool_ref):
    outs = []
    for p in range(len(pooled_groups) // 2):
        pair = jnp.concatenate(pooled_groups[2 * p:2 * p + 2], axis=-1)
        outs.append(_dot(pair, wpool_ref[p]))
    return jnp.concatenate(outs, axis=-1)


def _lower_bound(lb_logits):
    m = jnp.max(lb_logits, axis=0, keepdims=True)
    e = jnp.exp(lb_logits - m)
    return e[0:1, :] / jnp.sum(e, axis=0, keepdims=True)


def _gates(hq, hf, lb):
    q = _silu(hq)
    fg = 0.5 * (1.0 + lb) + (0.5 * (1.0 - lb)) * jnp.tanh(hf)
    return q, jnp.maximum(fg, F32_TINY)


def _cumsum_rows(g):
    n = g.shape[0]
    r = lax.broadcasted_iota(jnp.int32, (n, n), 0)
    c = lax.broadcasted_iota(jnp.int32, (n, n), 1)
    tril = (r >= c).astype(BF16)
    hi = g.astype(BF16)
    r1 = g - hi.astype(F32)
    mid = r1.astype(BF16)
    lo = (r1 - mid.astype(F32)).astype(BF16)
    return _dot(tril, hi) + _dot(tril, mid) + _dot(tril, lo)


def _level_map(n):
    t = lax.broadcasted_iota(jnp.int32, (n, n), 0)
    s = lax.broadcasted_iota(jnp.int32, (n, n), 1)
    x = t ^ s
    lvl = jnp.full((n, n), -1, jnp.int32)
    for li in range(n.bit_length() - 1):
        lvl = jnp.where((x >> li) == 1, li, lvl)
    return jnp.where(t > s, lvl, -1)


def _level_weights(m, r0, Gg, qg, kg, fgg, g_sc, row8, row16):
    if m == 1:
        return jnp.where((row16 & 1) != 0, qg * fgg, kg)
    if m >= SUB:
        blk = (r0 // (2 * m)) * (2 * m)
        ref = g_sc[blk + m - 1:blk + m, :]
        return (jnp.exp2(Gg - ref) * qg) if (r0 & m) else (jnp.exp2(ref - Gg) * kg)
    if m == SUB // 2:
        ref = g_sc[r0 + m - 1:r0 + m, :]
        return jnp.concatenate([jnp.exp2(ref - Gg[:m]) * kg[:m], jnp.exp2(Gg[m:] - ref) * qg[m:]], axis=0)
    halves = []
    for rb in (r0, r0 + 8):
        if m == 4:
            halves.append(jnp.broadcast_to(g_sc[rb + 3:rb + 4, :], (8, KTOT)))
        else:
            halves.append(jnp.where(row8 >= 4, g_sc[rb + 5:rb + 6, :], g_sc[rb + 1:rb + 2, :]))
    ref = jnp.concatenate(halves, axis=0)
    sel = jnp.where((row16 & m) != 0, qg, kg)
    return jnp.exp2(-jnp.abs(Gg - ref)) * sel


def _gates_stage(base, par, z_sc, lb, q_sc, k_sc, fg_sc, g_sc, od_sc, dec_sc):
    rs = slice(base, base + CHUNK)
    q, fg = _gates(z_sc[rs, O_Q:O_Q + KTOT], z_sc[rs, O_F:O_F + KTOT], lb)
    k = 1.0 - fg
    G = _cumsum_rows(jnp.log2(fg))
    q_sc[...] = q
    k_sc[...] = k
    fg_sc[...] = fg
    g_sc[...] = G
    v = z_sc[rs, O_V:O_V + HWIDTH]
    qk = q * k
    for hd in range(HEADS):
        sl = slice(hd * DK, (hd + 1) * DK)
        od_sc[par, :, sl] = jnp.sum(qk[:, sl], axis=-1, keepdims=True) * v[:, sl]
    dec_sc[par] = jnp.broadcast_to(jnp.exp2(G[CHUNK - 1:CHUNK, :]), (8, KTOT))


def _group_stage(j, par, q_sc, k_sc, fg_sc, g_sc, w_sc, qb_sc, kb_sc):
    row8 = lax.broadcasted_iota(jnp.int32, (8, 1), 0)
    row16 = lax.broadcasted_iota(jnp.int32, (SUB, 1), 0)
    r0 = j * SUB
    gs = slice(r0, r0 + SUB)
    Gg, qg, kg, fgg = g_sc[gs, :], q_sc[gs, :], k_sc[gs, :], fg_sc[gs, :]
    g_last = g_sc[CHUNK - 1:CHUNK, :]
    for li, m in enumerate(LEVEL_HALVES):
        w_sc[par, li, gs, :] = _level_weights(m, r0, Gg, qg, kg, fgg, g_sc, row8, row16).astype(BF16)
    qb_sc[par, gs, :] = (qg * jnp.exp2(Gg)).astype(BF16)
    kb_sc[par, gs, :] = (kg * jnp.exp2(g_last - Gg)).astype(BF16)


def _head_stage(base, hd, par, z_sc, lvl, w_sc, qb_sc, kb_sc, od_sc, dec_sc, o_sc, st_sc):
    rs = slice(base, base + CHUNK)
    sl = slice(hd * DK, (hd + 1) * DK)
    half = CHUNK // 2
    a_lo = jnp.zeros((half, half), F32)
    a_hi = jnp.zeros((half, half), F32)
    zero = jnp.zeros((half, DK), BF16)
    for li in range(len(LEVEL_HALVES) - 1):
        w_lo = w_sc[par, li, :half, sl]
        w_hi = w_sc[par, li, half:, sl]
        x = jnp.concatenate([w_lo, w_hi], axis=1)
        y = jnp.concatenate([jnp.concatenate([w_lo.T, zero], axis=1),
                             jnp.concatenate([zero, w_hi.T], axis=1)], axis=0)
        p = _dot(x, y)
        a_lo = jnp.where(lvl == li, p[:, :half], a_lo)
        a_hi = jnp.where(lvl == li, p[:, half:], a_hi)
    top = len(LEVEL_HALVES) - 1
    a_x = _dot_nt(w_sc[par, top, half:, sl], w_sc[par, top, :half, sl])
    A = jnp.concatenate([jnp.concatenate([a_lo, jnp.zeros((half, half), F32)], axis=1),
                         jnp.concatenate([a_x, a_hi], axis=1)], axis=0)
    v_bf = z_sc[rs, O_V + hd * DV:O_V + (hd + 1) * DV].astype(BF16)
    st = st_sc[hd]
    o_sc[rs, sl] = (_dot(A.astype(BF16), v_bf)
                    + _dot_nt(qb_sc[par, :, sl], st.astype(BF16))
                    + od_sc[par, :, sl])
    st_sc[hd] = st * dec_sc[par, 0:1, sl] + _dot_tn(v_bf, kb_sc[par, :, sl])


def _slice_copies(step, hbm_refs, vmem_refs, sems, to_hbm):
    copies = []
    for hbm, vmem, sem in zip(hbm_refs, vmem_refs, sems):
        rows = vmem.shape[0]
        window = hbm.at[pl.ds(pl.multiple_of(step * rows, rows), rows), :]
        copies.append(pltpu.make_async_copy(vmem, window, sem) if to_hbm
                      else pltpu.make_async_copy(window, vmem, sem))
    return copies


def _load_mixer_weights(hbm_refs, scales, vmem_refs, stage, sems):
    jobs = [(src, dst, scale, r0)
            for src, dst, scale in zip(hbm_refs, vmem_refs, scales)
            for r0 in range(0, src.shape[0], STAGE_ROWS)]

    def fetch(i):
        src, _, _, r0 = jobs[i]
        return pltpu.make_async_copy(src.at[r0:r0 + STAGE_ROWS, :],
                                     stage.at[i % 2, :, 0:src.shape[1]], sems.at[i % 2])

    fetch(0).start()
    for i, (src, dst, scale, r0) in enumerate(jobs):
        if i + 1 < len(jobs):
            fetch(i + 1).start()
        fetch(i).wait()
        vals = stage[i % 2, :, 0:src.shape[1]]
        if scale is not None:
            vals = vals * scale
        dst[r0:r0 + STAGE_ROWS, :] = vals.astype(BF16)


def _export_copies(vmem_refs, hbm_refs, sems):
    return [pltpu.make_async_copy(v, h, sems.at[i]) for i, (v, h) in enumerate(zip(vmem_refs, hbm_refs))]


def _prompt_mixer_kernel(x_ref, meta_ref, gmix_ref, wpool_ref, pscale_ref, lbl_ref, gon_ref, cscale_ref,
                         win_hbm, wa_hbm, wb_hbm, wout_hbm, wup_hbm, wdn_hbm,
                         x1_ref, pool_ref, hgrn_ref,
                         win16_hbm, wa16_hbm, wb16_hbm, wout16_hbm, wup16_hbm, wdn16_hbm,
                         z_sc, h_sc, ya_sc, ubuf, q_sc, k_sc, fg_sc, g_sc, od_sc, dec_sc,
                         w_sc, qb_sc, kb_sc, o_sc, st_sc, st_meta,
                         u_meta, up32, dn32, up16, dn16, cast_sems,
                         win_ref, wa_ref, wb_ref, wout_ref, stage, load_sems, export_sems, *, tile):
    b = pl.program_id(0)
    t = pl.program_id(1)
    nt = pl.num_programs(1)
    lb = _lower_bound(lbl_ref[...])
    gmix = gmix_ref[...]
    mixer_weights = (win_ref, wa_ref, wb_ref, wout_ref)
    exports = (win16_hbm, wa16_hbm, wb16_hbm, wout16_hbm)

    step = b * nt + t
    last_step = pl.num_programs(0) * nt - 1
    in_sems, out_sems = (cast_sems.at[0], cast_sems.at[1]), (cast_sems.at[2], cast_sems.at[3])

    def cast_out(s):
        return _slice_copies(s, (wup16_hbm, wdn16_hbm), (up16, dn16), out_sems, to_hbm=True)

    cast_in = _slice_copies(step, (wup_hbm, wdn_hbm), (up32, dn32), in_sems, to_hbm=False)
    for cp in cast_in:
        cp.start()

    @pl.when((b == 0) & (t == 0))
    def _meta():
        _load_mixer_weights((win_hbm, wa_hbm, wb_hbm, wout_hbm), (cscale_ref[...], None, None, 0.5),
                            mixer_weights, stage, load_sems)
        for cp in _export_copies(mixer_weights, exports, export_sems):
            cp.start()
        hm = _rms(meta_ref[...], gmix).astype(BF16)
        zm = _dot(hm, win_ref[:, 0:O_OG])
        u_meta[...] = zm[:, O_U:O_U + POOL_WIDTH]
        _, fg = _gates(zm[:, O_Q:O_Q + KTOT], zm[:, O_F:O_F + KTOT], lb)
        G = _cumsum_rows(jnp.log2(fg))
        kt = (1.0 - fg) * jnp.exp2(G[N_META - 1:N_META, :] - G)
        v = zm[:, O_V:O_V + HWIDTH]
        for h in range(HEADS):
            sl = slice(h * DK, (h + 1) * DK)
            st_meta[h] = _dot_tn(v[:, sl].astype(BF16), kt[:, sl].astype(BF16))

    @pl.when(t == 0)
    def _init():
        st_sc[...] = st_meta[...]
        ubuf[0:POOL_MAXW, :] = u_meta[...]

    n_chunks = tile // CHUNK
    h_sc[...] = _rms(x_ref[0], gmix).astype(BF16)
    z_sc[:, 0:O_OG] = _dot(h_sc[...], win_ref[:, 0:O_OG])

    u = z_sc[:, O_U:O_U + POOL_WIDTH]
    ubuf[POOL_MAXW:POOL_MAXW + tile, :] = u
    ya_parts = []
    for gi, w in enumerate(POOL_WINDOWS):
        cs = slice(gi * POOL_GROUP, (gi + 1) * POOL_GROUP)
        wsum = ubuf[:, cs]
        span = 1
        while span < w:
            wsum = wsum + pltpu.roll(wsum, span, 0)
            span *= 2
        pooled = wsum[POOL_MAXW:, :] / float(w) - u[:, cs]
        ya_parts.append(pooled.astype(BF16))
    ya_pre = _group_maps(ya_parts, wpool_ref) * pscale_ref[...]
    ya_sc[...] = _dot(ya_pre.astype(BF16), wa_ref[...])
    ubuf[0:POOL_MAXW, :] = ubuf[tile:tile + POOL_MAXW, :]

    def gate_piece(r0, c0):
        z_sc[r0:r0 + GATE_ROWS, c0:c0 + GATE_PIECE] = _dot(
            h_sc[r0:r0 + GATE_ROWS, :], win_ref[:, c0:c0 + GATE_PIECE])
    pending = [functools.partial(gate_piece, r0, c0)
               for c0 in range(O_OG, N_IN, GATE_PIECE) for r0 in range(0, tile, GATE_ROWS)]
    n_pieces = len(pending)
    n_groups = CHUNK // SUB
    n_slots = n_chunks * (n_groups + HEADS)
    slot = [0]

    def interleave():
        slot[0] += 1
        while len(pending) > n_pieces - (-(-slot[0] * n_pieces // n_slots)):
            pending.pop(0)()

    def gates(c):
        _gates_stage(c * CHUNK, c % 2, z_sc, lb, q_sc, k_sc, fg_sc, g_sc, od_sc, dec_sc)

    def group(c, j):
        _group_stage(j, c % 2, q_sc, k_sc, fg_sc, g_sc, w_sc, qb_sc, kb_sc)
        interleave()

    def head(c, hd):
        _head_stage(c * CHUNK, hd, c % 2, z_sc, lvl, w_sc, qb_sc, kb_sc, od_sc, dec_sc, o_sc, st_sc)
        interleave()

    lvl = _level_map(CHUNK // 2)
    gates(0)
    for j in range(n_groups):
        group(0, j)
    for c in range(n_chunks):
        if c + 1 < n_chunks:
            gates(c + 1)
        for hd in range(HEADS):
            head(c, hd)
            if c + 1 < n_chunks:
                for j in range(hd * n_groups // HEADS, (hd + 1) * n_groups // HEADS):
                    group(c + 1, j)
    assert not pending and slot[0] == n_slots

    gon = gon_ref[...]
    o_parts = []
    for hd in range(HEADS):
        sl = slice(hd * DV, (hd + 1) * DV)
        og = z_sc[:, O_OG + hd * DV:O_OG + (hd + 1) * DV]
        o_parts.append(_rms(o_sc[:, sl], gon) * _silu(og))
    yb = _dot(jnp.concatenate(o_parts, axis=-1).astype(BF16), wb_ref[...])
    m = (_twice_sigmoid(z_sc[:, O_GA:O_GA + D_MODEL]) * ya_sc[...]
         + _twice_sigmoid(z_sc[:, O_GB:O_GB + D_MODEL]) * yb)
    x1_ref[0] = x_ref[0] + _dot(m.astype(BF16), wout_ref[...])

    for cp in cast_in:
        cp.wait()

    @pl.when(step > 0)
    def _drain_previous():
        for cp in cast_out(step - 1):
            cp.wait()

    up16[...] = up32[...].astype(BF16)
    dn16[...] = dn32[...].astype(BF16)
    for cp in cast_out(step):
        cp.start()

    @pl.when(step == last_step)
    def _drain_last():
        for cp in cast_out(step):
            cp.wait()
        for cp in _export_copies(mixer_weights, exports, export_sems):
            cp.wait()

    @pl.when(t == nt - 1)
    def _state_out():
        pool_ref[0] = ubuf[1:POOL_MAXW, :]
        for hd in range(HEADS):
            hgrn_ref[0, hd] = st_sc[hd].T


def _const_spec(shape):
    nd = len(shape)
    return pl.BlockSpec(shape, lambda *_: (0,) * nd, pipeline_mode=pl.Buffered(1))


def _prompt_mixer(x, meta, gmix, wpool, pscale, lbl, gon, cscale, win, wa, wb, wout, wup, wdown, *, tile):
    B, L, _ = x.shape
    assert L % tile == 0 and tile % CHUNK == 0 and tile % GATE_ROWS == 0
    nt = L // tile
    n_steps = B * nt
    assert wup.shape[0] % (SUB * n_steps) == 0 and wdown.shape[0] % (SUB * n_steps) == 0
    up_rows, dn_rows = wup.shape[0] // n_steps, wdown.shape[0] // n_steps
    own = (win, wa, wb, wout)
    assert all(w.shape[0] % STAGE_ROWS == 0 and w.shape[1] <= N_IN for w in own)
    kern = functools.partial(_prompt_mixer_kernel, tile=tile)
    return pl.pallas_call(
        kern,
        grid=(B, nt),
        in_specs=[
            pl.BlockSpec((1, tile, D_MODEL), lambda b, t: (b, t, 0)),
            _const_spec((N_META, D_MODEL)),
            _const_spec((1, D_MODEL)),
            _const_spec((len(POOL_WINDOWS) // 2, 2 * POOL_GROUP, 2 * POOL_GROUP)),
            _const_spec((1, POOL_WIDTH)),
            _const_spec((2, KTOT)),
            _const_spec((1, DV)),
            _const_spec((1, N_IN)),
        ] + [pl.BlockSpec(memory_space=pl.ANY)] * 6,
        out_specs=[
            pl.BlockSpec((1, tile, D_MODEL), lambda b, t: (b, t, 0)),
            pl.BlockSpec((1, POOL_BUF, POOL_WIDTH), lambda b, t: (b, 0, 0)),
            pl.BlockSpec((1, HEADS, DK, DV), lambda b, t: (b, 0, 0, 0)),
        ] + [pl.BlockSpec(memory_space=pl.ANY)] * 6,
        out_shape=[
            jax.ShapeDtypeStruct((B, L, D_MODEL), F32),
            jax.ShapeDtypeStruct((B, POOL_BUF, POOL_WIDTH), F32),
            jax.ShapeDtypeStruct((B, HEADS, DK, DV), F32),
        ] + [jax.ShapeDtypeStruct(w.shape, BF16) for w in own + (wup, wdown)],
        scratch_shapes=[
            pltpu.VMEM((tile, N_IN), F32),
            pltpu.VMEM((tile, D_MODEL), BF16),
            pltpu.VMEM((tile, D_MODEL), F32),
            pltpu.VMEM((tile + POOL_MAXW, POOL_WIDTH), F32),
            pltpu.VMEM((CHUNK, KTOT), F32),
            pltpu.VMEM((CHUNK, KTOT), F32),
            pltpu.VMEM((CHUNK, KTOT), F32),
            pltpu.VMEM((CHUNK, KTOT), F32),
            pltpu.VMEM((2, CHUNK, HWIDTH), F32),
            pltpu.VMEM((2, 8, KTOT), F32),
            pltpu.VMEM((2, len(LEVEL_HALVES), CHUNK, KTOT), BF16),
            pltpu.VMEM((2, CHUNK, KTOT), BF16),
            pltpu.VMEM((2, CHUNK, KTOT), BF16),
            pltpu.VMEM((tile, HWIDTH), F32),
            pltpu.VMEM((HEADS, DV, DK), F32),
            pltpu.VMEM((HEADS, DV, DK), F32),
            pltpu.VMEM((N_META, POOL_WIDTH), F32),
            pltpu.VMEM((up_rows, wup.shape[1]), F32),
            pltpu.VMEM((dn_rows, wdown.shape[1]), F32),
            pltpu.VMEM((up_rows, wup.shape[1]), BF16),
            pltpu.VMEM((dn_rows, wdown.shape[1]), BF16),
            pltpu.SemaphoreType.DMA((4,)),
        ] + [pltpu.VMEM(w.shape, BF16) for w in own] + [
            pltpu.VMEM((2, STAGE_ROWS, N_IN), F32),
            pltpu.SemaphoreType.DMA((2,)),
            pltpu.SemaphoreType.DMA((len(own),)),
        ],
        compiler_params=pltpu.CompilerParams(
            dimension_semantics=("arbitrary", "arbitrary"),
            vmem_limit_bytes=VMEM_LIMIT),
        name="prompt_mixer",
    )(x, meta, gmix, wpool, pscale, lbl, gon, cscale, win, wa, wb, wout, wup, wdown)


def _decode_mixer_kernel(x_ref, poolT_ref, s_ref, gmix_ref, win_ref, wpool_ref, pscale_ref,
                         lbl_ref, gon_ref, wa_ref, wb_ref, wout_ref,
                         x1_ref, unew_ref, snew_ref,
                         z_sc, o_sc, *, ts):
    i = pl.program_id(0)
    n = pl.num_programs(0)
    lb = _lower_bound(lbl_ref[...])

    @pl.when(i == 0)
    def _in_proj():
        h = _rms(x_ref[...], gmix_ref[...]).astype(BF16)
        z_sc[...] = _dot(h, win_ref[...])

    r = pl.ds(pl.multiple_of(i * ts, ts), ts)
    q, fg = _gates(z_sc[r, O_Q:O_Q + KTOT], z_sc[r, O_F:O_F + KTOT], lb)
    k = 1.0 - fg
    v = z_sc[r, O_V:O_V + HWIDTH]
    row = lax.broadcasted_iota(jnp.int32, (ts, 1), 0)
    for hd in range(HEADS):
        sl = slice(hd * DK, (hd + 1) * DK)
        fT = fg[:, sl].T
        kT = k[:, sl].T
        qf = (q[:, sl] * fg[:, sl]).astype(BF16)
        qk = jnp.sum(q[:, sl] * k[:, sl], axis=-1, keepdims=True)
        o = qk * v[:, sl]
        for s in range(ts):
            s_old = s_ref[s, hd]
            snew_ref[s, hd] = fT[:, s:s + 1] * s_old + kT[:, s:s + 1] * v[s:s + 1, sl]
            o_row = _dot(qf[s:s + 1, :], s_old.astype(BF16))
            o = o + jnp.where(row == s, o_row, 0.0)
        o_sc[r, sl] = o

    @pl.when(i == n - 1)
    def _out_proj():
        x = x_ref[...]
        u = z_sc[:, O_U:O_U + POOL_WIDTH]
        unew_ref[...] = u
        ya_parts = []
        for gi, w in enumerate(POOL_WINDOWS):
            cs = slice(gi * POOL_GROUP, (gi + 1) * POOL_GROUP)
            wsum = u[:, cs]
            for j in range(1, w):
                wsum = wsum + poolT_ref[POOL_BUF - j, :, cs]
            pooled = wsum / float(w) - u[:, cs]
            ya_parts.append(pooled.astype(BF16))
        ya_pre = _group_maps(ya_parts, wpool_ref) * pscale_ref[...]
        ya = _dot(ya_pre.astype(BF16), wa_ref[...])
        gon = gon_ref[...]
        o_parts = []
        for hd in range(HEADS):
            sl = slice(hd * DV, (hd + 1) * DV)
            og = z_sc[:, O_OG + hd * DV:O_OG + (hd + 1) * DV]
            o_parts.append(_rms(o_sc[:, sl], gon) * _silu(og))
        yb = _dot(jnp.concatenate(o_parts, axis=-1).astype(BF16), wb_ref[...])
        m = (_twice_sigmoid(z_sc[:, O_GA:O_GA + D_MODEL]) * ya
             + _twice_sigmoid(z_sc[:, O_GB:O_GB + D_MODEL]) * yb)
        x1_ref[...] = x + _dot(m.astype(BF16), wout_ref[...])


def _decode_mixer(x, poolT, state, gmix, win, wpool, pscale, lbl, gon, wa, wb, wout, *, ts):
    n = x.shape[0]
    assert n % ts == 0
    kern = functools.partial(_decode_mixer_kernel, ts=ts)
    return pl.pallas_call(
        kern,
        grid=(n // ts,),
        in_specs=[
            _const_spec((n, D_MODEL)),
            _const_spec((POOL_BUF, n, POOL_WIDTH)),
            pl.BlockSpec((ts, HEADS, DK, DV), lambda i: (i, 0, 0, 0)),
            _const_spec((1, D_MODEL)),
            _const_spec((D_MODEL, N_IN)),
            _const_spec((len(POOL_WINDOWS) // 2, 2 * POOL_GROUP, 2 * POOL_GROUP)),
            _const_spec((1, POOL_WIDTH)),
            _const_spec((2, KTOT)),
            _const_spec((1, DV)),
            _const_spec((POOL_WIDTH, D_MODEL)),
            _const_spec((HWIDTH, D_MODEL)),
            _const_spec((D_MODEL, D_MODEL)),
        ],
        out_specs=[
            pl.BlockSpec((n, D_MODEL), lambda i: (0, 0)),
            pl.BlockSpec((n, POOL_WIDTH), lambda i: (0, 0)),
            pl.BlockSpec((ts, HEADS, DK, DV), lambda i: (i, 0, 0, 0)),
        ],
        out_shape=[
            jax.ShapeDtypeStruct((n, D_MODEL), F32),
            jax.ShapeDtypeStruct((n, POOL_WIDTH), F32),
            jax.ShapeDtypeStruct((n, HEADS, DK, DV), F32),
        ],
        scratch_shapes=[
            pltpu.VMEM((n, N_IN), F32),
            pltpu.VMEM((n, HWIDTH), F32),
        ],
        compiler_params=pltpu.CompilerParams(
            dimension_semantics=("arbitrary",),
            vmem_limit_bytes=VMEM_LIMIT),
        name="decode_mixer",
    )(x, poolT, state, gmix, win, wpool, pscale, lbl, gon, wa, wb, wout)


def _mlp_kernel(x_ref, xs_ref, gmlp_ref, wup_ref, wdown_ref, gfin_ref, y_ref, ys_ref, *, ff_chunk):
    i = pl.program_id(0)
    last = pl.num_programs(0) - 1

    def rows(src_ref, dst_ref):
        x = src_ref[...]
        h = _rms(x, gmlp_ref[...]).astype(BF16)
        acc = x
        for c in range(D_FF // ff_chunk):
            cs = slice(c * ff_chunk, (c + 1) * ff_chunk)
            a = jnp.maximum(_dot(h, wup_ref[:, cs]), 0.0)
            acc = acc + _dot((a * a).astype(BF16), wdown_ref[cs, :])
        dst_ref[...] = _rms(acc, gfin_ref[...])

    @pl.when(i < last)
    def _prompt_tile():
        rows(x_ref, y_ref)

    @pl.when(i == last)
    def _sample_rows():
        rows(xs_ref, ys_ref)


def _mlp(x, xs, gmlp, wup, wdown, gfin, *, tm, ff_chunk=1024):
    n, ns = x.shape[0], xs.shape[0]
    assert n % tm == 0 and D_FF % ff_chunk == 0
    n_tiles = n // tm
    kern = functools.partial(_mlp_kernel, ff_chunk=ff_chunk)

    def tile(i):
        return jnp.minimum(i, n_tiles - 1), 0

    return pl.pallas_call(
        kern,
        grid=(n_tiles + 1,),
        in_specs=[
            pl.BlockSpec((tm, D_MODEL), tile),
            _const_spec((ns, D_MODEL)),
            _const_spec((1, D_MODEL)),
            _const_spec((D_MODEL, D_FF)),
            _const_spec((D_FF, D_MODEL)),
            _const_spec((1, D_MODEL)),
        ],
        out_specs=[
            pl.BlockSpec((tm, D_MODEL), tile),
            pl.BlockSpec((ns, D_MODEL), lambda i: (0, 0)),
        ],
        out_shape=[
            jax.ShapeDtypeStruct((n, D_MODEL), F32),
            jax.ShapeDtypeStruct((ns, D_MODEL), F32),
        ],
        compiler_params=pltpu.CompilerParams(
            dimension_semantics=("arbitrary",),
            vmem_limit_bytes=VMEM_LIMIT),
        name="channel_mlp",
    )(x, xs, gmlp, wup, wdown, gfin)


def kernel(x_prompt, x_sample, state_pool, state_hgrn, meta_tokens, g_mix, w_in, w_pool, pool_scale,
           hgrn_lb_logits, g_onorm, w_a, w_b, w_out, g_mlp, w_up, w_down, g_final):
    B, L, _ = x_prompt.shape
    NS = x_sample.shape[0]
    assert g_mix.shape[0] == 1, "single-layer trunk"
    gmix = g_mix[0][None, :]
    col = jnp.arange(N_IN)
    gate_col = ((col >= O_Q) & (col < O_V)) | (col >= O_OG)
    cscale = jnp.where(gate_col, 0.5, 1.0).astype(F32)[None, :]
    wp = w_pool[0].astype(BF16).reshape(len(POOL_WINDOWS) // 2, 2, POOL_GROUP, POOL_GROUP)
    zp = jnp.zeros_like(wp[:, 0])
    wpool = jnp.concatenate([jnp.concatenate([wp[:, 0], zp], axis=2),
                             jnp.concatenate([zp, wp[:, 1]], axis=2)], axis=1)
    pscale = pool_scale[0][None, :]
    gon = g_onorm[0][None, :]
    gmlp = g_mlp[0][None, :]
    gfin = g_final[None, :]

    x1_p, pool_p, hgrn_p, win, wa, wb, wout, wup, wdown = _prompt_mixer(
        x_prompt, meta_tokens, gmix, wpool, pscale, hgrn_lb_logits, gon, cscale,
        w_in[0], w_a[0], w_b[0], w_out[0], w_up[0], w_down[0], tile=MIXER_TILE)

    xs = x_sample.reshape(NS, D_MODEL)
    poolT = jnp.swapaxes(state_pool[0], 0, 1)
    x1_s, u_s, hgrn_s = _decode_mixer(
        xs, poolT, state_hgrn[0], gmix, win, wpool, pscale, hgrn_lb_logits, gon, wa, wb, wout,
        ts=DECODE_TILE)

    y_p, y_s = _mlp(x1_p.reshape(B * L, D_MODEL), x1_s, gmlp, wup, wdown, gfin, tm=MLP_TILE)
    pool_s = jnp.concatenate([state_pool[0][:, 1:, :], u_s[:, None, :]], axis=1)

    return (y_p.reshape(B, L, D_MODEL), y_s.reshape(NS, 1, D_MODEL),
            pool_p[None], hgrn_p[None], pool_s[None], hgrn_s[None])
```

```python
import functools

import jax
import jax.numpy as jnp
from jax import lax
from jax.experimental import pallas as pl
from jax.experimental.pallas import tpu as pltpu

D_MODEL = 1024
N_META = 16
POOL_WIDTH = 512
POOL_WINDOWS = (2, 4, 8, 16)
POOL_GROUP = 128
POOL_MAXW = 16
POOL_BUF = 15
HEADS = 4
DK = 128
DV = 128
KTOT = 512
HWIDTH = 512
D_FF = 4096
EPS = 1e-6
N_IN = 4608
O_U, O_Q, O_F, O_V, O_OG, O_GA, O_GB = 0, 512, 1024, 1536, 2048, 2560, 3584

SUB = 16
CHUNK = 256
GATE_PIECE = 256
GATE_ROWS = 512
STAGE_ROWS = 256
LEVEL_HALVES = tuple(1 << i for i in range(CHUNK.bit_length() - 1))

V7X_VMEM_BYTES = 64 * 1024 * 1024
VMEM_LIMIT = V7X_VMEM_BYTES - 8 * 1024 * 1024
MIXER_TILE = 2 * CHUNK
MLP_TILE = 512
DECODE_TILE = SUB

F32 = jnp.float32
BF16 = jnp.bfloat16
F32_TINY = 1.1754944e-38


def _rms(x, g):
    return x * lax.rsqrt(jnp.mean(x * x, axis=-1, keepdims=True) + EPS) * g


def _twice_sigmoid(hx):
    return jnp.tanh(hx) + 1.0


def _silu(hx):
    return hx * jnp.tanh(hx) + hx


def _dot(a, b):
    return jnp.dot(a, b, preferred_element_type=F32)


def _dot_nt(a, b):
    return lax.dot_general(a, b, (((1,), (1,)), ((), ())), preferred_element_type=F32)


def _dot_tn(a, b):
    return lax.dot_general(a, b, (((0,), (0,)), ((), ())), preferred_element_type=F32)


def _group_maps(pooled_groups, wpool_ref):
    outs = []
    for p in range(len(pooled_groups) // 2):
        pair = jnp.concatenate(pooled_groups[2 * p:2 * p + 2], axis=-1)
        outs.append(_dot(pair, wpool_ref[p]))
    return jnp.concatenate(outs, axis=-1)


def _lower_bound(lb_logits):
    m = jnp.max(lb_logits, axis=0, keepdims=True)
    e = jnp.exp(lb_logits - m)
    return e[0:1, :] / jnp.sum(e, axis=0, keepdims=True)


def _gates(hq, hf, lb):
    q = _silu(hq)
    fg = 0.5 * (1.0 + lb) + (0.5 * (1.0 - lb)) * jnp.tanh(hf)
    return q, jnp.maximum(fg, F32_TINY)


def _cumsum_rows(g):
    n = g.shape[0]
    r = lax.broadcasted_iota(jnp.int32, (n, n), 0)
    c = lax.broadcasted_iota(jnp.int32, (n, n), 1)
    tril = (r >= c).astype(BF16)
    hi = g.astype(BF16)
    r1 = g - hi.astype(F32)
    mid = r1.astype(BF16)
    lo = (r1 - mid.astype(F32)).astype(BF16)
    return _dot(tril, hi) + _dot(tril, mid) + _dot(tril, lo)


def _level_map(n):
    t = lax.broadcasted_iota(jnp.int32, (n, n), 0)
    s = lax.broadcasted_iota(jnp.int32, (n, n), 1)
    x = t ^ s
    lvl = jnp.full((n, n), -1, jnp.int32)
    for li in range(n.bit_length() - 1):
        lvl = jnp.where((x >> li) == 1, li, lvl)
    return jnp.where(t > s, lvl, -1)


def _level_weights(m, r0, Gg, qg, kg, fgg, g_sc, row8, row16):
    if m == 1:
        return jnp.where((row16 & 1) != 0, qg * fgg, kg)
    if m >= SUB:
        blk = (r0 // (2 * m)) * (2 * m)
        ref = g_sc[blk + m - 1:blk + m, :]
        return (jnp.exp2(Gg - ref) * qg) if (r0 & m) else (jnp.exp2(ref - Gg) * kg)
    if m == SUB // 2:
        ref = g_sc[r0 + m - 1:r0 + m, :]
        return jnp.concatenate([jnp.exp2(ref - Gg[:m]) * kg[:m], jnp.exp2(Gg[m:] - ref) * qg[m:]], axis=0)
    halves = []
    for rb in (r0, r0 + 8):
        if m == 4:
            halves.append(jnp.broadcast_to(g_sc[rb + 3:rb + 4, :], (8, KTOT)))
        else:
            halves.append(jnp.where(row8 >= 4, g_sc[rb + 5:rb + 6, :], g_sc[rb + 1:rb + 2, :]))
    ref = jnp.concatenate(halves, axis=0)
    sel = jnp.where((row16 & m) != 0, qg, kg)
    return jnp.exp2(-jnp.abs(Gg - ref)) * sel


def _gates_stage(base, par, z_sc, lb, q_sc, k_sc, fg_sc, g_sc, od_sc, dec_sc):
    rs = slice(base, base + CHUNK)
    q, fg = _gates(z_sc[rs, O_Q:O_Q + KTOT], z_sc[rs, O_F:O_F + KTOT], lb)
    k = 1.0 - fg
    G = _cumsum_rows(jnp.log2(fg))
    q_sc[...] = q
    k_sc[...] = k
    fg_sc[...] = fg
    g_sc[...] = G
    v = z_sc[rs, O_V:O_V + HWIDTH]
    qk = q * k
    for hd in range(HEADS):
        sl = slice(hd * DK, (hd + 1) * DK)
        od_sc[par, :, sl] = jnp.sum(qk[:, sl], axis=-1, keepdims=True) * v[:, sl]
    dec_sc[par] = jnp.broadcast_to(jnp.exp2(G[CHUNK - 1:CHUNK, :]), (8, KTOT))


def _group_stage(j, par, q_sc, k_sc, fg_sc, g_sc, w_sc, qb_sc, kb_sc):
    row8 = lax.broadcasted_iota(jnp.int32, (8, 1), 0)
    row16 = lax.broadcasted_iota(jnp.int32, (SUB, 1), 0)
    r0 = j * SUB
    gs = slice(r0, r0 + SUB)
    Gg, qg, kg, fgg = g_sc[gs, :], q_sc[gs, :], k_sc[gs, :], fg_sc[gs, :]
    g_last = g_sc[CHUNK - 1:CHUNK, :]
    for li, m in enumerate(LEVEL_HALVES):
        w_sc[par, li, gs, :] = _level_weights(m, r0, Gg, qg, kg, fgg, g_sc, row8, row16).astype(BF16)
    qb_sc[par, gs, :] = (qg * jnp.exp2(Gg)).astype(BF16)
    kb_sc[par, gs, :] = (kg * jnp.exp2(g_last - Gg)).astype(BF16)


def _head_stage(base, hd, par, z_sc, lvl, w_sc, qb_sc, kb_sc, od_sc, dec_sc, o_sc, st_sc):
    rs = slice(base, base + CHUNK)
    sl = slice(hd * DK, (hd + 1) * DK)
    half = CHUNK // 2
    a_lo = jnp.zeros((half, half), F32)
    a_hi = jnp.zeros((half, half), F32)
    zero = jnp.zeros((half, DK), BF16)
    for li in range(len(LEVEL_HALVES) - 1):
        w_lo = w_sc[par, li, :half, sl]
        w_hi = w_sc[par, li, half:, sl]
        x = jnp.concatenate([w_lo, w_hi], axis=1)
        y = jnp.concatenate([jnp.concatenate([w_lo.T, zero], axis=1),
                             jnp.concatenate([zero, w_hi.T], axis=1)], axis=0)
        p = _dot(x, y)
        a_lo = jnp.where(lvl == li, p[:, :half], a_lo)
        a_hi = jnp.where(lvl == li, p[:, half:], a_hi)
    top = len(LEVEL_HALVES) - 1
    a_x = _dot_nt(w_sc[par, top, half:, sl], w_sc[par, top, :half, sl])
    A = jnp.concatenate([jnp.concatenate([a_lo, jnp.zeros((half, half), F32)], axis=1),
                         jnp.concatenate([a_x, a_hi], axis=1)], axis=0)
    v_bf = z_sc[rs, O_V + hd * DV:O_V + (hd + 1) * DV].astype(BF16)
    st = st_sc[hd]
    o_sc[rs, sl] = (_dot(A.astype(BF16), v_bf)
                    + _dot_nt(qb_sc[par, :, sl], st.astype(BF16))
                    + od_sc[par, :, sl])
    st_sc[hd] = st * dec_sc[par, 0:1, sl] + _dot_tn(v_bf, kb_sc[par, :, sl])


def _slice_copies(step, hbm_refs, vmem_refs, sems, to_hbm):
    copies = []
    for hbm, vmem, sem in zip(hbm_refs, vmem_refs, sems):
        rows = vmem.shape[0]
        window = hbm.at[pl.ds(pl.multiple_of(step * rows, rows), rows), :]
        copies.append(pltpu.make_async_copy(vmem, window, sem) if to_hbm
                      else pltpu.make_async_copy(window, vmem, sem))
    return copies


def _load_mixer_weights(hbm_refs, scales, vmem_refs, stage, sems):
    jobs = [(src, dst, scale, r0)
            for src, dst, scale in zip(hbm_refs, vmem_refs, scales)
            for r0 in range(0, src.shape[0], STAGE_ROWS)]

    def fetch(i):
        src, _, _, r0 = jobs[i]
        return pltpu.make_async_copy(src.at[r0:r0 + STAGE_ROWS, :],
                                     stage.at[i % 2, :, 0:src.shape[1]], sems.at[i % 2])

    fetch(0).start()
    for i, (src, dst, scale, r0) in enumerate(jobs):
        if i + 1 < len(jobs):
            fetch(i + 1).start()
        fetch(i).wait()
        vals = stage[i % 2, :, 0:src.shape[1]]
        if scale is not None:
            vals = vals * scale
        dst[r0:r0 + STAGE_ROWS, :] = vals.astype(BF16)


def _export_copies(vmem_refs, hbm_refs, sems):
    return [pltpu.make_async_copy(v, h, sems.at[i]) for i, (v, h) in enumerate(zip(vmem_refs, hbm_refs))]


def _prompt_mixer_kernel(x_ref, meta_ref, gmix_ref, wpool_ref, pscale_ref, lbl_ref, gon_ref, cscale_ref,
                         win_hbm, wa_hbm, wb_hbm, wout_hbm, wup_hbm, wdn_hbm,
                         x1_ref, pool_ref, hgrn_ref,
                         win16_hbm, wa16_hbm, wb16_hbm, wout16_hbm, wup16_hbm, wdn16_hbm,
                         z_sc, h_sc, ya_sc, ubuf, q_sc, k_sc, fg_sc, g_sc, od_sc, dec_sc,
                         w_sc, qb_sc, kb_sc, o_sc, st_sc, st_meta,
                         u_meta, up32, dn32, up16, dn16, cast_sems,
                         win_ref, wa_ref, wb_ref, wout_ref, stage, load_sems, export_sems, *, tile):
    b = pl.program_id(0)
    t = pl.program_id(1)
    nt = pl.num_programs(1)
    lb = _lower_bound(lbl_ref[...])
    gmix = gmix_ref[...]
    mixer_weights = (win_ref, wa_ref, wb_ref, wout_ref)
    exports = (win16_hbm, wa16_hbm, wb16_hbm, wout16_hbm)

    step = b * nt + t
    last_step = pl.num_programs(0) * nt - 1
    in_sems, out_sems = (cast_sems.at[0], cast_sems.at[1]), (cast_sems.at[2], cast_sems.at[3])

    def cast_out(s):
        return _slice_copies(s, (wup16_hbm, wdn16_hbm), (up16, dn16), out_sems, to_hbm=True)

    cast_in = _slice_copies(step, (wup_hbm, wdn_hbm), (up32, dn32), in_sems, to_hbm=False)
    for cp in cast_in:
        cp.start()

    @pl.when((b == 0) & (t == 0))
    def _meta():
        _load_mixer_weights((win_hbm, wa_hbm, wb_hbm, wout_hbm), (cscale_ref[...], None, None, 0.5),
                            mixer_weights, stage, load_sems)
        for cp in _export_copies(mixer_weights, exports, export_sems):
            cp.start()
        hm = _rms(meta_ref[...], gmix).astype(BF16)
        zm = _dot(hm, win_ref[:, 0:O_OG])
        u_meta[...] = zm[:, O_U:O_U + POOL_WIDTH]
        _, fg = _gates(zm[:, O_Q:O_Q + KTOT], zm[:, O_F:O_F + KTOT], lb)
        G = _cumsum_rows(jnp.log2(fg))
        kt = (1.0 - fg) * jnp.exp2(G[N_META - 1:N_META, :] - G)
        v = zm[:, O_V:O_V + HWIDTH]
        for h in range(HEADS):
            sl = slice(h * DK, (h + 1) * DK)
            st_meta[h] = _dot_tn(v[:, sl].astype(BF16), kt[:, sl].astype(BF16))

    @pl.when(t == 0)
    def _init():
        st_sc[...] = st_meta[...]
        ubuf[0:POOL_MAXW, :] = u_meta[...]

    n_chunks = tile // CHUNK
    h_sc[...] = _rms(x_ref[0], gmix).astype(BF16)
    z_sc[:, 0:O_OG] = _dot(h_sc[...], win_ref[:, 0:O_OG])

    u = z_sc[:, O_U:O_U + POOL_WIDTH]
    ubuf[POOL_MAXW:POOL_MAXW + tile, :] = u
    ya_parts = []
    for gi, w in enumerate(POOL_WINDOWS):
        cs = slice(gi * POOL_GROUP, (gi + 1) * POOL_GROUP)
        wsum = ubuf[:, cs]
        span = 1
        while span < w:
            wsum = wsum + pltpu.roll(wsum, span, 0)
            span *= 2
        pooled = wsum[POOL_MAXW:, :] / float(w) - u[:, cs]
        ya_parts.append(pooled.astype(BF16))
    ya_pre = _group_maps(ya_parts, wpool_ref) * pscale_ref[...]
    ya_sc[...] = _dot(ya_pre.astype(BF16), wa_ref[...])
    ubuf[0:POOL_MAXW, :] = ubuf[tile:tile + POOL_MAXW, :]

    def gate_piece(r0, c0):
        z_sc[r0:r0 + GATE_ROWS, c0:c0 + GATE_PIECE] = _dot(
            h_sc[r0:r0 + GATE_ROWS, :], win_ref[:, c0:c0 + GATE_PIECE])
    pending = [functools.partial(gate_piece, r0, c0)
               for c0 in range(O_OG, N_IN, GATE_PIECE) for r0 in range(0, tile, GATE_ROWS)]
    n_pieces = len(pending)
    n_groups = CHUNK // SUB
    n_slots = n_chunks * (n_groups + HEADS)
    slot = [0]

    def interleave():
        slot[0] += 1
        while len(pending) > n_pieces - (-(-slot[0] * n_pieces // n_slots)):
            pending.pop(0)()

    def gates(c):
        _gates_stage(c * CHUNK, c % 2, z_sc, lb, q_sc, k_sc, fg_sc, g_sc, od_sc, dec_sc)

    def group(c, j):
        _group_stage(j, c % 2, q_sc, k_sc, fg_sc, g_sc, w_sc, qb_sc, kb_sc)
        interleave()

    def head(c, hd):
        _head_stage(c * CHUNK, hd, c % 2, z_sc, lvl, w_sc, qb_sc, kb_sc, od_sc, dec_sc, o_sc, st_sc)
        interleave()

    lvl = _level_map(CHUNK // 2)
    gates(0)
    for j in range(n_groups):
        group(0, j)
    for c in range(n_chunks):
        if c + 1 < n_chunks:
            gates(c + 1)
        for hd in range(HEADS):
            head(c, hd)
            if c + 1 < n_chunks:
                for j in range(hd * n_groups // HEADS, (hd + 1) * n_groups // HEADS):
                    group(c + 1, j)
    assert not pending and slot[0] == n_slots

    gon = gon_ref[...]
    o_parts = []
    for hd in range(HEADS):
        sl = slice(hd * DV, (hd + 1) * DV)
        og = z_sc[:, O_OG + hd * DV:O_OG + (hd + 1) * DV]
        o_parts.append(_rms(o_sc[:, sl], gon) * _silu(og))
    yb = _dot(jnp.concatenate(o_parts, axis=-1).astype(BF16), wb_ref[...])
    m = (_twice_sigmoid(z_sc[:, O_GA:O_GA + D_MODEL]) * ya_sc[...]
         + _twice_sigmoid(z_sc[:, O_GB:O_GB + D_MODEL]) * yb)
    x1_ref[0] = x_ref[0] + _dot(m.astype(BF16), wout_ref[...])

    for cp in cast_in:
        cp.wait()

    @pl.when(step > 0)
    def _drain_previous():
        for cp in cast_out(step - 1):
            cp.wait()

    up16[...] = up32[...].astype(BF16)
    dn16[...] = dn32[...].astype(BF16)
    for cp in cast_out(step):
        cp.start()

    @pl.when(step == last_step)
    def _drain_last():
        for cp in cast_out(step):
            cp.wait()
        for cp in _export_copies(mixer_weights, exports, export_sems):
            cp.wait()

    @pl.when(t == nt - 1)
    def _state_out():
        pool_ref[0] = ubuf[1:POOL_MAXW, :]
        for hd in range(HEADS):
            hgrn_ref[0, hd] = st_sc[hd].T


def _const_spec(shape):
    nd = len(shape)
    return pl.BlockSpec(shape, lambda *_: (0,) * nd, pipeline_mode=pl.Buffered(1))


def _prompt_mixer(x, meta, gmix, wpool, pscale, lbl, gon, cscale, win, wa, wb, wout, wup, wdown, *, tile):
    B, L, _ = x.shape
    assert L % tile == 0 and tile % CHUNK == 0 and tile % GATE_ROWS == 0
    nt = L // tile
    n_steps = B * nt
    assert wup.shape[0] % (SUB * n_steps) == 0 and wdown.shape[0] % (SUB * n_steps) == 0
    up_rows, dn_rows = wup.shape[0] // n_steps, wdown.shape[0] // n_steps
    own = (win, wa, wb, wout)
    assert all(w.shape[0] % STAGE_ROWS == 0 and w.shape[1] <= N_IN for w in own)
    kern = functools.partial(_prompt_mixer_kernel, tile=tile)
    return pl.pallas_call(
        kern,
        grid=(B, nt),
        in_specs=[
            pl.BlockSpec((1, tile, D_MODEL), lambda b, t: (b, t, 0)),
            _const_spec((N_META, D_MODEL)),
            _const_spec((1, D_MODEL)),
            _const_spec((len(POOL_WINDOWS) // 2, 2 * POOL_GROUP, 2 * POOL_GROUP)),
            _const_spec((1, POOL_WIDTH)),
            _const_spec((2, KTOT)),
            _const_spec((1, DV)),
            _const_spec((1, N_IN)),
        ] + [pl.BlockSpec(memory_space=pl.ANY)] * 6,
        out_specs=[
            pl.BlockSpec((1, tile, D_MODEL), lambda b, t: (b, t, 0)),
            pl.BlockSpec((1, POOL_BUF, POOL_WIDTH), lambda b, t: (b, 0, 0)),
            pl.BlockSpec((1, HEADS, DK, DV), lambda b, t: (b, 0, 0, 0)),
        ] + [pl.BlockSpec(memory_space=pl.ANY)] * 6,
        out_shape=[
            jax.ShapeDtypeStruct((B, L, D_MODEL), F32),
            jax.ShapeDtypeStruct((B, POOL_BUF, POOL_WIDTH), F32),
            jax.ShapeDtypeStruct((B, HEADS, DK, DV), F32),
        ] + [jax.ShapeDtypeStruct(w.shape, BF16) for w in own + (wup, wdown)],
        scratch_shapes=[
            pltpu.VMEM((tile, N_IN), F32),
            pltpu.VMEM((tile, D_MODEL), BF16),
            pltpu.VMEM((tile, D_MODEL), F32),
            pltpu.VMEM((tile + POOL_MAXW, POOL_WIDTH), F32),
            pltpu.VMEM((CHUNK, KTOT), F32),
            pltpu.VMEM((CHUNK, KTOT), F32),
            pltpu.VMEM((CHUNK, KTOT), F32),
            pltpu.VMEM((CHUNK, KTOT), F32),
            pltpu.VMEM((2, CHUNK, HWIDTH), F32),
            pltpu.VMEM((2, 8, KTOT), F32),
            pltpu.VMEM((2, len(LEVEL_HALVES), CHUNK, KTOT), BF16),
            pltpu.VMEM((2, CHUNK, KTOT), BF16),
            pltpu.VMEM((2, CHUNK, KTOT), BF16),
            pltpu.VMEM((tile, HWIDTH), F32),
            pltpu.VMEM((HEADS, DV, DK), F32),
            pltpu.VMEM((HEADS, DV, DK), F32),
            pltpu.VMEM((N_META, POOL_WIDTH), F32),
            pltpu.VMEM((up_rows, wup.shape[1]), F32),
            pltpu.VMEM((dn_rows, wdown.shape[1]), F32),
            pltpu.VMEM((up_rows, wup.shape[1]), BF16),
            pltpu.VMEM((dn_rows, wdown.shape[1]), BF16),
            pltpu.SemaphoreType.DMA((4,)),
        ] + [pltpu.VMEM(w.shape, BF16) for w in own] + [
            pltpu.VMEM((2, STAGE_ROWS, N_IN), F32),
            pltpu.SemaphoreType.DMA((2,)),
            pltpu.SemaphoreType.DMA((len(own),)),
        ],
        compiler_params=pltpu.CompilerParams(
            dimension_semantics=("arbitrary", "arbitrary"),
            vmem_limit_bytes=VMEM_LIMIT),
        name="prompt_mixer",
    )(x, meta, gmix, wpool, pscale, lbl, gon, cscale, win, wa, wb, wout, wup, wdown)


def _decode_mixer_kernel(x_ref, poolT_ref, s_ref, gmix_ref, win_ref, wpool_ref, pscale_ref,
                         lbl_ref, gon_ref, wa_ref, wb_ref, wout_ref,
                         x1_ref, unew_ref, snew_ref,
                         z_sc, o_sc, *, ts):
    i = pl.program_id(0)
    n = pl.num_programs(0)
    lb = _lower_bound(lbl_ref[...])

    @pl.when(i == 0)
    def _in_proj():
        h = _rms(x_ref[...], gmix_ref[...]).astype(BF16)
        z_sc[...] = _dot(h, win_ref[...])

    r = pl.ds(pl.multiple_of(i * ts, ts), ts)
    q, fg = _gates(z_sc[r, O_Q:O_Q + KTOT], z_sc[r, O_F:O_F + KTOT], lb)
    k = 1.0 - fg
    v = z_sc[r, O_V:O_V + HWIDTH]
    row = lax.broadcasted_iota(jnp.int32, (ts, 1), 0)
    for hd in range(HEADS):
        sl = slice(hd * DK, (hd + 1) * DK)
        fT = fg[:, sl].T
        kT = k[:, sl].T
        qf = (q[:, sl] * fg[:, sl]).astype(BF16)
        qk = jnp.sum(q[:, sl] * k[:, sl], axis=-1, keepdims=True)
        o = qk * v[:, sl]
        for s in range(ts):
            s_old = s_ref[s, hd]
            snew_ref[s, hd] = fT[:, s:s + 1] * s_old + kT[:, s:s + 1] * v[s:s + 1, sl]
            o_row = _dot(qf[s:s + 1, :], s_old.astype(BF16))
            o = o + jnp.where(row == s, o_row, 0.0)
        o_sc[r, sl] = o

    @pl.when(i == n - 1)
    def _out_proj():
        x = x_ref[...]
        u = z_sc[:, O_U:O_U + POOL_WIDTH]
        unew_ref[...] = u
        ya_parts = []
        for gi, w in enumerate(POOL_WINDOWS):
            cs = slice(gi * POOL_GROUP, (gi + 1) * POOL_GROUP)
            wsum = u[:, cs]
            for j in range(1, w):
                wsum = wsum + poolT_ref[POOL_BUF - j, :, cs]
            pooled = wsum / float(w) - u[:, cs]
            ya_parts.append(pooled.astype(BF16))
        ya_pre = _group_maps(ya_parts, wpool_ref) * pscale_ref[...]
        ya = _dot(ya_pre.astype(BF16), wa_ref[...])
        gon = gon_ref[...]
        o_parts = []
        for hd in range(HEADS):
            sl = slice(hd * DV, (hd + 1) * DV)
            og = z_sc[:, O_OG + hd * DV:O_OG + (hd + 1) * DV]
            o_parts.append(_rms(o_sc[:, sl], gon) * _silu(og))
        yb = _dot(jnp.concatenate(o_parts, axis=-1).astype(BF16), wb_ref[...])
        m = (_twice_sigmoid(z_sc[:, O_GA:O_GA + D_MODEL]) * ya
             + _twice_sigmoid(z_sc[:, O_GB:O_GB + D_MODEL]) * yb)
        x1_ref[...] = x + _dot(m.astype(BF16), wout_ref[...])


def _decode_mixer(x, poolT, state, gmix, win, wpool, pscale, lbl, gon, wa, wb, wout, *, ts):
    n = x.shape[0]
    assert n % ts == 0
    kern = functools.partial(_decode_mixer_kernel, ts=ts)
    return pl.pallas_call(
        kern,
        grid=(n // ts,),
        in_specs=[
            _const_spec((n, D_MODEL)),
            _const_spec((POOL_BUF, n, POOL_WIDTH)),
            pl.BlockSpec((ts, HEADS, DK, DV), lambda i: (i, 0, 0, 0)),
            _const_spec((1, D_MODEL)),
            _const_spec((D_MODEL, N_IN)),
            _const_spec((len(POOL_WINDOWS) // 2, 2 * POOL_GROUP, 2 * POOL_GROUP)),
            _const_spec((1, POOL_WIDTH)),
            _const_spec((2, KTOT)),
            _const_spec((1, DV)),
            _const_spec((POOL_WIDTH, D_MODEL)),
            _const_spec((HWIDTH, D_MODEL)),
            _const_spec((D_MODEL, D_MODEL)),
        ],
        out_specs=[
            pl.BlockSpec((n, D_MODEL), lambda i: (0, 0)),
            pl.BlockSpec((n, POOL_WIDTH), lambda i: (0, 0)),
            pl.BlockSpec((ts, HEADS, DK, DV), lambda i: (i, 0, 0, 0)),
        ],
        out_shape=[
            jax.ShapeDtypeStruct((n, D_MODEL), F32),
            jax.ShapeDtypeStruct((n, POOL_WIDTH), F32),
            jax.ShapeDtypeStruct((n, HEADS, DK, DV), F32),
        ],
        scratch_shapes=[
            pltpu.VMEM((n, N_IN), F32),
            pltpu.VMEM((n, HWIDTH), F32),
        ],
        compiler_params=pltpu.CompilerParams(
            dimension_semantics=("arbitrary",),
            vmem_limit_bytes=VMEM_LIMIT),
        name="decode_mixer",
    )(x, poolT, state, gmix, win, wpool, pscale, lbl, gon, wa, wb, wout)


def _mlp_kernel(x_ref, xs_ref, gmlp_ref, wup_ref, wdown_ref, gfin_ref, y_ref, ys_ref, *, ff_chunk):
    i = pl.program_id(0)
    last = pl.num_programs(0) - 1

    def rows(src_ref, dst_ref):
        x = src_ref[...]
        h = _rms(x, gmlp_ref[...]).astype(BF16)
        acc = x
        for c in range(D_FF // ff_chunk):
            cs = slice(c * ff_chunk, (c + 1) * ff_chunk)
            a = jnp.maximum(_dot(h, wup_ref[:, cs]), 0.0)
            acc = acc + _dot((a * a).astype(BF16), wdown_ref[cs, :])
        dst_ref[...] = _rms(acc, gfin_ref[...])

    @pl.when(i < last)
    def _prompt_tile():
        rows(x_ref, y_ref)

    @pl.when(i == last)
    def _sample_rows():
        rows(xs_ref, ys_ref)


def _mlp(x, xs, gmlp, wup, wdown, gfin, *, tm, ff_chunk=1024):
    n, ns = x.shape[0], xs.shape[0]
    assert n % tm == 0 and D_FF % ff_chunk == 0
    n_tiles = n // tm
    kern = functools.partial(_mlp_kernel, ff_chunk=ff_chunk)

    def tile(i):
        return jnp.minimum(i, n_tiles - 1), 0

    return pl.pallas_call(
        kern,
        grid=(n_tiles + 1,),
        in_specs=[
            pl.BlockSpec((tm, D_MODEL), tile),
            _const_spec((ns, D_MODEL)),
            _const_spec((1, D_MODEL)),
            _const_spec((D_MODEL, D_FF)),
            _const_spec((D_FF, D_MODEL)),
            _const_spec((1, D_MODEL)),
        ],
        out_specs=[
            pl.BlockSpec((tm, D_MODEL), tile),
            pl.BlockSpec((ns, D_MODEL), lambda i: (0, 0)),
        ],
        out_shape=[
            jax.ShapeDtypeStruct((n, D_MODEL), F32),
            jax.ShapeDtypeStruct((ns, D_MODEL), F32),
        ],
        compiler_params=pltpu.CompilerParams(
            dimension_semantics=("arbitrary",),
            vmem_limit_bytes=VMEM_LIMIT),
        name="channel_mlp",
    )(x, xs, gmlp, wup, wdown, gfin)


def kernel(x_prompt, x_sample, state_pool, state_hgrn, meta_tokens, g_mix, w_in, w_pool, pool_scale,
           hgrn_lb_logits, g_onorm, w_a, w_b, w_out, g_mlp, w_up, w_down, g_final):
    B, L, _ = x_prompt.shape
    NS = x_sample.shape[0]
    assert g_mix.shape[0] == 1, "single-layer trunk"
    gmix = g_mix[0][None, :]
    col = jnp.arange(N_IN)
    gate_col = ((col >= O_Q) & (col < O_V)) | (col >= O_OG)
    cscale = jnp.where(gate_col, 0.5, 1.0).astype(F32)[None, :]
    wp = w_pool[0].astype(BF16).reshape(len(POOL_WINDOWS) // 2, 2, POOL_GROUP, POOL_GROUP)
    zp = jnp.zeros_like(wp[:, 0])
    wpool = jnp.concatenate([jnp.concatenate([wp[:, 0], zp], axis=2),
                             jnp.concatenate([zp, wp[:, 1]], axis=2)], axis=1)
    pscale = pool_scale[0][None, :]
    gon = g_onorm[0][None, :]
    gmlp = g_mlp[0][None, :]
    gfin = g_final[None, :]

    x1_p, pool_p, hgrn_p, win, wa, wb, wout, wup, wdown = _prompt_mixer(
        x_prompt, meta_tokens, gmix, wpool, pscale, hgrn_lb_logits, gon, cscale,
        w_in[0], w_a[0], w_b[0], w_out[0], w_up[0], w_down[0], tile=MIXER_TILE)

    xs = x_sample.reshape(NS, D_MODEL)
    poolT = jnp.swapaxes(state_pool[0], 0, 1)
    x1_s, u_s, hgrn_s = _decode_mixer(
        xs, poolT, state_hgrn[0], gmix, win, wpool, pscale, hgrn_lb_logits, gon, wa, wb, wout,
        ts=DECODE_TILE)

    y_p, y_s = _mlp(x1_p.reshape(B * L, D_MODEL), x1_s, gmlp, wup, wdown, gfin, tm=MLP_TILE)
    pool_s = jnp.concatenate([state_pool[0][:, 1:, :], u_s[:, None, :]], axis=1)

    return (y_p.reshape(B, L, D_MODEL), y_s.reshape(NS, 1, D_MODEL),
            pool_p[None], hgrn_p[None], pool_s[None], hgrn_s[None])
```

```python
import functools

import jax
import jax.numpy as jnp
from jax import lax
from jax.experimental import pallas as pl
from jax.experimental.pallas import tpu as pltpu

D_MODEL = 1024
N_META = 16
POOL_WIDTH = 512
POOL_WINDOWS = (2, 4, 8, 16)
POOL_GROUP = 128
POOL_MAXW = 16
POOL_BUF = 15
HEADS = 4
DK = 128
DV = 128
KTOT = 512
HWIDTH = 512
D_FF = 4096
EPS = 1e-6
N_IN = 4608
O_U, O_Q, O_F, O_V, O_OG, O_GA, O_GB = 0, 512, 1024, 1536, 2048, 2560, 3584

SUB = 16
CHUNK = 256
GATE_PIECE = 256
GATE_ROWS = 512
STAGE_ROWS = 256
LEVEL_HALVES = tuple(1 << i for i in range(CHUNK.bit_length() - 1))

V7X_VMEM_BYTES = 64 * 1024 * 1024
VMEM_LIMIT = V7X_VMEM_BYTES - 8 * 1024 * 1024
MIXER_TILE = 2 * CHUNK
MLP_TILE = 512
DECODE_TILE = 2 * SUB

F32 = jnp.float32
BF16 = jnp.bfloat16
F32_TINY = 1.1754944e-38


def _rms(x, g):
    return x * lax.rsqrt(jnp.mean(x * x, axis=-1, keepdims=True) + EPS) * g


def _twice_sigmoid(hx):
    return jnp.tanh(hx) + 1.0


def _silu(hx):
    return hx * jnp.tanh(hx) + hx


def _dot(a, b):
    return jnp.dot(a, b, preferred_element_type=F32)


def _dot_nt(a, b):
    return lax.dot_general(a, b, (((1,), (1,)), ((), ())), preferred_element_type=F32)


def _dot_tn(a, b):
    return lax.dot_general(a, b, (((0,), (0,)), ((), ())), preferred_element_type=F32)


def _group_maps(pooled_groups, wpool_ref):
    outs = []
    for p in range(len(pooled_groups) // 2):
        pair = jnp.concatenate(pooled_groups[2 * p:2 * p + 2], axis=-1)
        outs.append(_dot(pair, wpool_ref[p]))
    return jnp.concatenate(outs, axis=-1)


def _lower_bound(lb_logits):
    m = jnp.max(lb_logits, axis=0, keepdims=True)
    e = jnp.exp(lb_logits - m)
    return e[0:1, :] / jnp.sum(e, axis=0, keepdims=True)


def _gates(hq, hf, lb):
    q = _silu(hq)
    fg = 0.5 * (1.0 + lb) + (0.5 * (1.0 - lb)) * jnp.tanh(hf)
    return q, jnp.maximum(fg, F32_TINY)


def _cumsum_rows(g):
    n = g.shape[0]
    r = lax.broadcasted_iota(jnp.int32, (n, n), 0)
    c = lax.broadcasted_iota(jnp.int32, (n, n), 1)
    tril = (r >= c).astype(BF16)
    hi = g.astype(BF16)
    r1 = g - hi.astype(F32)
    mid = r1.astype(BF16)
    lo = (r1 - mid.astype(F32)).astype(BF16)
    return _dot(tril, hi) + _dot(tril, mid) + _dot(tril, lo)


def _level_map(n):
    t = lax.broadcasted_iota(jnp.int32, (n, n), 0)
    s = lax.broadcasted_iota(jnp.int32, (n, n), 1)
    x = t ^ s
    lvl = jnp.full((n, n), -1, jnp.int32)
    for li in range(n.bit_length() - 1):
        lvl = jnp.where((x >> li) == 1, li, lvl)
    return jnp.where(t > s, lvl, -1)


def _level_weights(m, r0, Gg, qg, kg, fgg, g_sc, row8, row16):
    if m == 1:
        return jnp.where((row16 & 1) != 0, qg * fgg, kg)
    if m >= SUB:
        blk = (r0 // (2 * m)) * (2 * m)
        ref = g_sc[blk + m - 1:blk + m, :]
        return (jnp.exp2(Gg - ref) * qg) if (r0 & m) else (jnp.exp2(ref - Gg) * kg)
    if m == SUB // 2:
        ref = g_sc[r0 + m - 1:r0 + m, :]
        return jnp.concatenate([jnp.exp2(ref - Gg[:m]) * kg[:m], jnp.exp2(Gg[m:] - ref) * qg[m:]], axis=0)
    halves = []
    for rb in (r0, r0 + 8):
        if m == 4:
            halves.append(jnp.broadcast_to(g_sc[rb + 3:rb + 4, :], (8, KTOT)))
        else:
            halves.append(jnp.where(row8 >= 4, g_sc[rb + 5:rb + 6, :], g_sc[rb + 1:rb + 2, :]))
    ref = jnp.concatenate(halves, axis=0)
    sel = jnp.where((row16 & m) != 0, qg, kg)
    return jnp.exp2(-jnp.abs(Gg - ref)) * sel


def _gates_stage(base, par, z_sc, lb, q_sc, k_sc, fg_sc, g_sc, od_sc, dec_sc):
    rs = slice(base, base + CHUNK)
    q, fg = _gates(z_sc[rs, O_Q:O_Q + KTOT], z_sc[rs, O_F:O_F + KTOT], lb)
    k = 1.0 - fg
    G = _cumsum_rows(jnp.log2(fg))
    q_sc[...] = q
    k_sc[...] = k
    fg_sc[...] = fg
    g_sc[...] = G
    v = z_sc[rs, O_V:O_V + HWIDTH]
    qk = q * k
    for hd in range(HEADS):
        sl = slice(hd * DK, (hd + 1) * DK)
        od_sc[par, :, sl] = jnp.sum(qk[:, sl], axis=-1, keepdims=True) * v[:, sl]
    dec_sc[par] = jnp.broadcast_to(jnp.exp2(G[CHUNK - 1:CHUNK, :]), (8, KTOT))


def _group_stage(j, par, q_sc, k_sc, fg_sc, g_sc, w_sc, qb_sc, kb_sc):
    row8 = lax.broadcasted_iota(jnp.int32, (8, 1), 0)
    row16 = lax.broadcasted_iota(jnp.int32, (SUB, 1), 0)
    r0 = j * SUB
    gs = slice(r0, r0 + SUB)
    Gg, qg, kg, fgg = g_sc[gs, :], q_sc[gs, :], k_sc[gs, :], fg_sc[gs, :]
    g_last = g_sc[CHUNK - 1:CHUNK, :]
    for li, m in enumerate(LEVEL_HALVES):
        w_sc[par, li, gs, :] = _level_weights(m, r0, Gg, qg, kg, fgg, g_sc, row8, row16).astype(BF16)
    qb_sc[par, gs, :] = (qg * jnp.exp2(Gg)).astype(BF16)
    kb_sc[par, gs, :] = (kg * jnp.exp2(g_last - Gg)).astype(BF16)


def _head_stage(base, hd, par, z_sc, lvl, w_sc, qb_sc, kb_sc, od_sc, dec_sc, o_sc, st_sc):
    rs = slice(base, base + CHUNK)
    sl = slice(hd * DK, (hd + 1) * DK)
    half = CHUNK // 2
    a_lo = jnp.zeros((half, half), F32)
    a_hi = jnp.zeros((half, half), F32)
    zero = jnp.zeros((half, DK), BF16)
    for li in range(len(LEVEL_HALVES) - 1):
        w_lo = w_sc[par, li, :half, sl]
        w_hi = w_sc[par, li, half:, sl]
        x = jnp.concatenate([w_lo, w_hi], axis=1)
        y = jnp.concatenate([jnp.concatenate([w_lo.T, zero], axis=1),
                             jnp.concatenate([zero, w_hi.T], axis=1)], axis=0)
        p = _dot(x, y)
        a_lo = jnp.where(lvl == li, p[:, :half], a_lo)
        a_hi = jnp.where(lvl == li, p[:, half:], a_hi)
    top = len(LEVEL_HALVES) - 1
    a_x = _dot_nt(w_sc[par, top, half:, sl], w_sc[par, top, :half, sl])
    A = jnp.concatenate([jnp.concatenate([a_lo, jnp.zeros((half, half), F32)], axis=1),
                         jnp.concatenate([a_x, a_hi], axis=1)], axis=0)
    v_bf = z_sc[rs, O_V + hd * DV:O_V + (hd + 1) * DV].astype(BF16)
    st = st_sc[hd]
    o_sc[rs, sl] = (_dot(A.astype(BF16), v_bf)
                    + _dot_nt(qb_sc[par, :, sl], st.astype(BF16))
                    + od_sc[par, :, sl])
    st_sc[hd] = st * dec_sc[par, 0:1, sl] + _dot_tn(v_bf, kb_sc[par, :, sl])


def _slice_copies(step, hbm_refs, vmem_refs, sems, to_hbm):
    copies = []
    for hbm, vmem, sem in zip(hbm_refs, vmem_refs, sems):
        rows = vmem.shape[0]
        window = hbm.at[pl.ds(pl.multiple_of(step * rows, rows), rows), :]
        copies.append(pltpu.make_async_copy(vmem, window, sem) if to_hbm
                      else pltpu.make_async_copy(window, vmem, sem))
    return copies


def _load_mixer_weights(hbm_refs, scales, vmem_refs, stage, sems):
    jobs = [(src, dst, scale, r0)
            for src, dst, scale in zip(hbm_refs, vmem_refs, scales)
            for r0 in range(0, src.shape[0], STAGE_ROWS)]

    def fetch(i):
        src, _, _, r0 = jobs[i]
        return pltpu.make_async_copy(src.at[r0:r0 + STAGE_ROWS, :],
                                     stage.at[i % 2, :, 0:src.shape[1]], sems.at[i % 2])

    fetch(0).start()
    for i, (src, dst, scale, r0) in enumerate(jobs):
        if i + 1 < len(jobs):
            fetch(i + 1).start()
        fetch(i).wait()
        vals = stage[i % 2, :, 0:src.shape[1]]
        if scale is not None:
            vals = vals * scale
        dst[r0:r0 + STAGE_ROWS, :] = vals.astype(BF16)


def _export_copies(vmem_refs, hbm_refs, sems):
    return [pltpu.make_async_copy(v, h, sems.at[i]) for i, (v, h) in enumerate(zip(vmem_refs, hbm_refs))]


def _prompt_mixer_kernel(x_ref, meta_ref, gmix_ref, wpool_ref, pscale_ref, lbl_ref, gon_ref, cscale_ref,
                         win_hbm, wa_hbm, wb_hbm, wout_hbm, wup_hbm, wdn_hbm,
                         x1_ref, pool_ref, hgrn_ref,
                         win16_hbm, wa16_hbm, wb16_hbm, wout16_hbm, wup16_hbm, wdn16_hbm,
                         z_sc, h_sc, ya_sc, ubuf, q_sc, k_sc, fg_sc, g_sc, od_sc, dec_sc,
                         w_sc, qb_sc, kb_sc, o_sc, st_sc, st_meta,
                         u_meta, up32, dn32, up16, dn16, cast_sems,
                         win_ref, wa_ref, wb_ref, wout_ref, stage, load_sems, export_sems, *, tile):
    b = pl.program_id(0)
    t = pl.program_id(1)
    nt = pl.num_programs(1)
    lb = _lower_bound(lbl_ref[...])
    gmix = gmix_ref[...]
    mixer_weights = (win_ref, wa_ref, wb_ref, wout_ref)
    exports = (win16_hbm, wa16_hbm, wb16_hbm, wout16_hbm)

    step = b * nt + t
    last_step = pl.num_programs(0) * nt - 1
    in_sems, out_sems = (cast_sems.at[0], cast_sems.at[1]), (cast_sems.at[2], cast_sems.at[3])

    def cast_out(s):
        return _slice_copies(s, (wup16_hbm, wdn16_hbm), (up16, dn16), out_sems, to_hbm=True)

    cast_in = _slice_copies(step, (wup_hbm, wdn_hbm), (up32, dn32), in_sems, to_hbm=False)
    for cp in cast_in:
        cp.start()

    @pl.when((b == 0) & (t == 0))
    def _meta():
        _load_mixer_weights((win_hbm, wa_hbm, wb_hbm, wout_hbm), (cscale_ref[...], None, None, 0.5),
                            mixer_weights, stage, load_sems)
        for cp in _export_copies(mixer_weights, exports, export_sems):
            cp.start()
        hm = _rms(meta_ref[...], gmix).astype(BF16)
        zm = _dot(hm, win_ref[:, 0:O_OG])
        u_meta[...] = zm[:, O_U:O_U + POOL_WIDTH]
        _, fg = _gates(zm[:, O_Q:O_Q + KTOT], zm[:, O_F:O_F + KTOT], lb)
        G = _cumsum_rows(jnp.log2(fg))
        kt = (1.0 - fg) * jnp.exp2(G[N_META - 1:N_META, :] - G)
        v = zm[:, O_V:O_V + HWIDTH]
        for h in range(HEADS):
            sl = slice(h * DK, (h + 1) * DK)
            st_meta[h] = _dot_tn(v[:, sl].astype(BF16), kt[:, sl].astype(BF16))

    @pl.when(t == 0)
    def _init():
        st_sc[...] = st_meta[...]
        ubuf[0:POOL_MAXW, :] = u_meta[...]

    n_chunks = tile // CHUNK
    h_sc[...] = _rms(x_ref[0], gmix).astype(BF16)
    z_sc[:, 0:O_OG] = _dot(h_sc[...], win_ref[:, 0:O_OG])

    u = z_sc[:, O_U:O_U + POOL_WIDTH]
    ubuf[POOL_MAXW:POOL_MAXW + tile, :] = u
    ya_parts = []
    for gi, w in enumerate(POOL_WINDOWS):
        cs = slice(gi * POOL_GROUP, (gi + 1) * POOL_GROUP)
        wsum = ubuf[:, cs]
        span = 1
        while span < w:
            wsum = wsum + pltpu.roll(wsum, span, 0)
            span *= 2
        pooled = wsum[POOL_MAXW:, :] / float(w) - u[:, cs]
        ya_parts.append(pooled.astype(BF16))
    ya_pre = _group_maps(ya_parts, wpool_ref) * pscale_ref[...]
    ya_sc[...] = _dot(ya_pre.astype(BF16), wa_ref[...])
    ubuf[0:POOL_MAXW, :] = ubuf[tile:tile + POOL_MAXW, :]

    def gate_piece(r0, c0):
        z_sc[r0:r0 + GATE_ROWS, c0:c0 + GATE_PIECE] = _dot(
            h_sc[r0:r0 + GATE_ROWS, :], win_ref[:, c0:c0 + GATE_PIECE])
    pending = [functools.partial(gate_piece, r0, c0)
               for c0 in range(O_OG, N_IN, GATE_PIECE) for r0 in range(0, tile, GATE_ROWS)]
    n_pieces = len(pending)
    n_groups = CHUNK // SUB
    n_slots = n_chunks * (n_groups + HEADS)
    slot = [0]

    def interleave():
        slot[0] += 1
        while len(pending) > n_pieces - (-(-slot[0] * n_pieces // n_slots)):
            pending.pop(0)()

    def gates(c):
        _gates_stage(c * CHUNK, c % 2, z_sc, lb, q_sc, k_sc, fg_sc, g_sc, od_sc, dec_sc)

    def group(c, j):
        _group_stage(j, c % 2, q_sc, k_sc, fg_sc, g_sc, w_sc, qb_sc, kb_sc)
        interleave()

    def head(c, hd):
        _head_stage(c * CHUNK, hd, c % 2, z_sc, lvl, w_sc, qb_sc, kb_sc, od_sc, dec_sc, o_sc, st_sc)
        interleave()

    lvl = _level_map(CHUNK // 2)
    gates(0)
    for j in range(n_groups):
        group(0, j)
    for c in range(n_chunks):
        if c + 1 < n_chunks:
            gates(c + 1)
        for hd in range(HEADS):
            head(c, hd)
            if c + 1 < n_chunks:
                for j in range(hd * n_groups // HEADS, (hd + 1) * n_groups // HEADS):
                    group(c + 1, j)
    assert not pending and slot[0] == n_slots

    gon = gon_ref[...]
    o_parts = []
    for hd in range(HEADS):
        sl = slice(hd * DV, (hd + 1) * DV)
        og = z_sc[:, O_OG + hd * DV:O_OG + (hd + 1) * DV]
        o_parts.append(_rms(o_sc[:, sl], gon) * _silu(og))
    yb = _dot(jnp.concatenate(o_parts, axis=-1).astype(BF16), wb_ref[...])
    m = (_twice_sigmoid(z_sc[:, O_GA:O_GA + D_MODEL]) * ya_sc[...]
         + _twice_sigmoid(z_sc[:, O_GB:O_GB + D_MODEL]) * yb)
    x1_ref[0] = x_ref[0] + _dot(m.astype(BF16), wout_ref[...])

    for cp in cast_in:
        cp.wait()

    @pl.when(step > 0)
    def _drain_previous():
        for cp in cast_out(step - 1):
            cp.wait()

    up16[...] = up32[...].astype(BF16)
    dn16[...] = dn32[...].astype(BF16)
    for cp in cast_out(step):
        cp.start()

    @pl.when(step == last_step)
    def _drain_last():
        for cp in cast_out(step):
            cp.wait()
        for cp in _export_copies(mixer_weights, exports, export_sems):
            cp.wait()

    @pl.when(t == nt - 1)
    def _state_out():
        pool_ref[0] = ubuf[1:POOL_MAXW, :]
        for hd in range(HEADS):
            hgrn_ref[0, hd] = st_sc[hd].T


def _const_spec(shape):
    nd = len(shape)
    return pl.BlockSpec(shape, lambda *_: (0,) * nd, pipeline_mode=pl.Buffered(1))


def _prompt_mixer(x, meta, gmix, wpool, pscale, lbl, gon, cscale, win, wa, wb, wout, wup, wdown, *, tile):
    B, L, _ = x.shape
    assert L % tile == 0 and tile % CHUNK == 0 and tile % GATE_ROWS == 0
    nt = L // tile
    n_steps = B * nt
    assert wup.shape[0] % (SUB * n_steps) == 0 and wdown.shape[0] % (SUB * n_steps) == 0
    up_rows, dn_rows = wup.shape[0] // n_steps, wdown.shape[0] // n_steps
    own = (win, wa, wb, wout)
    assert all(w.shape[0] % STAGE_ROWS == 0 and w.shape[1] <= N_IN for w in own)
    kern = functools.partial(_prompt_mixer_kernel, tile=tile)
    return pl.pallas_call(
        kern,
        grid=(B, nt),
        in_specs=[
            pl.BlockSpec((1, tile, D_MODEL), lambda b, t: (b, t, 0)),
            _const_spec((N_META, D_MODEL)),
            _const_spec((1, D_MODEL)),
            _const_spec((len(POOL_WINDOWS) // 2, 2 * POOL_GROUP, 2 * POOL_GROUP)),
            _const_spec((1, POOL_WIDTH)),
            _const_spec((2, KTOT)),
            _const_spec((1, DV)),
            _const_spec((1, N_IN)),
        ] + [pl.BlockSpec(memory_space=pl.ANY)] * 6,
        out_specs=[
            pl.BlockSpec((1, tile, D_MODEL), lambda b, t: (b, t, 0)),
            pl.BlockSpec((1, POOL_BUF, POOL_WIDTH), lambda b, t: (b, 0, 0)),
            pl.BlockSpec((1, HEADS, DK, DV), lambda b, t: (b, 0, 0, 0)),
        ] + [pl.BlockSpec(memory_space=pl.ANY)] * 6,
        out_shape=[
            jax.ShapeDtypeStruct((B, L, D_MODEL), F32),
            jax.ShapeDtypeStruct((B, POOL_BUF, POOL_WIDTH), F32),
            jax.ShapeDtypeStruct((B, HEADS, DK, DV), F32),
        ] + [jax.ShapeDtypeStruct(w.shape, BF16) for w in own + (wup, wdown)],
        scratch_shapes=[
            pltpu.VMEM((tile, N_IN), F32),
            pltpu.VMEM((tile, D_MODEL), BF16),
            pltpu.VMEM((tile, D_MODEL), F32),
            pltpu.VMEM((tile + POOL_MAXW, POOL_WIDTH), F32),
            pltpu.VMEM((CHUNK, KTOT), F32),
            pltpu.VMEM((CHUNK, KTOT), F32),
            pltpu.VMEM((CHUNK, KTOT), F32),
            pltpu.VMEM((CHUNK, KTOT), F32),
            pltpu.VMEM((2, CHUNK, HWIDTH), F32),
            pltpu.VMEM((2, 8, KTOT), F32),
            pltpu.VMEM((2, len(LEVEL_HALVES), CHUNK, KTOT), BF16),
            pltpu.VMEM((2, CHUNK, KTOT), BF16),
            pltpu.VMEM((2, CHUNK, KTOT), BF16),
            pltpu.VMEM((tile, HWIDTH), F32),
            pltpu.VMEM((HEADS, DV, DK), F32),
            pltpu.VMEM((HEADS, DV, DK), F32),
            pltpu.VMEM((N_META, POOL_WIDTH), F32),
            pltpu.VMEM((up_rows, wup.shape[1]), F32),
            pltpu.VMEM((dn_rows, wdown.shape[1]), F32),
            pltpu.VMEM((up_rows, wup.shape[1]), BF16),
            pltpu.VMEM((dn_rows, wdown.shape[1]), BF16),
            pltpu.SemaphoreType.DMA((4,)),
        ] + [pltpu.VMEM(w.shape, BF16) for w in own] + [
            pltpu.VMEM((2, STAGE_ROWS, N_IN), F32),
            pltpu.SemaphoreType.DMA((2,)),
            pltpu.SemaphoreType.DMA((len(own),)),
        ],
        compiler_params=pltpu.CompilerParams(
            dimension_semantics=("arbitrary", "arbitrary"),
            vmem_limit_bytes=VMEM_LIMIT),
        name="prompt_mixer",
    )(x, meta, gmix, wpool, pscale, lbl, gon, cscale, win, wa, wb, wout, wup, wdown)


def _decode_mixer_kernel(x_ref, poolT_ref, s_ref, gmix_ref, win_ref, wpool_ref, pscale_ref,
                         lbl_ref, gon_ref, wa_ref, wb_ref, wout_ref,
                         x1_ref, unew_ref, snew_ref,
                         z_sc, o_sc, *, ts):
    i = pl.program_id(0)
    n = pl.num_programs(0)
    lb = _lower_bound(lbl_ref[...])

    @pl.when(i == 0)
    def _in_proj():
        h = _rms(x_ref[...], gmix_ref[...]).astype(BF16)
        z_sc[...] = _dot(h, win_ref[...])

    r = pl.ds(pl.multiple_of(i * ts, ts), ts)
    q, fg = _gates(z_sc[r, O_Q:O_Q + KTOT], z_sc[r, O_F:O_F + KTOT], lb)
    k = 1.0 - fg
    v = z_sc[r, O_V:O_V + HWIDTH]
    row = lax.broadcasted_iota(jnp.int32, (ts, 1), 0)
    for hd in range(HEADS):
        sl = slice(hd * DK, (hd + 1) * DK)
        fT = fg[:, sl].T
        kT = k[:, sl].T
        qf = (q[:, sl] * fg[:, sl]).astype(BF16)
        qk = jnp.sum(q[:, sl] * k[:, sl], axis=-1, keepdims=True)
        o = qk * v[:, sl]
        for s in range(ts):
            s_old = s_ref[s, hd]
            snew_ref[s, hd] = fT[:, s:s + 1] * s_old + kT[:, s:s + 1] * v[s:s + 1, sl]
            o_row = _dot(qf[s:s + 1, :], s_old.astype(BF16))
            o = o + jnp.where(row == s, o_row, 0.0)
        o_sc[r, sl] = o

    @pl.when(i == n - 1)
    def _out_proj():
        x = x_ref[...]
        u = z_sc[:, O_U:O_U + POOL_WIDTH]
        unew_ref[...] = u
        ya_parts = []
        for gi, w in enumerate(POOL_WINDOWS):
            cs = slice(gi * POOL_GROUP, (gi + 1) * POOL_GROUP)
            wsum = u[:, cs]
            for j in range(1, w):
                wsum = wsum + poolT_ref[POOL_BUF - j, :, cs]
            pooled = wsum / float(w) - u[:, cs]
            ya_parts.append(pooled.astype(BF16))
        ya_pre = _group_maps(ya_parts, wpool_ref) * pscale_ref[...]
        ya = _dot(ya_pre.astype(BF16), wa_ref[...])
        gon = gon_ref[...]
        o_parts = []
        for hd in range(HEADS):
            sl = slice(hd * DV, (hd + 1) * DV)
            og = z_sc[:, O_OG + hd * DV:O_OG + (hd + 1) * DV]
            o_parts.append(_rms(o_sc[:, sl], gon) * _silu(og))
        yb = _dot(jnp.concatenate(o_parts, axis=-1).astype(BF16), wb_ref[...])
        m = (_twice_sigmoid(z_sc[:, O_GA:O_GA + D_MODEL]) * ya
             + _twice_sigmoid(z_sc[:, O_GB:O_GB + D_MODEL]) * yb)
        x1_ref[...] = x + _dot(m.astype(BF16), wout_ref[...])


def _decode_mixer(x, poolT, state, gmix, win, wpool, pscale, lbl, gon, wa, wb, wout, *, ts):
    n = x.shape[0]
    assert n % ts == 0
    kern = functools.partial(_decode_mixer_kernel, ts=ts)
    return pl.pallas_call(
        kern,
        grid=(n // ts,),
        in_specs=[
            _const_spec((n, D_MODEL)),
            _const_spec((POOL_BUF, n, POOL_WIDTH)),
            pl.BlockSpec((ts, HEADS, DK, DV), lambda i: (i, 0, 0, 0)),
            _const_spec((1, D_MODEL)),
            _const_spec((D_MODEL, N_IN)),
            _const_spec((len(POOL_WINDOWS) // 2, 2 * POOL_GROUP, 2 * POOL_GROUP)),
            _const_spec((1, POOL_WIDTH)),
            _const_spec((2, KTOT)),
            _const_spec((1, DV)),
            _const_spec((POOL_WIDTH, D_MODEL)),
            _const_spec((HWIDTH, D_MODEL)),
            _const_spec((D_MODEL, D_MODEL)),
        ],
        out_specs=[
            pl.BlockSpec((n, D_MODEL), lambda i: (0, 0)),
            pl.BlockSpec((n, POOL_WIDTH), lambda i: (0, 0)),
            pl.BlockSpec((ts, HEADS, DK, DV), lambda i: (i, 0, 0, 0)),
        ],
        out_shape=[
            jax.ShapeDtypeStruct((n, D_MODEL), F32),
            jax.ShapeDtypeStruct((n, POOL_WIDTH), F32),
            jax.ShapeDtypeStruct((n, HEADS, DK, DV), F32),
        ],
        scratch_shapes=[
            pltpu.VMEM((n, N_IN), F32),
            pltpu.VMEM((n, HWIDTH), F32),
        ],
        compiler_params=pltpu.CompilerParams(
            dimension_semantics=("arbitrary",),
            vmem_limit_bytes=VMEM_LIMIT),
        name="decode_mixer",
    )(x, poolT, state, gmix, win, wpool, pscale, lbl, gon, wa, wb, wout)


def _mlp_kernel(x_ref, xs_ref, gmlp_ref, wup_ref, wdown_ref, gfin_ref, y_ref, ys_ref, *, ff_chunk):
    i = pl.program_id(0)
    last = pl.num_programs(0) - 1

    def rows(src_ref, dst_ref):
        x = src_ref[...]
        h = _rms(x, gmlp_ref[...]).astype(BF16)
        acc = x
        for c in range(D_FF // ff_chunk):
            cs = slice(c * ff_chunk, (c + 1) * ff_chunk)
            a = jnp.maximum(_dot(h, wup_ref[:, cs]), 0.0)
            acc = acc + _dot((a * a).astype(BF16), wdown_ref[cs, :])
        dst_ref[...] = _rms(acc, gfin_ref[...])

    @pl.when(i < last)
    def _prompt_tile():
        rows(x_ref, y_ref)

    @pl.when(i == last)
    def _sample_rows():
        rows(xs_ref, ys_ref)


def _mlp(x, xs, gmlp, wup, wdown, gfin, *, tm, ff_chunk=1024):
    n, ns = x.shape[0], xs.shape[0]
    assert n % tm == 0 and D_FF % ff_chunk == 0
    n_tiles = n // tm
    kern = functools.partial(_mlp_kernel, ff_chunk=ff_chunk)

    def tile(i):
        return jnp.minimum(i, n_tiles - 1), 0

    return pl.pallas_call(
        kern,
        grid=(n_tiles + 1,),
        in_specs=[
            pl.BlockSpec((tm, D_MODEL), tile),
            _const_spec((ns, D_MODEL)),
            _const_spec((1, D_MODEL)),
            _const_spec((D_MODEL, D_FF)),
            _const_spec((D_FF, D_MODEL)),
            _const_spec((1, D_MODEL)),
        ],
        out_specs=[
            pl.BlockSpec((tm, D_MODEL), tile),
            pl.BlockSpec((ns, D_MODEL), lambda i: (0, 0)),
        ],
        out_shape=[
            jax.ShapeDtypeStruct((n, D_MODEL), F32),
            jax.ShapeDtypeStruct((ns, D_MODEL), F32),
        ],
        compiler_params=pltpu.CompilerParams(
            dimension_semantics=("arbitrary",),
            vmem_limit_bytes=VMEM_LIMIT),
        name="channel_mlp",
    )(x, xs, gmlp, wup, wdown, gfin)


def kernel(x_prompt, x_sample, state_pool, state_hgrn, meta_tokens, g_mix, w_in, w_pool, pool_scale,
           hgrn_lb_logits, g_onorm, w_a, w_b, w_out, g_mlp, w_up, w_down, g_final):
    B, L, _ = x_prompt.shape
    NS = x_sample.shape[0]
    assert g_mix.shape[0] == 1, "single-layer trunk"
    gmix = g_mix[0][None, :]
    col = jnp.arange(N_IN)
    gate_col = ((col >= O_Q) & (col < O_V)) | (col >= O_OG)
    cscale = jnp.where(gate_col, 0.5, 1.0).astype(F32)[None, :]
    wp = w_pool[0].astype(BF16).reshape(len(POOL_WINDOWS) // 2, 2, POOL_GROUP, POOL_GROUP)
    zp = jnp.zeros_like(wp[:, 0])
    wpool = jnp.concatenate([jnp.concatenate([wp[:, 0], zp], axis=2),
                             jnp.concatenate([zp, wp[:, 1]], axis=2)], axis=1)
    pscale = pool_scale[0][None, :]
    gon = g_onorm[0][None, :]
    gmlp = g_mlp[0][None, :]
    gfin = g_final[None, :]

    x1_p, pool_p, hgrn_p, win, wa, wb, wout, wup, wdown = _prompt_mixer(
        x_prompt, meta_tokens, gmix, wpool, pscale, hgrn_lb_logits, gon, cscale,
        w_in[0], w_a[0], w_b[0], w_out[0], w_up[0], w_down[0], tile=MIXER_TILE)

    xs = x_sample.reshape(NS, D_MODEL)
    poolT = jnp.swapaxes(state_pool[0], 0, 1)
    x1_s, u_s, hgrn_s = _decode_mixer(
        xs, poolT, state_hgrn[0], gmix, win, wpool, pscale, hgrn_lb_logits, gon, wa, wb, wout,
        ts=DECODE_TILE)

    y_p, y_s = _mlp(x1_p.reshape(B * L, D_MODEL), x1_s, gmlp, wup, wdown, gfin, tm=MLP_TILE)
    pool_s = jnp.concatenate([state_pool[0][:, 1:, :], u_s[:, None, :]], axis=1)

    return (y_p.reshape(B, L, D_MODEL), y_s.reshape(NS, 1, D_MODEL),
            pool_p[None], hgrn_p[None], pool_s[None], hgrn_s[None])
```

```python
import functools

import jax
import jax.numpy as jnp
from jax import lax
from jax.experimental import pallas as pl
from jax.experimental.pallas import tpu as pltpu

D_MODEL = 1024
N_META = 16
POOL_WIDTH = 512
POOL_WINDOWS = (2, 4, 8, 16)
POOL_GROUP = 128
POOL_MAXW = 16
POOL_BUF = 15
HEADS = 4
DK = 128
DV = 128
KTOT = 512
HWIDTH = 512
D_FF = 4096
EPS = 1e-6
N_IN = 4608
O_U, O_Q, O_F, O_V, O_OG, O_GA, O_GB = 0, 512, 1024, 1536, 2048, 2560, 3584

SUB = 16
CHUNK = 256
GATE_PIECE = 256
GATE_ROWS = 512
STAGE_ROWS = 256
LEVEL_HALVES = tuple(1 << i for i in range(CHUNK.bit_length() - 1))

V7X_VMEM_BYTES = 64 * 1024 * 1024
VMEM_LIMIT = V7X_VMEM_BYTES - 8 * 1024 * 1024
MIXER_TILE = 2 * CHUNK
MLP_TILE = 512
DECODE_TILE = SUB

F32 = jnp.float32
BF16 = jnp.bfloat16
F32_TINY = 1.1754944e-38


def _rms(x, g):
    return x * lax.rsqrt(jnp.mean(x * x, axis=-1, keepdims=True) + EPS) * g


def _twice_sigmoid(hx):
    return jnp.tanh(hx) + 1.0


def _silu(hx):
    return hx * jnp.tanh(hx) + hx


def _dot(a, b):
    return jnp.dot(a, b, preferred_element_type=F32)


def _dot_nt(a, b):
    return lax.dot_general(a, b, (((1,), (1,)), ((), ())), preferred_element_type=F32)


def _dot_tn(a, b):
    return lax.dot_general(a, b, (((0,), (0,)), ((), ())), preferred_element_type=F32)


def _group_maps(pooled_groups, wpool_ref):
    outs = []
    for p in range(len(pooled_groups) // 2):
        pair = jnp.concatenate(pooled_groups[2 * p:2 * p + 2], axis=-1)
        outs.append(_dot(pair, wpool_ref[p]))
    return jnp.concatenate(outs, axis=-1)


def _lower_bound(lb_logits):
    m = jnp.max(lb_logits, axis=0, keepdims=True)
    e = jnp.exp(lb_logits - m)
    return e[0:1, :] / jnp.sum(e, axis=0, keepdims=True)


def _gates(hq, hf, lb):
    q = _silu(hq)
    fg = 0.5 * (1.0 + lb) + (0.5 * (1.0 - lb)) * jnp.tanh(hf)
    return q, jnp.maximum(fg, F32_TINY)


def _cumsum_rows(g):
    n = g.shape[0]
    r = lax.broadcasted_iota(jnp.int32, (n, n), 0)
    c = lax.broadcasted_iota(jnp.int32, (n, n), 1)
    tril = (r >= c).astype(BF16)
    hi = g.astype(BF16)
    r1 = g - hi.astype(F32)
    mid = r1.astype(BF16)
    lo = (r1 - mid.astype(F32)).astype(BF16)
    return _dot(tril, hi) + _dot(tril, mid) + _dot(tril, lo)


def _level_map(n):
    t = lax.broadcasted_iota(jnp.int32, (n, n), 0)
    s = lax.broadcasted_iota(jnp.int32, (n, n), 1)
    x = t ^ s
    lvl = jnp.full((n, n), -1, jnp.int32)
    for li in range(n.bit_length() - 1):
        lvl = jnp.where((x >> li) == 1, li, lvl)
    return jnp.where(t > s, lvl, -1)


def _level_weights(m, r0, Gg, qg, kg, fgg, g_sc, row8, row16):
    if m == 1:
        return jnp.where((row16 & 1) != 0, qg * fgg, kg)
    if m >= SUB:
        blk = (r0 // (2 * m)) * (2 * m)
        ref = g_sc[blk + m - 1:blk + m, :]
        return (jnp.exp2(Gg - ref) * qg) if (r0 & m) else (jnp.exp2(ref - Gg) * kg)
    if m == SUB // 2:
        ref = g_sc[r0 + m - 1:r0 + m, :]
        return jnp.concatenate([jnp.exp2(ref - Gg[:m]) * kg[:m], jnp.exp2(Gg[m:] - ref) * qg[m:]], axis=0)
    halves = []
    for rb in (r0, r0 + 8):
        if m == 4:
            halves.append(jnp.broadcast_to(g_sc[rb + 3:rb + 4, :], (8, KTOT)))
        else:
            halves.append(jnp.where(row8 >= 4, g_sc[rb + 5:rb + 6, :], g_sc[rb + 1:rb + 2, :]))
    ref = jnp.concatenate(halves, axis=0)
    sel = jnp.where((row16 & m) != 0, qg, kg)
    return jnp.exp2(-jnp.abs(Gg - ref)) * sel


def _gates_stage(base, par, z_sc, lb, q_sc, k_sc, fg_sc, g_sc, od_sc, dec_sc):
    rs = slice(base, base + CHUNK)
    q, fg = _gates(z_sc[rs, O_Q:O_Q + KTOT], z_sc[rs, O_F:O_F + KTOT], lb)
    k = 1.0 - fg
    G = _cumsum_rows(jnp.log2(fg))
    q_sc[...] = q
    k_sc[...] = k
    fg_sc[...] = fg
    g_sc[...] = G
    v = z_sc[rs, O_V:O_V + HWIDTH]
    qk = q * k
    for hd in range(HEADS):
        sl = slice(hd * DK, (hd + 1) * DK)
        od_sc[par, :, sl] = jnp.sum(qk[:, sl], axis=-1, keepdims=True) * v[:, sl]
    dec_sc[par] = jnp.broadcast_to(jnp.exp2(G[CHUNK - 1:CHUNK, :]), (8, KTOT))


def _group_stage(j, par, q_sc, k_sc, fg_sc, g_sc, w_sc, qb_sc, kb_sc):
    row8 = lax.broadcasted_iota(jnp.int32, (8, 1), 0)
    row16 = lax.broadcasted_iota(jnp.int32, (SUB, 1), 0)
    r0 = j * SUB
    gs = slice(r0, r0 + SUB)
    Gg, qg, kg, fgg = g_sc[gs, :], q_sc[gs, :], k_sc[gs, :], fg_sc[gs, :]
    g_last = g_sc[CHUNK - 1:CHUNK, :]
    for li, m in enumerate(LEVEL_HALVES):
        w_sc[par, li, gs, :] = _level_weights(m, r0, Gg, qg, kg, fgg, g_sc, row8, row16).astype(BF16)
    qb_sc[par, gs, :] = (qg * jnp.exp2(Gg)).astype(BF16)
    kb_sc[par, gs, :] = (kg * jnp.exp2(g_last - Gg)).astype(BF16)


def _head_stage(base, hd, par, z_sc, lvl, w_sc, qb_sc, kb_sc, od_sc, dec_sc, o_sc, st_sc):
    rs = slice(base, base + CHUNK)
    sl = slice(hd * DK, (hd + 1) * DK)
    half = CHUNK // 2
    a_lo = jnp.zeros((half, half), BF16)
    a_hi = jnp.zeros((half, half), BF16)
    zero = jnp.zeros((half, DK), BF16)
    for li in range(len(LEVEL_HALVES) - 1):
        w_lo = w_sc[par, li, :half, sl]
        w_hi = w_sc[par, li, half:, sl]
        x = jnp.concatenate([w_lo, w_hi], axis=1)
        y = jnp.concatenate([jnp.concatenate([w_lo.T, zero], axis=1),
                             jnp.concatenate([zero, w_hi.T], axis=1)], axis=0)
        p = _dot(x, y).astype(BF16)
        a_lo = jnp.where(lvl == li, p[:, :half], a_lo)
        a_hi = jnp.where(lvl == li, p[:, half:], a_hi)
    top = len(LEVEL_HALVES) - 1
    a_x = _dot_nt(w_sc[par, top, half:, sl], w_sc[par, top, :half, sl])
    A = jnp.concatenate([jnp.concatenate([a_lo, jnp.zeros((half, half), BF16)], axis=1),
                         jnp.concatenate([a_x.astype(BF16), a_hi], axis=1)], axis=0)
    v_bf = z_sc[rs, O_V + hd * DV:O_V + (hd + 1) * DV].astype(BF16)
    st = st_sc[hd]
    o_sc[rs, sl] = (_dot(A, v_bf)
                    + _dot_nt(qb_sc[par, :, sl], st.astype(BF16))
                    + od_sc[par, :, sl])
    st_sc[hd] = st * dec_sc[par, 0:1, sl] + _dot_tn(v_bf, kb_sc[par, :, sl])


def _slice_copies(step, hbm_refs, vmem_refs, sems, to_hbm):
    copies = []
    for hbm, vmem, sem in zip(hbm_refs, vmem_refs, sems):
        rows = vmem.shape[0]
        window = hbm.at[pl.ds(pl.multiple_of(step * rows, rows), rows), :]
        copies.append(pltpu.make_async_copy(vmem, window, sem) if to_hbm
                      else pltpu.make_async_copy(window, vmem, sem))
    return copies


def _load_mixer_weights(hbm_refs, scales, vmem_refs, stage, sems):
    jobs = [(src, dst, scale, r0)
            for src, dst, scale in zip(hbm_refs, vmem_refs, scales)
            for r0 in range(0, src.shape[0], STAGE_ROWS)]

    def fetch(i):
        src, _, _, r0 = jobs[i]
        return pltpu.make_async_copy(src.at[r0:r0 + STAGE_ROWS, :],
                                     stage.at[i % 2, :, 0:src.shape[1]], sems.at[i % 2])

    fetch(0).start()
    for i, (src, dst, scale, r0) in enumerate(jobs):
        if i + 1 < len(jobs):
            fetch(i + 1).start()
        fetch(i).wait()
        vals = stage[i % 2, :, 0:src.shape[1]]
        if scale is not None:
            vals = vals * scale
        dst[r0:r0 + STAGE_ROWS, :] = vals.astype(BF16)


def _export_copies(vmem_refs, hbm_refs, sems):
    return [pltpu.make_async_copy(v, h, sems.at[i]) for i, (v, h) in enumerate(zip(vmem_refs, hbm_refs))]


def _prompt_mixer_kernel(x_ref, meta_ref, gmix_ref, wpool_ref, pscale_ref, lbl_ref, gon_ref, cscale_ref,
                         win_hbm, wa_hbm, wb_hbm, wout_hbm, wup_hbm, wdn_hbm,
                         x1_ref, pool_ref, hgrn_ref,
                         win16_hbm, wa16_hbm, wb16_hbm, wout16_hbm, wup16_hbm, wdn16_hbm,
                         z_sc, h_sc, ya_sc, ubuf, q_sc, k_sc, fg_sc, g_sc, od_sc, dec_sc,
                         w_sc, qb_sc, kb_sc, o_sc, st_sc, st_meta,
                         u_meta, up32, dn32, up16, dn16, cast_sems,
                         win_ref, wa_ref, wb_ref, wout_ref, stage, load_sems, export_sems, *, tile):
    b = pl.program_id(0)
    t = pl.program_id(1)
    nt = pl.num_programs(1)
    lb = _lower_bound(lbl_ref[...])
    gmix = gmix_ref[...]
    mixer_weights = (win_ref, wa_ref, wb_ref, wout_ref)
    exports = (win16_hbm, wa16_hbm, wb16_hbm, wout16_hbm)

    step = b * nt + t
    last_step = pl.num_programs(0) * nt - 1
    in_sems, out_sems = (cast_sems.at[0], cast_sems.at[1]), (cast_sems.at[2], cast_sems.at[3])

    def cast_out(s):
        return _slice_copies(s, (wup16_hbm, wdn16_hbm), (up16, dn16), out_sems, to_hbm=True)

    cast_in = _slice_copies(step, (wup_hbm, wdn_hbm), (up32, dn32), in_sems, to_hbm=False)
    for cp in cast_in:
        cp.start()

    @pl.when((b == 0) & (t == 0))
    def _meta():
        _load_mixer_weights((win_hbm, wa_hbm, wb_hbm, wout_hbm), (cscale_ref[...], None, None, 0.5),
                            mixer_weights, stage, load_sems)
        for cp in _export_copies(mixer_weights, exports, export_sems):
            cp.start()
        hm = _rms(meta_ref[...], gmix).astype(BF16)
        zm = _dot(hm, win_ref[:, 0:O_OG])
        u_meta[...] = zm[:, O_U:O_U + POOL_WIDTH]
        _, fg = _gates(zm[:, O_Q:O_Q + KTOT], zm[:, O_F:O_F + KTOT], lb)
        G = _cumsum_rows(jnp.log2(fg))
        kt = (1.0 - fg) * jnp.exp2(G[N_META - 1:N_META, :] - G)
        v = zm[:, O_V:O_V + HWIDTH]
        for h in range(HEADS):
            sl = slice(h * DK, (h + 1) * DK)
            st_meta[h] = _dot_tn(v[:, sl].astype(BF16), kt[:, sl].astype(BF16))

    @pl.when(t == 0)
    def _init():
        st_sc[...] = st_meta[...]
        ubuf[0:POOL_MAXW, :] = u_meta[...]

    n_chunks = tile // CHUNK
    h_sc[...] = _rms(x_ref[0], gmix).astype(BF16)
    z_sc[:, 0:O_OG] = _dot(h_sc[...], win_ref[:, 0:O_OG])

    u = z_sc[:, O_U:O_U + POOL_WIDTH]
    ubuf[POOL_MAXW:POOL_MAXW + tile, :] = u
    ya_parts = []
    for gi, w in enumerate(POOL_WINDOWS):
        cs = slice(gi * POOL_GROUP, (gi + 1) * POOL_GROUP)
        wsum = ubuf[:, cs]
        span = 1
        while span < w:
            wsum = wsum + pltpu.roll(wsum, span, 0)
            span *= 2
        pooled = wsum[POOL_MAXW:, :] / float(w) - u[:, cs]
        ya_parts.append(pooled.astype(BF16))
    ya_pre = _group_maps(ya_parts, wpool_ref) * pscale_ref[...]
    ya_sc[...] = _dot(ya_pre.astype(BF16), wa_ref[...])
    ubuf[0:POOL_MAXW, :] = ubuf[tile:tile + POOL_MAXW, :]

    def gate_piece(r0, c0):
        z_sc[r0:r0 + GATE_ROWS, c0:c0 + GATE_PIECE] = _dot(
            h_sc[r0:r0 + GATE_ROWS, :], win_ref[:, c0:c0 + GATE_PIECE])
    pending = [functools.partial(gate_piece, r0, c0)
               for c0 in range(O_OG, N_IN, GATE_PIECE) for r0 in range(0, tile, GATE_ROWS)]
    n_pieces = len(pending)
    n_groups = CHUNK // SUB
    n_slots = n_chunks * (n_groups + HEADS)
    slot = [0]

    def interleave():
        slot[0] += 1
        while len(pending) > n_pieces - (-(-slot[0] * n_pieces // n_slots)):
            pending.pop(0)()

    def gates(c):
        _gates_stage(c * CHUNK, c % 2, z_sc, lb, q_sc, k_sc, fg_sc, g_sc, od_sc, dec_sc)

    def group(c, j):
        _group_stage(j, c % 2, q_sc, k_sc, fg_sc, g_sc, w_sc, qb_sc, kb_sc)
        interleave()

    def head(c, hd):
        _head_stage(c * CHUNK, hd, c % 2, z_sc, lvl, w_sc, qb_sc, kb_sc, od_sc, dec_sc, o_sc, st_sc)
        interleave()

    lvl = _level_map(CHUNK // 2).astype(F32).astype(BF16)
    gates(0)
    for j in range(n_groups):
        group(0, j)
    for c in range(n_chunks):
        if c + 1 < n_chunks:
            gates(c + 1)
        for hd in range(HEADS):
            head(c, hd)
            if c + 1 < n_chunks:
                for j in range(hd * n_groups // HEADS, (hd + 1) * n_groups // HEADS):
                    group(c + 1, j)
    assert not pending and slot[0] == n_slots

    gon = gon_ref[...]
    o_parts = []
    for hd in range(HEADS):
        sl = slice(hd * DV, (hd + 1) * DV)
        og = z_sc[:, O_OG + hd * DV:O_OG + (hd + 1) * DV]
        o_parts.append(_rms(o_sc[:, sl], gon) * _silu(og))
    yb = _dot(jnp.concatenate(o_parts, axis=-1).astype(BF16), wb_ref[...])
    m = (_twice_sigmoid(z_sc[:, O_GA:O_GA + D_MODEL]) * ya_sc[...]
         + _twice_sigmoid(z_sc[:, O_GB:O_GB + D_MODEL]) * yb)
    x1_ref[0] = x_ref[0] + _dot(m.astype(BF16), wout_ref[...])

    for cp in cast_in:
        cp.wait()

    @pl.when(step > 0)
    def _drain_previous():
        for cp in cast_out(step - 1):
            cp.wait()

    up16[...] = up32[...].astype(BF16)
    dn16[...] = dn32[...].astype(BF16)
    for cp in cast_out(step):
        cp.start()

    @pl.when(step == last_step)
    def _drain_last():
        for cp in cast_out(step):
            cp.wait()
        for cp in _export_copies(mixer_weights, exports, export_sems):
            cp.wait()

    @pl.when(t == nt - 1)
    def _state_out():
        pool_ref[0] = ubuf[1:POOL_MAXW, :]
        for hd in range(HEADS):
            hgrn_ref[0, hd] = st_sc[hd].T


def _const_spec(shape):
    nd = len(shape)
    return pl.BlockSpec(shape, lambda *_: (0,) * nd, pipeline_mode=pl.Buffered(1))


def _prompt_mixer(x, meta, gmix, wpool, pscale, lbl, gon, cscale, win, wa, wb, wout, wup, wdown, *, tile):
    B, L, _ = x.shape
    assert L % tile == 0 and tile % CHUNK == 0 and tile % GATE_ROWS == 0
    nt = L // tile
    n_steps = B * nt
    assert wup.shape[0] % (SUB * n_steps) == 0 and wdown.shape[0] % (SUB * n_steps) == 0
    up_rows, dn_rows = wup.shape[0] // n_steps, wdown.shape[0] // n_steps
    own = (win, wa, wb, wout)
    assert all(w.shape[0] % STAGE_ROWS == 0 and w.shape[1] <= N_IN for w in own)
    kern = functools.partial(_prompt_mixer_kernel, tile=tile)
    return pl.pallas_call(
        kern,
        grid=(B, nt),
        in_specs=[
            pl.BlockSpec((1, tile, D_MODEL), lambda b, t: (b, t, 0)),
            _const_spec((N_META, D_MODEL)),
            _const_spec((1, D_MODEL)),
            _const_spec((len(POOL_WINDOWS) // 2, 2 * POOL_GROUP, 2 * POOL_GROUP)),
            _const_spec((1, POOL_WIDTH)),
            _const_spec((2, KTOT)),
            _const_spec((1, DV)),
            _const_spec((1, N_IN)),
        ] + [pl.BlockSpec(memory_space=pl.ANY)] * 6,
        out_specs=[
            pl.BlockSpec((1, tile, D_MODEL), lambda b, t: (b, t, 0)),
            pl.BlockSpec((1, POOL_BUF, POOL_WIDTH), lambda b, t: (b, 0, 0)),
            pl.BlockSpec((1, HEADS, DK, DV), lambda b, t: (b, 0, 0, 0)),
        ] + [pl.BlockSpec(memory_space=pl.ANY)] * 6,
        out_shape=[
            jax.ShapeDtypeStruct((B, L, D_MODEL), F32),
            jax.ShapeDtypeStruct((B, POOL_BUF, POOL_WIDTH), F32),
            jax.ShapeDtypeStruct((B, HEADS, DK, DV), F32),
        ] + [jax.ShapeDtypeStruct(w.shape, BF16) for w in own + (wup, wdown)],
        scratch_shapes=[
            pltpu.VMEM((tile, N_IN), F32),
            pltpu.VMEM((tile, D_MODEL), BF16),
            pltpu.VMEM((tile, D_MODEL), F32),
            pltpu.VMEM((tile + POOL_MAXW, POOL_WIDTH), F32),
            pltpu.VMEM((CHUNK, KTOT), F32),
            pltpu.VMEM((CHUNK, KTOT), F32),
            pltpu.VMEM((CHUNK, KTOT), F32),
            pltpu.VMEM((CHUNK, KTOT), F32),
            pltpu.VMEM((2, CHUNK, HWIDTH), F32),
            pltpu.VMEM((2, 8, KTOT), F32),
            pltpu.VMEM((2, len(LEVEL_HALVES), CHUNK, KTOT), BF16),
            pltpu.VMEM((2, CHUNK, KTOT), BF16),
            pltpu.VMEM((2, CHUNK, KTOT), BF16),
            pltpu.VMEM((tile, HWIDTH), F32),
            pltpu.VMEM((HEADS, DV, DK), F32),
            pltpu.VMEM((HEADS, DV, DK), F32),
            pltpu.VMEM((N_META, POOL_WIDTH), F32),
            pltpu.VMEM((up_rows, wup.shape[1]), F32),
            pltpu.VMEM((dn_rows, wdown.shape[1]), F32),
            pltpu.VMEM((up_rows, wup.shape[1]), BF16),
            pltpu.VMEM((dn_rows, wdown.shape[1]), BF16),
            pltpu.SemaphoreType.DMA((4,)),
        ] + [pltpu.VMEM(w.shape, BF16) for w in own] + [
            pltpu.VMEM((2, STAGE_ROWS, N_IN), F32),
            pltpu.SemaphoreType.DMA((2,)),
            pltpu.SemaphoreType.DMA((len(own),)),
        ],
        compiler_params=pltpu.CompilerParams(
            dimension_semantics=("arbitrary", "arbitrary"),
            vmem_limit_bytes=VMEM_LIMIT),
        name="prompt_mixer",
    )(x, meta, gmix, wpool, pscale, lbl, gon, cscale, win, wa, wb, wout, wup, wdown)


def _decode_mixer_kernel(x_ref, poolT_ref, s_ref, gmix_ref, win_ref, wpool_ref, pscale_ref,
                         lbl_ref, gon_ref, wa_ref, wb_ref, wout_ref,
                         x1_ref, unew_ref, snew_ref,
                         z_sc, o_sc, *, ts):
    i = pl.program_id(0)
    n = pl.num_programs(0)
    lb = _lower_bound(lbl_ref[...])

    @pl.when(i == 0)
    def _in_proj():
        h = _rms(x_ref[...], gmix_ref[...]).astype(BF16)
        z_sc[...] = _dot(h, win_ref[...])

    r = pl.ds(pl.multiple_of(i * ts, ts), ts)
    q, fg = _gates(z_sc[r, O_Q:O_Q + KTOT], z_sc[r, O_F:O_F + KTOT], lb)
    k = 1.0 - fg
    v = z_sc[r, O_V:O_V + HWIDTH]
    row = lax.broadcasted_iota(jnp.int32, (ts, 1), 0)
    for hd in range(HEADS):
        sl = slice(hd * DK, (hd + 1) * DK)
        fT = fg[:, sl].T
        kT = k[:, sl].T
        qf = (q[:, sl] * fg[:, sl]).astype(BF16)
        qk = jnp.sum(q[:, sl] * k[:, sl], axis=-1, keepdims=True)
        o = qk * v[:, sl]
        for s in range(ts):
            s_old = s_ref[s, hd]
            snew_ref[s, hd] = fT[:, s:s + 1] * s_old + kT[:, s:s + 1] * v[s:s + 1, sl]
            o_row = _dot(qf[s:s + 1, :], s_old.astype(BF16))
            o = o + jnp.where(row == s, o_row, 0.0)
        o_sc[r, sl] = o

    @pl.when(i == n - 1)
    def _out_proj():
        x = x_ref[...]
        u = z_sc[:, O_U:O_U + POOL_WIDTH]
        unew_ref[...] = u
        ya_parts = []
        for gi, w in enumerate(POOL_WINDOWS):
            cs = slice(gi * POOL_GROUP, (gi + 1) * POOL_GROUP)
            wsum = u[:, cs]
            for j in range(1, w):
                wsum = wsum + poolT_ref[POOL_BUF - j, :, cs]
            pooled = wsum / float(w) - u[:, cs]
            ya_parts.append(pooled.astype(BF16))
        ya_pre = _group_maps(ya_parts, wpool_ref) * pscale_ref[...]
        ya = _dot(ya_pre.astype(BF16), wa_ref[...])
        gon = gon_ref[...]
        o_parts = []
        for hd in range(HEADS):
            sl = slice(hd * DV, (hd + 1) * DV)
            og = z_sc[:, O_OG + hd * DV:O_OG + (hd + 1) * DV]
            o_parts.append(_rms(o_sc[:, sl], gon) * _silu(og))
        yb = _dot(jnp.concatenate(o_parts, axis=-1).astype(BF16), wb_ref[...])
        m = (_twice_sigmoid(z_sc[:, O_GA:O_GA + D_MODEL]) * ya
             + _twice_sigmoid(z_sc[:, O_GB:O_GB + D_MODEL]) * yb)
        x1_ref[...] = x + _dot(m.astype(BF16), wout_ref[...])


def _decode_mixer(x, poolT, state, gmix, win, wpool, pscale, lbl, gon, wa, wb, wout, *, ts):
    n = x.shape[0]
    assert n % ts == 0
    kern = functools.partial(_decode_mixer_kernel, ts=ts)
    return pl.pallas_call(
        kern,
        grid=(n // ts,),
        in_specs=[
            _const_spec((n, D_MODEL)),
            _const_spec((POOL_BUF, n, POOL_WIDTH)),
            pl.BlockSpec((ts, HEADS, DK, DV), lambda i: (i, 0, 0, 0)),
            _const_spec((1, D_MODEL)),
            _const_spec((D_MODEL, N_IN)),
            _const_spec((len(POOL_WINDOWS) // 2, 2 * POOL_GROUP, 2 * POOL_GROUP)),
            _const_spec((1, POOL_WIDTH)),
            _const_spec((2, KTOT)),
            _const_spec((1, DV)),
            _const_spec((POOL_WIDTH, D_MODEL)),
            _const_spec((HWIDTH, D_MODEL)),
            _const_spec((D_MODEL, D_MODEL)),
        ],
        out_specs=[
            pl.BlockSpec((n, D_MODEL), lambda i: (0, 0)),
            pl.BlockSpec((n, POOL_WIDTH), lambda i: (0, 0)),
            pl.BlockSpec((ts, HEADS, DK, DV), lambda i: (i, 0, 0, 0)),
        ],
        out_shape=[
            jax.ShapeDtypeStruct((n, D_MODEL), F32),
            jax.ShapeDtypeStruct((n, POOL_WIDTH), F32),
            jax.ShapeDtypeStruct((n, HEADS, DK, DV), F32),
        ],
        scratch_shapes=[
            pltpu.VMEM((n, N_IN), F32),
            pltpu.VMEM((n, HWIDTH), F32),
        ],
        compiler_params=pltpu.CompilerParams(
            dimension_semantics=("arbitrary",),
            vmem_limit_bytes=VMEM_LIMIT),
        name="decode_mixer",
    )(x, poolT, state, gmix, win, wpool, pscale, lbl, gon, wa, wb, wout)


def _mlp_kernel(x_ref, xs_ref, gmlp_ref, wup_ref, wdown_ref, gfin_ref, y_ref, ys_ref, *, ff_chunk):
    i = pl.program_id(0)
    last = pl.num_programs(0) - 1

    def rows(src_ref, dst_ref):
        x = src_ref[...]
        h = _rms(x, gmlp_ref[...]).astype(BF16)
        acc = x
        for c in range(D_FF // ff_chunk):
            cs = slice(c * ff_chunk, (c + 1) * ff_chunk)
            a = jnp.maximum(_dot(h, wup_ref[:, cs]), 0.0)
            acc = acc + _dot((a * a).astype(BF16), wdown_ref[cs, :])
        dst_ref[...] = _rms(acc, gfin_ref[...])

    @pl.when(i < last)
    def _prompt_tile():
        rows(x_ref, y_ref)

    @pl.when(i == last)
    def _sample_rows():
        rows(xs_ref, ys_ref)


def _mlp(x, xs, gmlp, wup, wdown, gfin, *, tm, ff_chunk=1024):
    n, ns = x.shape[0], xs.shape[0]
    assert n % tm == 0 and D_FF % ff_chunk == 0
    n_tiles = n // tm
    kern = functools.partial(_mlp_kernel, ff_chunk=ff_chunk)

    def tile(i):
        return jnp.minimum(i, n_tiles - 1), 0

    return pl.pallas_call(
        kern,
        grid=(n_tiles + 1,),
        in_specs=[
            pl.BlockSpec((tm, D_MODEL), tile),
            _const_spec((ns, D_MODEL)),
            _const_spec((1, D_MODEL)),
            _const_spec((D_MODEL, D_FF)),
            _const_spec((D_FF, D_MODEL)),
            _const_spec((1, D_MODEL)),
        ],
        out_specs=[
            pl.BlockSpec((tm, D_MODEL), tile),
            pl.BlockSpec((ns, D_MODEL), lambda i: (0, 0)),
        ],
        out_shape=[
            jax.ShapeDtypeStruct((n, D_MODEL), F32),
            jax.ShapeDtypeStruct((ns, D_MODEL), F32),
        ],
        compiler_params=pltpu.CompilerParams(
            dimension_semantics=("arbitrary",),
            vmem_limit_bytes=VMEM_LIMIT),
        name="channel_mlp",
    )(x, xs, gmlp, wup, wdown, gfin)


def kernel(x_prompt, x_sample, state_pool, state_hgrn, meta_tokens, g_mix, w_in, w_pool, pool_scale,
           hgrn_lb_logits, g_onorm, w_a, w_b, w_out, g_mlp, w_up, w_down, g_final):
    B, L, _ = x_prompt.shape
    NS = x_sample.shape[0]
    assert g_mix.shape[0] == 1, "single-layer trunk"
    gmix = g_mix[0][None, :]
    col = jnp.arange(N_IN)
    gate_col = ((col >= O_Q) & (col < O_V)) | (col >= O_OG)
    cscale = jnp.where(gate_col, 0.5, 1.0).astype(F32)[None, :]
    wp = w_pool[0].astype(BF16).reshape(len(POOL_WINDOWS) // 2, 2, POOL_GROUP, POOL_GROUP)
    zp = jnp.zeros_like(wp[:, 0])
    wpool = jnp.concatenate([jnp.concatenate([wp[:, 0], zp], axis=2),
                             jnp.concatenate([zp, wp[:, 1]], axis=2)], axis=1)
    pscale = pool_scale[0][None, :]
    gon = g_onorm[0][None, :]
    gmlp = g_mlp[0][None, :]
    gfin = g_final[None, :]

    x1_p, pool_p, hgrn_p, win, wa, wb, wout, wup, wdown = _prompt_mixer(
        x_prompt, meta_tokens, gmix, wpool, pscale, hgrn_lb_logits, gon, cscale,
        w_in[0], w_a[0], w_b[0], w_out[0], w_up[0], w_down[0], tile=MIXER_TILE)

    xs = x_sample.reshape(NS, D_MODEL)
    poolT = jnp.swapaxes(state_pool[0], 0, 1)
    x1_s, u_s, hgrn_s = _decode_mixer(
        xs, poolT, state_hgrn[0], gmix, win, wpool, pscale, hgrn_lb_logits, gon, wa, wb, wout,
        ts=DECODE_TILE)

    y_p, y_s = _mlp(x1_p.reshape(B * L, D_MODEL), x1_s, gmlp, wup, wdown, gfin, tm=MLP_TILE)
    pool_s = jnp.concatenate([state_pool[0][:, 1:, :], u_s[:, None, :]], axis=1)

    return (y_p.reshape(B, L, D_MODEL), y_s.reshape(NS, 1, D_MODEL),
            pool_p[None], hgrn_p[None], pool_s[None], hgrn_s[None])
```

```python
import functools

import jax
import jax.numpy as jnp
from jax import lax
from jax.experimental import pallas as pl
from jax.experimental.pallas import tpu as pltpu

D_MODEL = 1024
N_META = 16
POOL_WIDTH = 512
POOL_WINDOWS = (2, 4, 8, 16)
POOL_GROUP = 128
POOL_MAXW = 16
POOL_BUF = 15
HEADS = 4
DK = 128
DV = 128
KTOT = 512
HWIDTH = 512
D_FF = 4096
EPS = 1e-6
N_IN = 4608
O_U, O_Q, O_F, O_V, O_OG, O_GA, O_GB = 0, 512, 1024, 1536, 2048, 2560, 3584

SUB = 16
CHUNK = 256
GATE_PIECE = 256
GATE_ROWS = 512
STAGE_ROWS = 256
LEVEL_HALVES = tuple(1 << i for i in range(CHUNK.bit_length() - 1))

V7X_VMEM_BYTES = 64 * 1024 * 1024
VMEM_LIMIT = V7X_VMEM_BYTES - 8 * 1024 * 1024
MIXER_TILE = 2 * CHUNK
MLP_TILE = 512
DECODE_TILE = 2 * SUB

F32 = jnp.float32
BF16 = jnp.bfloat16
F32_TINY = 1.1754944e-38


def _rms(x, g):
    return x * lax.rsqrt(jnp.mean(x * x, axis=-1, keepdims=True) + EPS) * g


def _twice_sigmoid(hx):
    return jnp.tanh(hx) + 1.0


def _silu(hx):
    return hx * jnp.tanh(hx) + hx


def _dot(a, b):
    return jnp.dot(a, b, preferred_element_type=F32)


def _dot_nt(a, b):
    return lax.dot_general(a, b, (((1,), (1,)), ((), ())), preferred_element_type=F32)


def _dot_tn(a, b):
    return lax.dot_general(a, b, (((0,), (0,)), ((), ())), preferred_element_type=F32)


def _group_maps(pooled_groups, wpool_ref):
    outs = []
    for p in range(len(pooled_groups) // 2):
        pair = jnp.concatenate(pooled_groups[2 * p:2 * p + 2], axis=-1)
        outs.append(_dot(pair, wpool_ref[p]))
    return jnp.concatenate(outs, axis=-1)


def _lower_bound(lb_logits):
    m = jnp.max(lb_logits, axis=0, keepdims=True)
    e = jnp.exp(lb_logits - m)
    return e[0:1, :] / jnp.sum(e, axis=0, keepdims=True)


def _gates(hq, hf, lb):
    q = _silu(hq)
    fg = 0.5 * (1.0 + lb) + (0.5 * (1.0 - lb)) * jnp.tanh(hf)
    return q, jnp.maximum(fg, F32_TINY)


def _cumsum_rows(g):
    n = g.shape[0]
    r = lax.broadcasted_iota(jnp.int32, (n, n), 0)
    c = lax.broadcasted_iota(jnp.int32, (n, n), 1)
    tril = (r >= c).astype(BF16)
    hi = g.astype(BF16)
    r1 = g - hi.astype(F32)
    mid = r1.astype(BF16)
    lo = (r1 - mid.astype(F32)).astype(BF16)
    return _dot(tril, hi) + _dot(tril, mid) + _dot(tril, lo)


def _level_map(n):
    t = lax.broadcasted_iota(jnp.int32, (n, n), 0)
    s = lax.broadcasted_iota(jnp.int32, (n, n), 1)
    x = t ^ s
    lvl = jnp.full((n, n), -1, jnp.int32)
    for li in range(n.bit_length() - 1):
        lvl = jnp.where((x >> li) == 1, li, lvl)
    return jnp.where(t > s, lvl, -1)


def _level_weights(m, r0, Gg, qg, kg, fgg, g_sc, row8, row16):
    if m == 1:
        return jnp.where((row16 & 1) != 0, qg * fgg, kg)
    if m >= SUB:
        blk = (r0 // (2 * m)) * (2 * m)
        ref = g_sc[blk + m - 1:blk + m, :]
        return (jnp.exp2(Gg - ref) * qg) if (r0 & m) else (jnp.exp2(ref - Gg) * kg)
    if m == SUB // 2:
        ref = g_sc[r0 + m - 1:r0 + m, :]
        return jnp.concatenate([jnp.exp2(ref - Gg[:m]) * kg[:m], jnp.exp2(Gg[m:] - ref) * qg[m:]], axis=0)
    halves = []
    for rb in (r0, r0 + 8):
        if m == 4:
            halves.append(jnp.broadcast_to(g_sc[rb + 3:rb + 4, :], (8, KTOT)))
        else:
            halves.append(jnp.where(row8 >= 4, g_sc[rb + 5:rb + 6, :], g_sc[rb + 1:rb + 2, :]))
    ref = jnp.concatenate(halves, axis=0)
    sel = jnp.where((row16 & m) != 0, qg, kg)
    return jnp.exp2(-jnp.abs(Gg - ref)) * sel


def _gates_stage(base, par, z_sc, lb, q_sc, k_sc, fg_sc, g_sc, od_sc, dec_sc):
    rs = slice(base, base + CHUNK)
    q, fg = _gates(z_sc[rs, O_Q:O_Q + KTOT], z_sc[rs, O_F:O_F + KTOT], lb)
    k = 1.0 - fg
    G = _cumsum_rows(jnp.log2(fg))
    q_sc[...] = q
    k_sc[...] = k
    fg_sc[...] = fg
    g_sc[...] = G
    v = z_sc[rs, O_V:O_V + HWIDTH]
    qk = q * k
    for hd in range(HEADS):
        sl = slice(hd * DK, (hd + 1) * DK)
        od_sc[par, :, sl] = jnp.sum(qk[:, sl], axis=-1, keepdims=True) * v[:, sl]
    dec_sc[par] = jnp.broadcast_to(jnp.exp2(G[CHUNK - 1:CHUNK, :]), (8, KTOT))


def _group_stage(j, par, q_sc, k_sc, fg_sc, g_sc, w_sc, qb_sc, kb_sc):
    row8 = lax.broadcasted_iota(jnp.int32, (8, 1), 0)
    row16 = lax.broadcasted_iota(jnp.int32, (SUB, 1), 0)
    r0 = j * SUB
    gs = slice(r0, r0 + SUB)
    Gg, qg, kg, fgg = g_sc[gs, :], q_sc[gs, :], k_sc[gs, :], fg_sc[gs, :]
    g_last = g_sc[CHUNK - 1:CHUNK, :]
    for li, m in enumerate(LEVEL_HALVES):
        w_sc[par, li, gs, :] = _level_weights(m, r0, Gg, qg, kg, fgg, g_sc, row8, row16).astype(BF16)
    qb_sc[par, gs, :] = (qg * jnp.exp2(Gg)).astype(BF16)
    kb_sc[par, gs, :] = (kg * jnp.exp2(g_last - Gg)).astype(BF16)


def _head_stage(base, hd, par, z_sc, lvl, w_sc, qb_sc, kb_sc, od_sc, dec_sc, o_sc, st_sc):
    rs = slice(base, base + CHUNK)
    sl = slice(hd * DK, (hd + 1) * DK)
    half = CHUNK // 2
    a_lo = jnp.zeros((half, half), F32)
    a_hi = jnp.zeros((half, half), F32)
    zero = jnp.zeros((half, DK), BF16)
    for li in range(len(LEVEL_HALVES) - 1):
        w_lo = w_sc[par, li, :half, sl]
        w_hi = w_sc[par, li, half:, sl]
        x = jnp.concatenate([w_lo, w_hi], axis=1)
        y = jnp.concatenate([jnp.concatenate([w_lo.T, zero], axis=1),
                             jnp.concatenate([zero, w_hi.T], axis=1)], axis=0)
        p = _dot(x, y)
        a_lo = jnp.where(lvl == li, p[:, :half], a_lo)
        a_hi = jnp.where(lvl == li, p[:, half:], a_hi)
    top = len(LEVEL_HALVES) - 1
    a_x = _dot_nt(w_sc[par, top, half:, sl], w_sc[par, top, :half, sl])
    A = jnp.concatenate([jnp.concatenate([a_lo, jnp.zeros((half, half), F32)], axis=1),
                         jnp.concatenate([a_x, a_hi], axis=1)], axis=0)
    v_bf = z_sc[rs, O_V + hd * DV:O_V + (hd + 1) * DV].astype(BF16)
    st = st_sc[hd]
    o_sc[rs, sl] = (_dot(A.astype(BF16), v_bf)
                    + _dot_nt(qb_sc[par, :, sl], st.astype(BF16))
                    + od_sc[par, :, sl])
    st_sc[hd] = st * dec_sc[par, 0:1, sl] + _dot_tn(v_bf, kb_sc[par, :, sl])


def _slice_copies(step, hbm_refs, vmem_refs, sems, to_hbm):
    copies = []
    for hbm, vmem, sem in zip(hbm_refs, vmem_refs, sems):
        rows = vmem.shape[0]
        window = hbm.at[pl.ds(pl.multiple_of(step * rows, rows), rows), :]
        copies.append(pltpu.make_async_copy(vmem, window, sem) if to_hbm
                      else pltpu.make_async_copy(window, vmem, sem))
    return copies


def _load_mixer_weights(hbm_refs, scales, vmem_refs, stage, sems):
    jobs = [(src, dst, scale, r0)
            for src, dst, scale in zip(hbm_refs, vmem_refs, scales)
            for r0 in range(0, src.shape[0], STAGE_ROWS)]

    def fetch(i):
        src, _, _, r0 = jobs[i]
        return pltpu.make_async_copy(src.at[r0:r0 + STAGE_ROWS, :],
                                     stage.at[i % 2, :, 0:src.shape[1]], sems.at[i % 2])

    fetch(0).start()
    for i, (src, dst, scale, r0) in enumerate(jobs):
        if i + 1 < len(jobs):
            fetch(i + 1).start()
        fetch(i).wait()
        vals = stage[i % 2, :, 0:src.shape[1]]
        if scale is not None:
            vals = vals * scale
        dst[r0:r0 + STAGE_ROWS, :] = vals.astype(BF16)


def _export_copies(src_refs, dst_refs, sems):
    return [pltpu.make_async_copy(s, d, sems.at[i]) for i, (s, d) in enumerate(zip(src_refs, dst_refs))]


def _prompt_mixer_kernel(x_ref, meta_ref, gmix_ref, wpool_ref, pscale_ref, lbl_ref, gon_ref, cscale_ref,
                         win_hbm, wa_hbm, wb_hbm, wout_hbm, wup_hbm, wdn_hbm,
                         x1_ref, pool_ref, hgrn_ref,
                         win16_hbm, wa16_hbm, wb16_hbm, wout16_hbm, wup16_hbm, wdn16_hbm,
                         z_sc, h_sc, ya_sc, ubuf, q_sc, k_sc, fg_sc, g_sc, od_sc, dec_sc,
                         w_sc, qb_sc, kb_sc, o_sc, st_sc, st_meta,
                         u_meta, up32, dn32, up16, dn16, cast_sems,
                         win_ref, wa_ref, wb_ref, wout_ref, stage, load_sems, export_sems, *, tile):
    b = pl.program_id(0)
    t = pl.program_id(1)
    nt = pl.num_programs(1)
    lb = _lower_bound(lbl_ref[...])
    gmix = gmix_ref[...]
    mixer_weights = (win_ref, wa_ref, wb_ref, wout_ref)
    exports = (win16_hbm, wa16_hbm, wb16_hbm, wout16_hbm)

    step = b * nt + t
    last_step = pl.num_programs(0) * nt - 1
    in_sems, out_sems = (cast_sems.at[0], cast_sems.at[1]), (cast_sems.at[2], cast_sems.at[3])

    def cast_out(s):
        return _slice_copies(s, (wup16_hbm, wdn16_hbm), (up16, dn16), out_sems, to_hbm=True)

    cast_in = _slice_copies(step, (wup_hbm, wdn_hbm), (up32, dn32), in_sems, to_hbm=False)
    for cp in cast_in:
        cp.start()

    @pl.when((b == 0) & (t == 0))
    def _meta():
        _load_mixer_weights((win_hbm, wa_hbm, wb_hbm, wout_hbm), (cscale_ref[...], None, None, 0.5),
                            mixer_weights, stage, load_sems)
        for cp in _export_copies(mixer_weights, exports, export_sems):
            cp.start()
        hm = _rms(meta_ref[...], gmix).astype(BF16)
        zm = _dot(hm, win_ref[:, 0:O_OG])
        u_meta[...] = zm[:, O_U:O_U + POOL_WIDTH]
        _, fg = _gates(zm[:, O_Q:O_Q + KTOT], zm[:, O_F:O_F + KTOT], lb)
        G = _cumsum_rows(jnp.log2(fg))
        kt = (1.0 - fg) * jnp.exp2(G[N_META - 1:N_META, :] - G)
        v = zm[:, O_V:O_V + HWIDTH]
        for h in range(HEADS):
            sl = slice(h * DK, (h + 1) * DK)
            st_meta[h] = _dot_tn(v[:, sl].astype(BF16), kt[:, sl].astype(BF16))

    @pl.when(t == 0)
    def _init():
        st_sc[...] = st_meta[...]
        ubuf[0:POOL_MAXW, :] = u_meta[...]

    n_chunks = tile // CHUNK
    h_sc[...] = _rms(x_ref[0], gmix).astype(BF16)
    z_sc[:, 0:O_OG] = _dot(h_sc[...], win_ref[:, 0:O_OG])

    u = z_sc[:, O_U:O_U + POOL_WIDTH]
    ubuf[POOL_MAXW:POOL_MAXW + tile, :] = u
    ya_parts = []
    for gi, w in enumerate(POOL_WINDOWS):
        cs = slice(gi * POOL_GROUP, (gi + 1) * POOL_GROUP)
        wsum = ubuf[:, cs]
        span = 1
        while span < w:
            wsum = wsum + pltpu.roll(wsum, span, 0)
            span *= 2
        pooled = wsum[POOL_MAXW:, :] / float(w) - u[:, cs]
        ya_parts.append(pooled.astype(BF16))
    ya_pre = _group_maps(ya_parts, wpool_ref) * pscale_ref[...]
    ya_sc[...] = _dot(ya_pre.astype(BF16), wa_ref[...])
    ubuf[0:POOL_MAXW, :] = ubuf[tile:tile + POOL_MAXW, :]

    def gate_piece(r0, c0):
        z_sc[r0:r0 + GATE_ROWS, c0:c0 + GATE_PIECE] = _dot(
            h_sc[r0:r0 + GATE_ROWS, :], win_ref[:, c0:c0 + GATE_PIECE])
    pending = [functools.partial(gate_piece, r0, c0)
               for c0 in range(O_OG, N_IN, GATE_PIECE) for r0 in range(0, tile, GATE_ROWS)]
    n_pieces = len(pending)
    n_groups = CHUNK // SUB
    n_slots = n_chunks * (n_groups + HEADS)
    slot = [0]

    def interleave():
        slot[0] += 1
        while len(pending) > n_pieces - (-(-slot[0] * n_pieces // n_slots)):
            pending.pop(0)()

    def gates(c):
        _gates_stage(c * CHUNK, c % 2, z_sc, lb, q_sc, k_sc, fg_sc, g_sc, od_sc, dec_sc)

    def group(c, j):
        _group_stage(j, c % 2, q_sc, k_sc, fg_sc, g_sc, w_sc, qb_sc, kb_sc)
        interleave()

    def head(c, hd):
        _head_stage(c * CHUNK, hd, c % 2, z_sc, lvl, w_sc, qb_sc, kb_sc, od_sc, dec_sc, o_sc, st_sc)
        interleave()

    lvl = _level_map(CHUNK // 2)
    gates(0)
    for j in range(n_groups):
        group(0, j)
    for c in range(n_chunks):
        if c + 1 < n_chunks:
            gates(c + 1)
        for hd in range(HEADS):
            head(c, hd)
            if c + 1 < n_chunks:
                for j in range(hd * n_groups // HEADS, (hd + 1) * n_groups // HEADS):
                    group(c + 1, j)
    assert not pending and slot[0] == n_slots

    gon = gon_ref[...]
    o_parts = []
    for hd in range(HEADS):
        sl = slice(hd * DV, (hd + 1) * DV)
        og = z_sc[:, O_OG + hd * DV:O_OG + (hd + 1) * DV]
        o_parts.append(_rms(o_sc[:, sl], gon) * _silu(og))
    yb = _dot(jnp.concatenate(o_parts, axis=-1).astype(BF16), wb_ref[...])
    m = (_twice_sigmoid(z_sc[:, O_GA:O_GA + D_MODEL]) * ya_sc[...]
         + _twice_sigmoid(z_sc[:, O_GB:O_GB + D_MODEL]) * yb)
    x1_ref[0] = x_ref[0] + _dot(m.astype(BF16), wout_ref[...])

    for cp in cast_in:
        cp.wait()

    @pl.when(step > 0)
    def _drain_previous():
        for cp in cast_out(step - 1):
            cp.wait()

    up16[...] = up32[...].astype(BF16)
    dn16[...] = dn32[...].astype(BF16)
    for cp in cast_out(step):
        cp.start()

    @pl.when(step == last_step)
    def _drain_last():
        for cp in cast_out(step):
            cp.wait()
        for cp in _export_copies(mixer_weights, exports, export_sems):
            cp.wait()

    @pl.when(t == nt - 1)
    def _state_out():
        pool_ref[0] = ubuf[1:POOL_MAXW, :]
        for hd in range(HEADS):
            hgrn_ref[0, hd] = st_sc[hd].T


def _const_spec(shape):
    nd = len(shape)
    return pl.BlockSpec(shape, lambda *_: (0,) * nd, pipeline_mode=pl.Buffered(1))


def _prompt_mixer(x, meta, gmix, wpool, pscale, lbl, gon, cscale, win, wa, wb, wout, wup, wdown, *, tile):
    B, L, _ = x.shape
    assert L % tile == 0 and tile % CHUNK == 0 and tile % GATE_ROWS == 0
    nt = L // tile
    n_steps = B * nt
    assert wup.shape[0] % (SUB * n_steps) == 0 and wdown.shape[0] % (SUB * n_steps) == 0
    up_rows, dn_rows = wup.shape[0] // n_steps, wdown.shape[0] // n_steps
    own = (win, wa, wb, wout)
    assert all(w.shape[0] % STAGE_ROWS == 0 and w.shape[1] <= N_IN for w in own)
    kern = functools.partial(_prompt_mixer_kernel, tile=tile)
    return pl.pallas_call(
        kern,
        grid=(B, nt),
        in_specs=[
            pl.BlockSpec((1, tile, D_MODEL), lambda b, t: (b, t, 0)),
            _const_spec((N_META, D_MODEL)),
            _const_spec((1, D_MODEL)),
            _const_spec((len(POOL_WINDOWS) // 2, 2 * POOL_GROUP, 2 * POOL_GROUP)),
            _const_spec((1, POOL_WIDTH)),
            _const_spec((2, KTOT)),
            _const_spec((1, DV)),
            _const_spec((1, N_IN)),
        ] + [pl.BlockSpec(memory_space=pl.ANY)] * 6,
        out_specs=[
            pl.BlockSpec((1, tile, D_MODEL), lambda b, t: (b, t, 0)),
            pl.BlockSpec((1, POOL_BUF, POOL_WIDTH), lambda b, t: (b, 0, 0)),
            pl.BlockSpec((1, HEADS, DK, DV), lambda b, t: (b, 0, 0, 0)),
        ] + [pl.BlockSpec(memory_space=pl.ANY)] * 6,
        out_shape=[
            jax.ShapeDtypeStruct((B, L, D_MODEL), F32),
            jax.ShapeDtypeStruct((B, POOL_BUF, POOL_WIDTH), F32),
            jax.ShapeDtypeStruct((B, HEADS, DK, DV), F32),
        ] + [jax.ShapeDtypeStruct(w.shape, BF16) for w in own + (wup, wdown)],
        scratch_shapes=[
            pltpu.VMEM((tile, N_IN), F32),
            pltpu.VMEM((tile, D_MODEL), BF16),
            pltpu.VMEM((tile, D_MODEL), F32),
            pltpu.VMEM((tile + POOL_MAXW, POOL_WIDTH), F32),
            pltpu.VMEM((CHUNK, KTOT), F32),
            pltpu.VMEM((CHUNK, KTOT), F32),
            pltpu.VMEM((CHUNK, KTOT), F32),
            pltpu.VMEM((CHUNK, KTOT), F32),
            pltpu.VMEM((2, CHUNK, HWIDTH), F32),
            pltpu.VMEM((2, 8, KTOT), F32),
            pltpu.VMEM((2, len(LEVEL_HALVES), CHUNK, KTOT), BF16),
            pltpu.VMEM((2, CHUNK, KTOT), BF16),
            pltpu.VMEM((2, CHUNK, KTOT), BF16),
            pltpu.VMEM((tile, HWIDTH), F32),
            pltpu.VMEM((HEADS, DV, DK), F32),
            pltpu.VMEM((HEADS, DV, DK), F32),
            pltpu.VMEM((N_META, POOL_WIDTH), F32),
            pltpu.VMEM((up_rows, wup.shape[1]), F32),
            pltpu.VMEM((dn_rows, wdown.shape[1]), F32),
            pltpu.VMEM((up_rows, wup.shape[1]), BF16),
            pltpu.VMEM((dn_rows, wdown.shape[1]), BF16),
            pltpu.SemaphoreType.DMA((4,)),
        ] + [pltpu.VMEM(w.shape, BF16) for w in own] + [
            pltpu.VMEM((2, STAGE_ROWS, N_IN), F32),
            pltpu.SemaphoreType.DMA((2,)),
            pltpu.SemaphoreType.DMA((len(own),)),
        ],
        compiler_params=pltpu.CompilerParams(
            dimension_semantics=("arbitrary", "arbitrary"),
            vmem_limit_bytes=VMEM_LIMIT),
        name="prompt_mixer",
    )(x, meta, gmix, wpool, pscale, lbl, gon, cscale, win, wa, wb, wout, wup, wdown)


def _decode_mixer_kernel(x_ref, s_ref, gmix_ref, win_ref, pscale_ref, lbl_ref, gon_ref,
                         poolT_hbm, wpool_hbm, wa_hbm, wb_hbm, wout_hbm,
                         x1_ref, unew_ref, snew_ref,
                         z_sc, o_sc, poolT_ref, wpool_ref, wa_ref, wb_ref, wout_ref, late_sems, *, ts):
    i = pl.program_id(0)
    n = pl.num_programs(0)
    lb = _lower_bound(lbl_ref[...])
    late_src = (poolT_hbm, wpool_hbm, wa_hbm, wb_hbm, wout_hbm)
    late_dst = (poolT_ref, wpool_ref, wa_ref, wb_ref, wout_ref)

    @pl.when(i == 0)
    def _in_proj():
        for cp in _export_copies(late_src, late_dst, late_sems):
            cp.start()
        h = _rms(x_ref[...], gmix_ref[...]).astype(BF16)
        z_sc[...] = _dot(h, win_ref[...])

    r = pl.ds(pl.multiple_of(i * ts, ts), ts)
    q, fg = _gates(z_sc[r, O_Q:O_Q + KTOT], z_sc[r, O_F:O_F + KTOT], lb)
    k = 1.0 - fg
    v = z_sc[r, O_V:O_V + HWIDTH]
    row = lax.broadcasted_iota(jnp.int32, (ts, 1), 0)
    for hd in range(HEADS):
        sl = slice(hd * DK, (hd + 1) * DK)
        fT = fg[:, sl].T
        kT = k[:, sl].T
        qf = (q[:, sl] * fg[:, sl]).astype(BF16)
        qk = jnp.sum(q[:, sl] * k[:, sl], axis=-1, keepdims=True)
        o = qk * v[:, sl]
        for s in range(ts):
            s_old = s_ref[s, hd]
            snew_ref[s, hd] = fT[:, s:s + 1] * s_old + kT[:, s:s + 1] * v[s:s + 1, sl]
            o_row = _dot(qf[s:s + 1, :], s_old.astype(BF16))
            o = o + jnp.where(row == s, o_row, 0.0)
        o_sc[r, sl] = o

    @pl.when(i == n - 1)
    def _out_proj():
        for cp in _export_copies(late_src, late_dst, late_sems):
            cp.wait()
        x = x_ref[...]
        u = z_sc[:, O_U:O_U + POOL_WIDTH]
        unew_ref[...] = u
        ya_parts = []
        for gi, w in enumerate(POOL_WINDOWS):
            cs = slice(gi * POOL_GROUP, (gi + 1) * POOL_GROUP)
            wsum = u[:, cs]
            for j in range(1, w):
                wsum = wsum + poolT_ref[POOL_BUF - j, :, cs]
            pooled = wsum / float(w) - u[:, cs]
            ya_parts.append(pooled.astype(BF16))
        ya_pre = _group_maps(ya_parts, wpool_ref) * pscale_ref[...]
        ya = _dot(ya_pre.astype(BF16), wa_ref[...])
        gon = gon_ref[...]
        o_parts = []
        for hd in range(HEADS):
            sl = slice(hd * DV, (hd + 1) * DV)
            og = z_sc[:, O_OG + hd * DV:O_OG + (hd + 1) * DV]
            o_parts.append(_rms(o_sc[:, sl], gon) * _silu(og))
        yb = _dot(jnp.concatenate(o_parts, axis=-1).astype(BF16), wb_ref[...])
        m = (_twice_sigmoid(z_sc[:, O_GA:O_GA + D_MODEL]) * ya
             + _twice_sigmoid(z_sc[:, O_GB:O_GB + D_MODEL]) * yb)
        x1_ref[...] = x + _dot(m.astype(BF16), wout_ref[...])


def _decode_mixer(x, poolT, state, gmix, win, wpool, pscale, lbl, gon, wa, wb, wout, *, ts):
    n = x.shape[0]
    assert n % ts == 0
    late = (poolT, wpool, wa, wb, wout)
    kern = functools.partial(_decode_mixer_kernel, ts=ts)
    return pl.pallas_call(
        kern,
        grid=(n // ts,),
        in_specs=[
            _const_spec((n, D_MODEL)),
            pl.BlockSpec((ts, HEADS, DK, DV), lambda i: (i, 0, 0, 0)),
            _const_spec((1, D_MODEL)),
            _const_spec((D_MODEL, N_IN)),
            _const_spec((1, POOL_WIDTH)),
            _const_spec((2, KTOT)),
            _const_spec((1, DV)),
        ] + [pl.BlockSpec(memory_space=pl.ANY)] * len(late),
        out_specs=[
            pl.BlockSpec((n, D_MODEL), lambda i: (0, 0)),
            pl.BlockSpec((n, POOL_WIDTH), lambda i: (0, 0)),
            pl.BlockSpec((ts, HEADS, DK, DV), lambda i: (i, 0, 0, 0)),
        ],
        out_shape=[
            jax.ShapeDtypeStruct((n, D_MODEL), F32),
            jax.ShapeDtypeStruct((n, POOL_WIDTH), F32),
            jax.ShapeDtypeStruct((n, HEADS, DK, DV), F32),
        ],
        scratch_shapes=[
            pltpu.VMEM((n, N_IN), F32),
            pltpu.VMEM((n, HWIDTH), F32),
        ] + [pltpu.VMEM(a.shape, a.dtype) for a in late] + [pltpu.SemaphoreType.DMA((len(late),))],
        compiler_params=pltpu.CompilerParams(
            dimension_semantics=("arbitrary",),
            vmem_limit_bytes=VMEM_LIMIT),
        name="decode_mixer",
    )(x, state, gmix, win, pscale, lbl, gon, *late)


def _mlp_kernel(x_ref, xs_ref, gmlp_ref, wup_ref, wdown_ref, gfin_ref, y_ref, ys_ref, *, ff_chunk):
    i = pl.program_id(0)
    last = pl.num_programs(0) - 1

    def rows(src_ref, dst_ref):
        x = src_ref[...]
        h = _rms(x, gmlp_ref[...]).astype(BF16)
        acc = x
        for c in range(D_FF // ff_chunk):
            cs = slice(c * ff_chunk, (c + 1) * ff_chunk)
            a = jnp.maximum(_dot(h, wup_ref[:, cs]), 0.0)
            acc = acc + _dot((a * a).astype(BF16), wdown_ref[cs, :])
        dst_ref[...] = _rms(acc, gfin_ref[...])

    @pl.when(i < last)
    def _prompt_tile():
        rows(x_ref, y_ref)

    @pl.when(i == last)
    def _sample_rows():
        rows(xs_ref, ys_ref)


def _mlp(x, xs, gmlp, wup, wdown, gfin, *, tm, ff_chunk=1024):
    n, ns = x.shape[0], xs.shape[0]
    assert n % tm == 0 and D_FF % ff_chunk == 0
    n_tiles = n // tm
    kern = functools.partial(_mlp_kernel, ff_chunk=ff_chunk)

    def tile(i):
        return jnp.minimum(i, n_tiles - 1), 0

    return pl.pallas_call(
        kern,
        grid=(n_tiles + 1,),
        in_specs=[
            pl.BlockSpec((tm, D_MODEL), tile),
            _const_spec((ns, D_MODEL)),
            _const_spec((1, D_MODEL)),
            _const_spec((D_MODEL, D_FF)),
            _const_spec((D_FF, D_MODEL)),
            _const_spec((1, D_MODEL)),
        ],
        out_specs=[
            pl.BlockSpec((tm, D_MODEL), tile),
            pl.BlockSpec((ns, D_MODEL), lambda i: (0, 0)),
        ],
        out_shape=[
            jax.ShapeDtypeStruct((n, D_MODEL), F32),
            jax.ShapeDtypeStruct((ns, D_MODEL), F32),
        ],
        compiler_params=pltpu.CompilerParams(
            dimension_semantics=("arbitrary",),
            vmem_limit_bytes=VMEM_LIMIT),
        name="channel_mlp",
    )(x, xs, gmlp, wup, wdown, gfin)


def kernel(x_prompt, x_sample, state_pool, state_hgrn, meta_tokens, g_mix, w_in, w_pool, pool_scale,
           hgrn_lb_logits, g_onorm, w_a, w_b, w_out, g_mlp, w_up, w_down, g_final):
    B, L, _ = x_prompt.shape
    NS = x_sample.shape[0]
    assert g_mix.shape[0] == 1, "single-layer trunk"
    gmix = g_mix[0][None, :]
    col = jnp.arange(N_IN)
    gate_col = ((col >= O_Q) & (col < O_V)) | (col >= O_OG)
    cscale = jnp.where(gate_col, 0.5, 1.0).astype(F32)[None, :]
    wp = w_pool[0].astype(BF16).reshape(len(POOL_WINDOWS) // 2, 2, POOL_GROUP, POOL_GROUP)
    zp = jnp.zeros_like(wp[:, 0])
    wpool = jnp.concatenate([jnp.concatenate([wp[:, 0], zp], axis=2),
                             jnp.concatenate([zp, wp[:, 1]], axis=2)], axis=1)
    pscale = pool_scale[0][None, :]
    gon = g_onorm[0][None, :]
    gmlp = g_mlp[0][None, :]
    gfin = g_final[None, :]

    x1_p, pool_p, hgrn_p, win, wa, wb, wout, wup, wdown = _prompt_mixer(
        x_prompt, meta_tokens, gmix, wpool, pscale, hgrn_lb_logits, gon, cscale,
        w_in[0], w_a[0], w_b[0], w_out[0], w_up[0], w_down[0], tile=MIXER_TILE)

    xs = x_sample.reshape(NS, D_MODEL)
    poolT = jnp.swapaxes(state_pool[0], 0, 1)
    x1_s, u_s, hgrn_s = _decode_mixer(
        xs, poolT, state_hgrn[0], gmix, win, wpool, pscale, hgrn_lb_logits, gon, wa, wb, wout,
        ts=DECODE_TILE)

    y_p, y_s = _mlp(x1_p.reshape(B * L, D_MODEL), x1_s, gmlp, wup, wdown, gfin, tm=MLP_TILE)
    pool_s = jnp.concatenate([state_pool[0][:, 1:, :], u_s[:, None, :]], axis=1)

    return (y_p.reshape(B, L, D_MODEL), y_s.reshape(NS, 1, D_MODEL),
            pool_p[None], hgrn_p[None], pool_s[None], hgrn_s[None])
```

```python
import functools

import jax
import jax.numpy as jnp
from jax import lax
from jax.experimental import pallas as pl
from jax.experimental.pallas import tpu as pltpu

D_MODEL = 1024
N_META = 16
POOL_WIDTH = 512
POOL_WINDOWS = (2, 4, 8, 16)
POOL_GROUP = 128
POOL_MAXW = 16
POOL_BUF = 15
HEADS = 4
DK = 128
DV = 128
KTOT = 512
HWIDTH = 512
D_FF = 4096
EPS = 1e-6
N_IN = 4608
O_U, O_Q, O_F, O_V, O_OG, O_GA, O_GB = 0, 512, 1024, 1536, 2048, 2560, 3584

SUB = 16
CHUNK = 256
GATE_PIECE = 256
GATE_ROWS = 512
STAGE_ROWS = 256
LEVEL_HALVES = tuple(1 << i for i in range(CHUNK.bit_length() - 1))

V7X_VMEM_BYTES = 64 * 1024 * 1024
VMEM_LIMIT = V7X_VMEM_BYTES - 8 * 1024 * 1024
MIXER_TILE = 2 * CHUNK
MLP_TILE = 512
DECODE_TILE = 2 * SUB

F32 = jnp.float32
BF16 = jnp.bfloat16
F32_TINY = 1.1754944e-38


def _rms(x, g):
    return x * lax.rsqrt(jnp.mean(x * x, axis=-1, keepdims=True) + EPS) * g


def _twice_sigmoid(hx):
    return jnp.tanh(hx) + 1.0


def _silu(hx):
    return hx * jnp.tanh(hx) + hx


def _dot(a, b):
    return jnp.dot(a, b, preferred_element_type=F32)


def _dot_nt(a, b):
    return lax.dot_general(a, b, (((1,), (1,)), ((), ())), preferred_element_type=F32)


def _dot_tn(a, b):
    return lax.dot_general(a, b, (((0,), (0,)), ((), ())), preferred_element_type=F32)


def _group_maps(pooled_groups, wpool_ref):
    outs = []
    for p in range(len(pooled_groups) // 2):
        pair = jnp.concatenate(pooled_groups[2 * p:2 * p + 2], axis=-1)
        outs.append(_dot(pair, wpool_ref[p]))
    return jnp.concatenate(outs, axis=-1)


def _lower_bound(lb_logits):
    m = jnp.max(lb_logits, axis=0, keepdims=True)
    e = jnp.exp(lb_logits - m)
    return e[0:1, :] / jnp.sum(e, axis=0, keepdims=True)


def _gates(hq, hf, lb):
    q = _silu(hq)
    fg = 0.5 * (1.0 + lb) + (0.5 * (1.0 - lb)) * jnp.tanh(hf)
    return q, jnp.maximum(fg, F32_TINY)


def _cumsum_rows(g):
    n = g.shape[0]
    r = lax.broadcasted_iota(jnp.int32, (n, n), 0)
    c = lax.broadcasted_iota(jnp.int32, (n, n), 1)
    tril = (r >= c).astype(BF16)
    hi = g.astype(BF16)
    r1 = g - hi.astype(F32)
    mid = r1.astype(BF16)
    lo = (r1 - mid.astype(F32)).astype(BF16)
    return _dot(tril, hi) + _dot(tril, mid) + _dot(tril, lo)


def _level_map(n):
    t = lax.broadcasted_iota(jnp.int32, (n, n), 0)
    s = lax.broadcasted_iota(jnp.int32, (n, n), 1)
    x = t ^ s
    lvl = jnp.full((n, n), -1, jnp.int32)
    for li in range(n.bit_length() - 1):
        lvl = jnp.where((x >> li) == 1, li, lvl)
    return jnp.where(t > s, lvl, -1)


def _level_weights(m, r0, Gg, qg, kg, fgg, g_sc, row8, row16):
    if m == 1:
        return jnp.where((row16 & 1) != 0, qg * fgg, kg)
    if m >= SUB:
        blk = (r0 // (2 * m)) * (2 * m)
        ref = g_sc[blk + m - 1:blk + m, :]
        return (jnp.exp2(Gg - ref) * qg) if (r0 & m) else (jnp.exp2(ref - Gg) * kg)
    if m == SUB // 2:
        ref = g_sc[r0 + m - 1:r0 + m, :]
        return jnp.concatenate([jnp.exp2(ref - Gg[:m]) * kg[:m], jnp.exp2(Gg[m:] - ref) * qg[m:]], axis=0)
    halves = []
    for rb in (r0, r0 + 8):
        if m == 4:
            halves.append(jnp.broadcast_to(g_sc[rb + 3:rb + 4, :], (8, KTOT)))
        else:
            halves.append(jnp.where(row8 >= 4, g_sc[rb + 5:rb + 6, :], g_sc[rb + 1:rb + 2, :]))
    ref = jnp.concatenate(halves, axis=0)
    sel = jnp.where((row16 & m) != 0, qg, kg)
    return jnp.exp2(-jnp.abs(Gg - ref)) * sel


def _gates_stage(base, par, z_sc, lb, q_sc, k_sc, fg_sc, g_sc, od_sc, dec_sc):
    rs = slice(base, base + CHUNK)
    q, fg = _gates(z_sc[rs, O_Q:O_Q + KTOT], z_sc[rs, O_F:O_F + KTOT], lb)
    k = 1.0 - fg
    G = _cumsum_rows(jnp.log2(fg))
    q_sc[...] = q
    k_sc[...] = k
    fg_sc[...] = fg
    g_sc[...] = G
    v = z_sc[rs, O_V:O_V + HWIDTH]
    qk = q * k
    for hd in range(HEADS):
        sl = slice(hd * DK, (hd + 1) * DK)
        od_sc[par, :, sl] = jnp.sum(qk[:, sl], axis=-1, keepdims=True) * v[:, sl]
    dec_sc[par] = jnp.broadcast_to(jnp.exp2(G[CHUNK - 1:CHUNK, :]), (8, KTOT))


def _group_stage(j, par, q_sc, k_sc, fg_sc, g_sc, w_sc, qb_sc, kb_sc):
    row8 = lax.broadcasted_iota(jnp.int32, (8, 1), 0)
    row16 = lax.broadcasted_iota(jnp.int32, (SUB, 1), 0)
    r0 = j * SUB
    gs = slice(r0, r0 + SUB)
    Gg, qg, kg, fgg = g_sc[gs, :], q_sc[gs, :], k_sc[gs, :], fg_sc[gs, :]
    g_last = g_sc[CHUNK - 1:CHUNK, :]
    for li, m in enumerate(LEVEL_HALVES):
        w_sc[par, li, gs, :] = _level_weights(m, r0, Gg, qg, kg, fgg, g_sc, row8, row16).astype(BF16)
    qb_sc[par, gs, :] = (qg * jnp.exp2(Gg)).astype(BF16)
    kb_sc[par, gs, :] = (kg * jnp.exp2(g_last - Gg)).astype(BF16)


def _head_stage(base, hd, par, z_sc, lvl, w_sc, qb_sc, kb_sc, od_sc, dec_sc, o_sc, st_sc):
    rs = slice(base, base + CHUNK)
    sl = slice(hd * DK, (hd + 1) * DK)
    half = CHUNK // 2
    a_lo = jnp.zeros((half, half), F32)
    a_hi = jnp.zeros((half, half), F32)
    zero = jnp.zeros((half, DK), BF16)
    for li in range(len(LEVEL_HALVES) - 1):
        w_lo = w_sc[par, li, :half, sl]
        w_hi = w_sc[par, li, half:, sl]
        x = jnp.concatenate([w_lo, w_hi], axis=1)
        y = jnp.concatenate([jnp.concatenate([w_lo.T, zero], axis=1),
                             jnp.concatenate([zero, w_hi.T], axis=1)], axis=0)
        p = _dot(x, y)
        a_lo = jnp.where(lvl == li, p[:, :half], a_lo)
        a_hi = jnp.where(lvl == li, p[:, half:], a_hi)
    top = len(LEVEL_HALVES) - 1
    a_x = _dot_nt(w_sc[par, top, half:, sl], w_sc[par, top, :half, sl])
    A = jnp.concatenate([jnp.concatenate([a_lo, jnp.zeros((half, half), F32)], axis=1),
                         jnp.concatenate([a_x, a_hi], axis=1)], axis=0)
    v_bf = z_sc[rs, O_V + hd * DV:O_V + (hd + 1) * DV].astype(BF16)
    st = st_sc[hd]
    o_sc[rs, sl] = (_dot(A.astype(BF16), v_bf)
                    + _dot_nt(qb_sc[par, :, sl], st.astype(BF16))
                    + od_sc[par, :, sl])
    st_sc[hd] = st * dec_sc[par, 0:1, sl] + _dot_tn(v_bf, kb_sc[par, :, sl])


def _slice_copies(step, hbm_refs, vmem_refs, sems, to_hbm):
    copies = []
    for hbm, vmem, sem in zip(hbm_refs, vmem_refs, sems):
        rows = vmem.shape[0]
        window = hbm.at[pl.ds(pl.multiple_of(step * rows, rows), rows), :]
        copies.append(pltpu.make_async_copy(vmem, window, sem) if to_hbm
                      else pltpu.make_async_copy(window, vmem, sem))
    return copies


def _load_mixer_weights(hbm_refs, scales, vmem_refs, stage, sems):
    jobs = [(src, dst, scale, r0)
            for src, dst, scale in zip(hbm_refs, vmem_refs, scales)
            for r0 in range(0, src.shape[0], STAGE_ROWS)]

    def fetch(i):
        src, _, _, r0 = jobs[i]
        return pltpu.make_async_copy(src.at[r0:r0 + STAGE_ROWS, :],
                                     stage.at[i % 2, :, 0:src.shape[1]], sems.at[i % 2])

    fetch(0).start()
    for i, (src, dst, scale, r0) in enumerate(jobs):
        if i + 1 < len(jobs):
            fetch(i + 1).start()
        fetch(i).wait()
        vals = stage[i % 2, :, 0:src.shape[1]]
        if scale is not None:
            vals = vals * scale
        dst[r0:r0 + STAGE_ROWS, :] = vals.astype(BF16)


def _array_copies(src_refs, dst_refs, sems):
    return [pltpu.make_async_copy(s, d, sems.at[i]) for i, (s, d) in enumerate(zip(src_refs, dst_refs))]


def _prompt_mixer_kernel(x_ref, meta_ref, gmix_ref, wpool_ref, pscale_ref, lbl_ref, gon_ref, cscale_ref,
                         win_hbm, wa_hbm, wb_hbm, wout_hbm, wup_hbm, wdn_hbm,
                         x1_ref, pool_ref, hgrn_ref,
                         win16_hbm, wa16_hbm, wb16_hbm, wout16_hbm, wup16_hbm, wdn16_hbm,
                         z_sc, h_sc, ya_sc, ubuf, q_sc, k_sc, fg_sc, g_sc, od_sc, dec_sc,
                         w_sc, qb_sc, kb_sc, o_sc, st_sc, st_meta,
                         u_meta, up32, dn32, up16, dn16, cast_sems,
                         win_ref, wa_ref, wb_ref, wout_ref, stage, load_sems, export_sems, *, tile):
    b = pl.program_id(0)
    t = pl.program_id(1)
    nt = pl.num_programs(1)
    lb = _lower_bound(lbl_ref[...])
    gmix = gmix_ref[...]
    mixer_weights = (win_ref, wa_ref, wb_ref, wout_ref)
    exports = (win16_hbm, wa16_hbm, wb16_hbm, wout16_hbm)

    step = b * nt + t
    last_step = pl.num_programs(0) * nt - 1
    in_sems, out_sems = (cast_sems.at[0], cast_sems.at[1]), (cast_sems.at[2], cast_sems.at[3])

    def cast_out(s):
        return _slice_copies(s, (wup16_hbm, wdn16_hbm), (up16, dn16), out_sems, to_hbm=True)

    cast_in = _slice_copies(step, (wup_hbm, wdn_hbm), (up32, dn32), in_sems, to_hbm=False)
    for cp in cast_in:
        cp.start()

    @pl.when((b == 0) & (t == 0))
    def _meta():
        _load_mixer_weights((win_hbm, wa_hbm, wb_hbm, wout_hbm), (cscale_ref[...], None, None, 0.5),
                            mixer_weights, stage, load_sems)
        for cp in _array_copies(mixer_weights, exports, export_sems):
            cp.start()
        hm = _rms(meta_ref[...], gmix).astype(BF16)
        zm = _dot(hm, win_ref[:, 0:O_OG])
        u_meta[...] = zm[:, O_U:O_U + POOL_WIDTH]
        _, fg = _gates(zm[:, O_Q:O_Q + KTOT], zm[:, O_F:O_F + KTOT], lb)
        G = _cumsum_rows(jnp.log2(fg))
        kt = (1.0 - fg) * jnp.exp2(G[N_META - 1:N_META, :] - G)
        v = zm[:, O_V:O_V + HWIDTH]
        for h in range(HEADS):
            sl = slice(h * DK, (h + 1) * DK)
            st_meta[h] = _dot_tn(v[:, sl].astype(BF16), kt[:, sl].astype(BF16))

    @pl.when(t == 0)
    def _init():
        st_sc[...] = st_meta[...]
        ubuf[0:POOL_MAXW, :] = u_meta[...]

    n_chunks = tile // CHUNK
    h_sc[...] = _rms(x_ref[0], gmix).astype(BF16)
    z_sc[:, 0:O_OG] = _dot(h_sc[...], win_ref[:, 0:O_OG])

    u = z_sc[:, O_U:O_U + POOL_WIDTH]
    ubuf[POOL_MAXW:POOL_MAXW + tile, :] = u
    ya_parts = []
    for gi, w in enumerate(POOL_WINDOWS):
        cs = slice(gi * POOL_GROUP, (gi + 1) * POOL_GROUP)
        wsum = ubuf[:, cs]
        span = 1
        while span < w:
            wsum = wsum + pltpu.roll(wsum, span, 0)
            span *= 2
        pooled = wsum[POOL_MAXW:, :] / float(w) - u[:, cs]
        ya_parts.append(pooled.astype(BF16))
    ya_pre = _group_maps(ya_parts, wpool_ref) * pscale_ref[...]
    ya_sc[...] = _dot(ya_pre.astype(BF16), wa_ref[...])
    ubuf[0:POOL_MAXW, :] = ubuf[tile:tile + POOL_MAXW, :]

    def gate_piece(r0, c0):
        z_sc[r0:r0 + GATE_ROWS, c0:c0 + GATE_PIECE] = _dot(
            h_sc[r0:r0 + GATE_ROWS, :], win_ref[:, c0:c0 + GATE_PIECE])
    pending = [functools.partial(gate_piece, r0, c0)
               for c0 in range(O_OG, N_IN, GATE_PIECE) for r0 in range(0, tile, GATE_ROWS)]
    n_pieces = len(pending)
    n_groups = CHUNK // SUB
    n_slots = n_chunks * (n_groups + HEADS)
    slot = [0]

    def interleave():
        slot[0] += 1
        while len(pending) > n_pieces - (-(-slot[0] * n_pieces // n_slots)):
            pending.pop(0)()

    def gates(c):
        _gates_stage(c * CHUNK, c % 2, z_sc, lb, q_sc, k_sc, fg_sc, g_sc, od_sc, dec_sc)

    def group(c, j):
        _group_stage(j, c % 2, q_sc, k_sc, fg_sc, g_sc, w_sc, qb_sc, kb_sc)
        interleave()

    def head(c, hd):
        _head_stage(c * CHUNK, hd, c % 2, z_sc, lvl, w_sc, qb_sc, kb_sc, od_sc, dec_sc, o_sc, st_sc)
        interleave()

    lvl = _level_map(CHUNK // 2)
    gates(0)
    for j in range(n_groups):
        group(0, j)
    for c in range(n_chunks):
        if c + 1 < n_chunks:
            gates(c + 1)
        for hd in range(HEADS):
            head(c, hd)
            if c + 1 < n_chunks:
                for j in range(hd * n_groups // HEADS, (hd + 1) * n_groups // HEADS):
                    group(c + 1, j)
    assert not pending and slot[0] == n_slots

    gon = gon_ref[...]
    o_parts = []
    for hd in range(HEADS):
        sl = slice(hd * DV, (hd + 1) * DV)
        og = z_sc[:, O_OG + hd * DV:O_OG + (hd + 1) * DV]
        o_parts.append(_rms(o_sc[:, sl], gon) * _silu(og))
    yb = _dot(jnp.concatenate(o_parts, axis=-1).astype(BF16), wb_ref[...])
    m = (_twice_sigmoid(z_sc[:, O_GA:O_GA + D_MODEL]) * ya_sc[...]
         + _twice_sigmoid(z_sc[:, O_GB:O_GB + D_MODEL]) * yb)
    x1_ref[0] = x_ref[0] + _dot(m.astype(BF16), wout_ref[...])

    for cp in cast_in:
        cp.wait()

    @pl.when(step > 0)
    def _drain_previous():
        for cp in cast_out(step - 1):
            cp.wait()

    up16[...] = up32[...].astype(BF16)
    dn16[...] = dn32[...].astype(BF16)
    for cp in cast_out(step):
        cp.start()

    @pl.when(step == last_step)
    def _drain_last():
        for cp in cast_out(step):
            cp.wait()
        for cp in _array_copies(mixer_weights, exports, export_sems):
            cp.wait()

    @pl.when(t == nt - 1)
    def _state_out():
        pool_ref[0] = ubuf[1:POOL_MAXW, :]
        for hd in range(HEADS):
            hgrn_ref[0, hd] = st_sc[hd].T


def _const_spec(shape):
    nd = len(shape)
    return pl.BlockSpec(shape, lambda *_: (0,) * nd, pipeline_mode=pl.Buffered(1))


def _prompt_mixer(x, meta, gmix, wpool, pscale, lbl, gon, cscale, win, wa, wb, wout, wup, wdown, *, tile):
    B, L, _ = x.shape
    assert L % tile == 0 and tile % CHUNK == 0 and tile % GATE_ROWS == 0
    nt = L // tile
    n_steps = B * nt
    assert wup.shape[0] % (SUB * n_steps) == 0 and wdown.shape[0] % (SUB * n_steps) == 0
    up_rows, dn_rows = wup.shape[0] // n_steps, wdown.shape[0] // n_steps
    own = (win, wa, wb, wout)
    assert all(w.shape[0] % STAGE_ROWS == 0 and w.shape[1] <= N_IN for w in own)
    kern = functools.partial(_prompt_mixer_kernel, tile=tile)
    return pl.pallas_call(
        kern,
        grid=(B, nt),
        in_specs=[
            pl.BlockSpec((1, tile, D_MODEL), lambda b, t: (b, t, 0)),
            _const_spec((N_META, D_MODEL)),
            _const_spec((1, D_MODEL)),
            _const_spec((len(POOL_WINDOWS) // 2, 2 * POOL_GROUP, 2 * POOL_GROUP)),
            _const_spec((1, POOL_WIDTH)),
            _const_spec((2, KTOT)),
            _const_spec((1, DV)),
            _const_spec((1, N_IN)),
        ] + [pl.BlockSpec(memory_space=pl.ANY)] * 6,
        out_specs=[
            pl.BlockSpec((1, tile, D_MODEL), lambda b, t: (b, t, 0)),
            pl.BlockSpec((1, POOL_BUF, POOL_WIDTH), lambda b, t: (b, 0, 0)),
            pl.BlockSpec((1, HEADS, DK, DV), lambda b, t: (b, 0, 0, 0)),
        ] + [pl.BlockSpec(memory_space=pl.ANY)] * 6,
        out_shape=[
            jax.ShapeDtypeStruct((B, L, D_MODEL), F32),
            jax.ShapeDtypeStruct((B, POOL_BUF, POOL_WIDTH), F32),
            jax.ShapeDtypeStruct((B, HEADS, DK, DV), F32),
        ] + [jax.ShapeDtypeStruct(w.shape, BF16) for w in own + (wup, wdown)],
        scratch_shapes=[
            pltpu.VMEM((tile, N_IN), F32),
            pltpu.VMEM((tile, D_MODEL), BF16),
            pltpu.VMEM((tile, D_MODEL), F32),
            pltpu.VMEM((tile + POOL_MAXW, POOL_WIDTH), F32),
            pltpu.VMEM((CHUNK, KTOT), F32),
            pltpu.VMEM((CHUNK, KTOT), F32),
            pltpu.VMEM((CHUNK, KTOT), F32),
            pltpu.VMEM((CHUNK, KTOT), F32),
            pltpu.VMEM((2, CHUNK, HWIDTH), F32),
            pltpu.VMEM((2, 8, KTOT), F32),
            pltpu.VMEM((2, len(LEVEL_HALVES), CHUNK, KTOT), BF16),
            pltpu.VMEM((2, CHUNK, KTOT), BF16),
            pltpu.VMEM((2, CHUNK, KTOT), BF16),
            pltpu.VMEM((tile, HWIDTH), F32),
            pltpu.VMEM((HEADS, DV, DK), F32),
            pltpu.VMEM((HEADS, DV, DK), F32),
            pltpu.VMEM((N_META, POOL_WIDTH), F32),
            pltpu.VMEM((up_rows, wup.shape[1]), F32),
            pltpu.VMEM((dn_rows, wdown.shape[1]), F32),
            pltpu.VMEM((up_rows, wup.shape[1]), BF16),
            pltpu.VMEM((dn_rows, wdown.shape[1]), BF16),
            pltpu.SemaphoreType.DMA((4,)),
        ] + [pltpu.VMEM(w.shape, BF16) for w in own] + [
            pltpu.VMEM((2, STAGE_ROWS, N_IN), F32),
            pltpu.SemaphoreType.DMA((2,)),
            pltpu.SemaphoreType.DMA((len(own),)),
        ],
        compiler_params=pltpu.CompilerParams(
            dimension_semantics=("arbitrary", "arbitrary"),
            vmem_limit_bytes=VMEM_LIMIT),
        name="prompt_mixer",
    )(x, meta, gmix, wpool, pscale, lbl, gon, cscale, win, wa, wb, wout, wup, wdown)


def _decode_mixer_kernel(x_ref, s_ref, gmix_ref, win_ref, pscale_ref, lbl_ref, gon_ref,
                         poolT_hbm, wpool_hbm, wa_hbm, wb_hbm, wout_hbm,
                         x1_ref, unew_ref, snew_ref,
                         z_sc, o_sc, poolT_ref, wpool_ref, wa_ref, wb_ref, wout_ref, late_sems, *, ts):
    i = pl.program_id(0)
    n = pl.num_programs(0)
    lb = _lower_bound(lbl_ref[...])
    late_src = (poolT_hbm, wpool_hbm, wa_hbm, wb_hbm, wout_hbm)
    late_dst = (poolT_ref, wpool_ref, wa_ref, wb_ref, wout_ref)

    @pl.when(i == 0)
    def _in_proj():
        for cp in _array_copies(late_src, late_dst, late_sems):
            cp.start()
        h = _rms(x_ref[...], gmix_ref[...]).astype(BF16)
        z_sc[...] = _dot(h, win_ref[...])

    r = pl.ds(pl.multiple_of(i * ts, ts), ts)
    q, fg = _gates(z_sc[r, O_Q:O_Q + KTOT], z_sc[r, O_F:O_F + KTOT], lb)
    k = 1.0 - fg
    v = z_sc[r, O_V:O_V + HWIDTH]
    row = lax.broadcasted_iota(jnp.int32, (ts, 1), 0)
    for hd in range(HEADS):
        sl = slice(hd * DK, (hd + 1) * DK)
        fT = fg[:, sl].T
        kT = k[:, sl].T
        qf = (q[:, sl] * fg[:, sl]).astype(BF16)
        qk = jnp.sum(q[:, sl] * k[:, sl], axis=-1, keepdims=True)
        o = qk * v[:, sl]
        for s in range(ts):
            s_old = s_ref[s, hd]
            snew_ref[s, hd] = fT[:, s:s + 1] * s_old + kT[:, s:s + 1] * v[s:s + 1, sl]
            o_row = _dot(qf[s:s + 1, :], s_old.astype(BF16))
            o = o + jnp.where(row == s, o_row, 0.0)
        o_sc[r, sl] = o

    @pl.when(i == n - 1)
    def _out_proj():
        for cp in _array_copies(late_src, late_dst, late_sems):
            cp.wait()
        x = x_ref[...]
        u = z_sc[:, O_U:O_U + POOL_WIDTH]
        unew_ref[...] = u
        ya_parts = []
        for gi, w in enumerate(POOL_WINDOWS):
            cs = slice(gi * POOL_GROUP, (gi + 1) * POOL_GROUP)
            wsum = u[:, cs]
            for j in range(1, w):
                wsum = wsum + poolT_ref[POOL_BUF - j, :, cs]
            pooled = wsum / float(w) - u[:, cs]
            ya_parts.append(pooled.astype(BF16))
        ya_pre = _group_maps(ya_parts, wpool_ref) * pscale_ref[...]
        ya = _dot(ya_pre.astype(BF16), wa_ref[...])
        gon = gon_ref[...]
        o_parts = []
        for hd in range(HEADS):
            sl = slice(hd * DV, (hd + 1) * DV)
            og = z_sc[:, O_OG + hd * DV:O_OG + (hd + 1) * DV]
            o_parts.append(_rms(o_sc[:, sl], gon) * _silu(og))
        yb = _dot(jnp.concatenate(o_parts, axis=-1).astype(BF16), wb_ref[...])
        m = (_twice_sigmoid(z_sc[:, O_GA:O_GA + D_MODEL]) * ya
             + _twice_sigmoid(z_sc[:, O_GB:O_GB + D_MODEL]) * yb)
        x1_ref[...] = x + _dot(m.astype(BF16), wout_ref[...])


def _decode_mixer(x, poolT, state, gmix, win, wpool, pscale, lbl, gon, wa, wb, wout, *, ts):
    n = x.shape[0]
    assert n % ts == 0
    late = (poolT, wpool, wa, wb, wout)
    kern = functools.partial(_decode_mixer_kernel, ts=ts)
    return pl.pallas_call(
        kern,
        grid=(n // ts,),
        in_specs=[
            _const_spec((n, D_MODEL)),
            pl.BlockSpec((ts, HEADS, DK, DV), lambda i: (i, 0, 0, 0)),
            _const_spec((1, D_MODEL)),
            _const_spec((D_MODEL, N_IN)),
            _const_spec((1, POOL_WIDTH)),
            _const_spec((2, KTOT)),
            _const_spec((1, DV)),
        ] + [pl.BlockSpec(memory_space=pl.ANY)] * len(late),
        out_specs=[
            pl.BlockSpec((n, D_MODEL), lambda i: (0, 0)),
            pl.BlockSpec((n, POOL_WIDTH), lambda i: (0, 0)),
            pl.BlockSpec((ts, HEADS, DK, DV), lambda i: (i, 0, 0, 0)),
        ],
        out_shape=[
            jax.ShapeDtypeStruct((n, D_MODEL), F32),
            jax.ShapeDtypeStruct((n, POOL_WIDTH), F32),
            jax.ShapeDtypeStruct((n, HEADS, DK, DV), F32),
        ],
        scratch_shapes=[
            pltpu.VMEM((n, N_IN), F32),
            pltpu.VMEM((n, HWIDTH), F32),
        ] + [pltpu.VMEM(a.shape, a.dtype) for a in late] + [pltpu.SemaphoreType.DMA((len(late),))],
        compiler_params=pltpu.CompilerParams(
            dimension_semantics=("arbitrary",),
            vmem_limit_bytes=VMEM_LIMIT),
        name="decode_mixer",
    )(x, state, gmix, win, pscale, lbl, gon, *late)


def _mlp_kernel(x_ref, xs_ref, gmlp_ref, wup_ref, wdown_ref, gfin_ref, y_ref, ys_ref, *, ff_chunk):
    i = pl.program_id(0)
    last = pl.num_programs(0) - 1

    def rows(src_ref, dst_ref):
        x = src_ref[...]
        h = _rms(x, gmlp_ref[...]).astype(BF16)
        acc = x
        for c in range(D_FF // ff_chunk):
            cs = slice(c * ff_chunk, (c + 1) * ff_chunk)
            a = jnp.maximum(_dot(h, wup_ref[:, cs]), 0.0)
            acc = acc + _dot((a * a).astype(BF16), wdown_ref[cs, :])
        dst_ref[...] = _rms(acc, gfin_ref[...])

    @pl.when(i < last)
    def _prompt_tile():
        rows(x_ref, y_ref)

    @pl.when(i == last)
    def _sample_rows():
        rows(xs_ref, ys_ref)


def _mlp(x, xs, gmlp, wup, wdown, gfin, *, tm, ff_chunk=1024):
    n, ns = x.shape[0], xs.shape[0]
    assert n % tm == 0 and D_FF % ff_chunk == 0
    n_tiles = n // tm
    kern = functools.partial(_mlp_kernel, ff_chunk=ff_chunk)

    def tile(i):
        return jnp.minimum(i, n_tiles - 1), 0

    return pl.pallas_call(
        kern,
        grid=(n_tiles + 1,),
        in_specs=[
            pl.BlockSpec((tm, D_MODEL), tile),
            _const_spec((ns, D_MODEL)),
            _const_spec((1, D_MODEL)),
            _const_spec((D_MODEL, D_FF)),
            _const_spec((D_FF, D_MODEL)),
            _const_spec((1, D_MODEL)),
        ],
        out_specs=[
            pl.BlockSpec((tm, D_MODEL), tile),
            pl.BlockSpec((ns, D_MODEL), lambda i: (0, 0)),
        ],
        out_shape=[
            jax.ShapeDtypeStruct((n, D_MODEL), F32),
            jax.ShapeDtypeStruct((ns, D_MODEL), F32),
        ],
        compiler_params=pltpu.CompilerParams(
            dimension_semantics=("arbitrary",),
            vmem_limit_bytes=VMEM_LIMIT),
        name="channel_mlp",
    )(x, xs, gmlp, wup, wdown, gfin)


def kernel(x_prompt, x_sample, state_pool, state_hgrn, meta_tokens, g_mix, w_in, w_pool, pool_scale,
           hgrn_lb_logits, g_onorm, w_a, w_b, w_out, g_mlp, w_up, w_down, g_final):
    B, L, _ = x_prompt.shape
    NS = x_sample.shape[0]
    assert g_mix.shape[0] == 1, "single-layer trunk"
    gmix = g_mix[0][None, :]
    col = jnp.arange(N_IN)
    gate_col = ((col >= O_Q) & (col < O_V)) | (col >= O_OG)
    cscale = jnp.where(gate_col, 0.5, 1.0).astype(F32)[None, :]
    wp = w_pool[0].astype(BF16).reshape(len(POOL_WINDOWS) // 2, 2, POOL_GROUP, POOL_GROUP)
    zp = jnp.zeros_like(wp[:, 0])
    wpool = jnp.concatenate([jnp.concatenate([wp[:, 0], zp], axis=2),
                             jnp.concatenate([zp, wp[:, 1]], axis=2)], axis=1)
    pscale = pool_scale[0][None, :]
    gon = g_onorm[0][None, :]
    gmlp = g_mlp[0][None, :]
    gfin = g_final[None, :]

    x1_p, pool_p, hgrn_p, win, wa, wb, wout, wup, wdown = _prompt_mixer(
        x_prompt, meta_tokens, gmix, wpool, pscale, hgrn_lb_logits, gon, cscale,
        w_in[0], w_a[0], w_b[0], w_out[0], w_up[0], w_down[0], tile=MIXER_TILE)

    xs = x_sample.reshape(NS, D_MODEL)
    poolT = jnp.swapaxes(state_pool[0], 0, 1)
    x1_s, u_s, hgrn_s = _decode_mixer(
        xs, poolT, state_hgrn[0], gmix, win, wpool, pscale, hgrn_lb_logits, gon, wa, wb, wout,
        ts=DECODE_TILE)

    y_p, y_s = _mlp(x1_p.reshape(B * L, D_MODEL), x1_s, gmlp, wup, wdown, gfin, tm=MLP_TILE)
    pool_s = jnp.concatenate([state_pool[0][:, 1:, :], u_s[:, None, :]], axis=1)

    return (y_p.reshape(B, L, D_MODEL), y_s.reshape(NS, 1, D_MODEL),
            pool_p[None], hgrn_p[None], pool_s[None], hgrn_s[None])
```

```python
import functools

import jax
import jax.numpy as jnp
from jax import lax
from jax.experimental import pallas as pl
from jax.experimental.pallas import tpu as pltpu

D_MODEL = 1024
N_META = 16
POOL_WIDTH = 512
POOL_WINDOWS = (2, 4, 8, 16)
POOL_GROUP = 128
POOL_MAXW = 16
POOL_BUF = 15
HEADS = 4
DK = 128
DV = 128
KTOT = 512
HWIDTH = 512
D_FF = 4096
EPS = 1e-6
N_IN = 4608
O_U, O_Q, O_F, O_V, O_OG, O_GA, O_GB = 0, 512, 1024, 1536, 2048, 2560, 3584

SUB = 16
CHUNK = 256
GATE_PIECE = 256
GATE_ROWS = 512
STAGE_ROWS = 256
LEVEL_HALVES = tuple(1 << i for i in range(CHUNK.bit_length() - 1))

V7X_VMEM_BYTES = 64 * 1024 * 1024
VMEM_LIMIT = V7X_VMEM_BYTES - 8 * 1024 * 1024
MIXER_TILE = 2 * CHUNK
MLP_TILE = 512
DECODE_TILE = SUB
STATE_SLOTS = 3

F32 = jnp.float32
BF16 = jnp.bfloat16
F32_TINY = 1.1754944e-38


def _rms(x, g):
    return x * lax.rsqrt(jnp.mean(x * x, axis=-1, keepdims=True) + EPS) * g


def _twice_sigmoid(hx):
    return jnp.tanh(hx) + 1.0


def _silu(hx):
    return hx * jnp.tanh(hx) + hx


def _dot(a, b):
    return jnp.dot(a, b, preferred_element_type=F32)


def _dot_nt(a, b):
    return lax.dot_general(a, b, (((1,), (1,)), ((), ())), preferred_element_type=F32)


def _dot_tn(a, b):
    return lax.dot_general(a, b, (((0,), (0,)), ((), ())), preferred_element_type=F32)


def _group_maps(pooled_groups, wpool_ref):
    outs = []
    for p in range(len(pooled_groups) // 2):
        pair = jnp.concatenate(pooled_groups[2 * p:2 * p + 2], axis=-1)
        outs.append(_dot(pair, wpool_ref[p]))
    return jnp.concatenate(outs, axis=-1)


def _lower_bound(lb_logits):
    m = jnp.max(lb_logits, axis=0, keepdims=True)
    e = jnp.exp(lb_logits - m)
    return e[0:1, :] / jnp.sum(e, axis=0, keepdims=True)


def _gates(hq, hf, lb):
    q = _silu(hq)
    fg = 0.5 * (1.0 + lb) + (0.5 * (1.0 - lb)) * jnp.tanh(hf)
    return q, jnp.maximum(fg, F32_TINY)


def _cumsum_rows(g):
    n = g.shape[0]
    r = lax.broadcasted_iota(jnp.int32, (n, n), 0)
    c = lax.broadcasted_iota(jnp.int32, (n, n), 1)
    tril = (r >= c).astype(BF16)
    hi = g.astype(BF16)
    r1 = g - hi.astype(F32)
    mid = r1.astype(BF16)
    lo = (r1 - mid.astype(F32)).astype(BF16)
    return _dot(tril, hi) + _dot(tril, mid) + _dot(tril, lo)


def _level_map(n):
    t = lax.broadcasted_iota(jnp.int32, (n, n), 0)
    s = lax.broadcasted_iota(jnp.int32, (n, n), 1)
    x = t ^ s
    lvl = jnp.full((n, n), -1, jnp.int32)
    for li in range(n.bit_length() - 1):
        lvl = jnp.where((x >> li) == 1, li, lvl)
    return jnp.where(t > s, lvl, -1)


def _level_weights(m, r0, Gg, qg, kg, fgg, g_sc, row8, row16):
    if m == 1:
        return jnp.where((row16 & 1) != 0, qg * fgg, kg)
    if m >= SUB:
        blk = (r0 // (2 * m)) * (2 * m)
        ref = g_sc[blk + m - 1:blk + m, :]
        return (jnp.exp2(Gg - ref) * qg) if (r0 & m) else (jnp.exp2(ref - Gg) * kg)
    if m == SUB // 2:
        ref = g_sc[r0 + m - 1:r0 + m, :]
        return jnp.concatenate([jnp.exp2(ref - Gg[:m]) * kg[:m], jnp.exp2(Gg[m:] - ref) * qg[m:]], axis=0)
    halves = []
    for rb in (r0, r0 + 8):
        if m == 4:
            halves.append(jnp.broadcast_to(g_sc[rb + 3:rb + 4, :], (8, KTOT)))
        else:
            halves.append(jnp.where(row8 >= 4, g_sc[rb + 5:rb + 6, :], g_sc[rb + 1:rb + 2, :]))
    ref = jnp.concatenate(halves, axis=0)
    sel = jnp.where((row16 & m) != 0, qg, kg)
    return jnp.exp2(-jnp.abs(Gg - ref)) * sel


def _gates_stage(base, par, z_sc, lb, q_sc, k_sc, fg_sc, g_sc, od_sc, dec_sc):
    rs = slice(base, base + CHUNK)
    q, fg = _gates(z_sc[rs, O_Q:O_Q + KTOT], z_sc[rs, O_F:O_F + KTOT], lb)
    k = 1.0 - fg
    G = _cumsum_rows(jnp.log2(fg))
    q_sc[...] = q
    k_sc[...] = k
    fg_sc[...] = fg
    g_sc[...] = G
    v = z_sc[rs, O_V:O_V + HWIDTH]
    qk = q * k
    for hd in range(HEADS):
        sl = slice(hd * DK, (hd + 1) * DK)
        od_sc[par, :, sl] = jnp.sum(qk[:, sl], axis=-1, keepdims=True) * v[:, sl]
    dec_sc[par] = jnp.broadcast_to(jnp.exp2(G[CHUNK - 1:CHUNK, :]), (8, KTOT))


def _group_stage(j, par, q_sc, k_sc, fg_sc, g_sc, w_sc, qb_sc, kb_sc):
    row8 = lax.broadcasted_iota(jnp.int32, (8, 1), 0)
    row16 = lax.broadcasted_iota(jnp.int32, (SUB, 1), 0)
    r0 = j * SUB
    gs = slice(r0, r0 + SUB)
    Gg, qg, kg, fgg = g_sc[gs, :], q_sc[gs, :], k_sc[gs, :], fg_sc[gs, :]
    g_last = g_sc[CHUNK - 1:CHUNK, :]
    for li, m in enumerate(LEVEL_HALVES):
        w_sc[par, li, gs, :] = _level_weights(m, r0, Gg, qg, kg, fgg, g_sc, row8, row16).astype(BF16)
    qb_sc[par, gs, :] = (qg * jnp.exp2(Gg)).astype(BF16)
    kb_sc[par, gs, :] = (kg * jnp.exp2(g_last - Gg)).astype(BF16)


def _head_stage(base, hd, par, z_sc, lvl, w_sc, qb_sc, kb_sc, od_sc, dec_sc, o_sc, st_sc):
    rs = slice(base, base + CHUNK)
    sl = slice(hd * DK, (hd + 1) * DK)
    half = CHUNK // 2
    a_lo = jnp.zeros((half, half), F32)
    a_hi = jnp.zeros((half, half), F32)
    zero = jnp.zeros((half, DK), BF16)
    for li in range(len(LEVEL_HALVES) - 1):
        w_lo = w_sc[par, li, :half, sl]
        w_hi = w_sc[par, li, half:, sl]
        x = jnp.concatenate([w_lo, w_hi], axis=1)
        y = jnp.concatenate([jnp.concatenate([w_lo.T, zero], axis=1),
                             jnp.concatenate([zero, w_hi.T], axis=1)], axis=0)
        p = _dot(x, y)
        a_lo = jnp.where(lvl == li, p[:, :half], a_lo)
        a_hi = jnp.where(lvl == li, p[:, half:], a_hi)
    top = len(LEVEL_HALVES) - 1
    a_x = _dot_nt(w_sc[par, top, half:, sl], w_sc[par, top, :half, sl])
    A = jnp.concatenate([jnp.concatenate([a_lo, jnp.zeros((half, half), F32)], axis=1),
                         jnp.concatenate([a_x, a_hi], axis=1)], axis=0)
    v_bf = z_sc[rs, O_V + hd * DV:O_V + (hd + 1) * DV].astype(BF16)
    st = st_sc[hd]
    o_sc[rs, sl] = (_dot(A.astype(BF16), v_bf)
                    + _dot_nt(qb_sc[par, :, sl], st.astype(BF16))
                    + od_sc[par, :, sl])
    st_sc[hd] = st * dec_sc[par, 0:1, sl] + _dot_tn(v_bf, kb_sc[par, :, sl])


def _slice_copies(step, hbm_refs, vmem_refs, sems, to_hbm):
    copies = []
    for hbm, vmem, sem in zip(hbm_refs, vmem_refs, sems):
        rows = vmem.shape[0]
        window = hbm.at[pl.ds(pl.multiple_of(step * rows, rows), rows), :]
        copies.append(pltpu.make_async_copy(vmem, window, sem) if to_hbm
                      else pltpu.make_async_copy(window, vmem, sem))
    return copies


def _load_mixer_weights(hbm_refs, scales, vmem_refs, stage, sems):
    jobs = [(src, dst, scale, r0)
            for src, dst, scale in zip(hbm_refs, vmem_refs, scales)
            for r0 in range(0, src.shape[0], STAGE_ROWS)]

    def fetch(i):
        src, _, _, r0 = jobs[i]
        return pltpu.make_async_copy(src.at[r0:r0 + STAGE_ROWS, :],
                                     stage.at[i % 2, :, 0:src.shape[1]], sems.at[i % 2])

    fetch(0).start()
    for i, (src, dst, scale, r0) in enumerate(jobs):
        if i + 1 < len(jobs):
            fetch(i + 1).start()
        fetch(i).wait()
        vals = stage[i % 2, :, 0:src.shape[1]]
        if scale is not None:
            vals = vals * scale
        dst[r0:r0 + STAGE_ROWS, :] = vals.astype(BF16)


def _export_copies(src_refs, dst_refs, sems):
    return [pltpu.make_async_copy(s, d, sems.at[i]) for i, (s, d) in enumerate(zip(src_refs, dst_refs))]


def _prompt_mixer_kernel(x_ref, meta_ref, gmix_ref, wpool_ref, pscale_ref, lbl_ref, gon_ref, cscale_ref,
                         win_hbm, wa_hbm, wb_hbm, wout_hbm, wup_hbm, wdn_hbm,
                         x1_ref, pool_ref, hgrn_ref,
                         win16_hbm, wa16_hbm, wb16_hbm, wout16_hbm, wup16_hbm, wdn16_hbm,
                         z_sc, h_sc, ya_sc, ubuf, q_sc, k_sc, fg_sc, g_sc, od_sc, dec_sc,
                         w_sc, qb_sc, kb_sc, o_sc, st_sc, st_meta,
                         u_meta, up32, dn32, up16, dn16, cast_sems,
                         win_ref, wa_ref, wb_ref, wout_ref, stage, load_sems, export_sems, *, tile):
    b = pl.program_id(0)
    t = pl.program_id(1)
    nt = pl.num_programs(1)
    lb = _lower_bound(lbl_ref[...])
    gmix = gmix_ref[...]
    mixer_weights = (win_ref, wa_ref, wb_ref, wout_ref)
    exports = (win16_hbm, wa16_hbm, wb16_hbm, wout16_hbm)

    step = b * nt + t
    last_step = pl.num_programs(0) * nt - 1
    in_sems, out_sems = (cast_sems.at[0], cast_sems.at[1]), (cast_sems.at[2], cast_sems.at[3])

    def cast_out(s):
        return _slice_copies(s, (wup16_hbm, wdn16_hbm), (up16, dn16), out_sems, to_hbm=True)

    cast_in = _slice_copies(step, (wup_hbm, wdn_hbm), (up32, dn32), in_sems, to_hbm=False)
    for cp in cast_in:
        cp.start()

    @pl.when((b == 0) & (t == 0))
    def _meta():
        _load_mixer_weights((win_hbm, wa_hbm, wb_hbm, wout_hbm), (cscale_ref[...], None, None, 0.5),
                            mixer_weights, stage, load_sems)
        for cp in _export_copies(mixer_weights, exports, export_sems):
            cp.start()
        hm = _rms(meta_ref[...], gmix).astype(BF16)
        zm = _dot(hm, win_ref[:, 0:O_OG])
        u_meta[...] = zm[:, O_U:O_U + POOL_WIDTH]
        _, fg = _gates(zm[:, O_Q:O_Q + KTOT], zm[:, O_F:O_F + KTOT], lb)
        G = _cumsum_rows(jnp.log2(fg))
        kt = (1.0 - fg) * jnp.exp2(G[N_META - 1:N_META, :] - G)
        v = zm[:, O_V:O_V + HWIDTH]
        for h in range(HEADS):
            sl = slice(h * DK, (h + 1) * DK)
            st_meta[h] = _dot_tn(v[:, sl].astype(BF16), kt[:, sl].astype(BF16))

    @pl.when(t == 0)
    def _init():
        st_sc[...] = st_meta[...]
        ubuf[0:POOL_MAXW, :] = u_meta[...]

    n_chunks = tile // CHUNK
    h_sc[...] = _rms(x_ref[0], gmix).astype(BF16)
    z_sc[:, 0:O_OG] = _dot(h_sc[...], win_ref[:, 0:O_OG])

    u = z_sc[:, O_U:O_U + POOL_WIDTH]
    ubuf[POOL_MAXW:POOL_MAXW + tile, :] = u
    ya_parts = []
    for gi, w in enumerate(POOL_WINDOWS):
        cs = slice(gi * POOL_GROUP, (gi + 1) * POOL_GROUP)
        wsum = ubuf[:, cs]
        span = 1
        while span < w:
            wsum = wsum + pltpu.roll(wsum, span, 0)
            span *= 2
        pooled = wsum[POOL_MAXW:, :] / float(w) - u[:, cs]
        ya_parts.append(pooled.astype(BF16))
    ya_pre = _group_maps(ya_parts, wpool_ref) * pscale_ref[...]
    ya_sc[...] = _dot(ya_pre.astype(BF16), wa_ref[...])
    ubuf[0:POOL_MAXW, :] = ubuf[tile:tile + POOL_MAXW, :]

    def gate_piece(r0, c0):
        z_sc[r0:r0 + GATE_ROWS, c0:c0 + GATE_PIECE] = _dot(
            h_sc[r0:r0 + GATE_ROWS, :], win_ref[:, c0:c0 + GATE_PIECE])
    pending = [functools.partial(gate_piece, r0, c0)
               for c0 in range(O_OG, N_IN, GATE_PIECE) for r0 in range(0, tile, GATE_ROWS)]
    n_pieces = len(pending)
    n_groups = CHUNK // SUB
    n_slots = n_chunks * (n_groups + HEADS)
    slot = [0]

    def interleave():
        slot[0] += 1
        while len(pending) > n_pieces - (-(-slot[0] * n_pieces // n_slots)):
            pending.pop(0)()

    def gates(c):
        _gates_stage(c * CHUNK, c % 2, z_sc, lb, q_sc, k_sc, fg_sc, g_sc, od_sc, dec_sc)

    def group(c, j):
        _group_stage(j, c % 2, q_sc, k_sc, fg_sc, g_sc, w_sc, qb_sc, kb_sc)
        interleave()

    def head(c, hd):
        _head_stage(c * CHUNK, hd, c % 2, z_sc, lvl, w_sc, qb_sc, kb_sc, od_sc, dec_sc, o_sc, st_sc)
        interleave()

    lvl = _level_map(CHUNK // 2)
    gates(0)
    for j in range(n_groups):
        group(0, j)
    for c in range(n_chunks):
        if c + 1 < n_chunks:
            gates(c + 1)
        for hd in range(HEADS):
            head(c, hd)
            if c + 1 < n_chunks:
                for j in range(hd * n_groups // HEADS, (hd + 1) * n_groups // HEADS):
                    group(c + 1, j)
    assert not pending and slot[0] == n_slots

    gon = gon_ref[...]
    o_parts = []
    for hd in range(HEADS):
        sl = slice(hd * DV, (hd + 1) * DV)
        og = z_sc[:, O_OG + hd * DV:O_OG + (hd + 1) * DV]
        o_parts.append(_rms(o_sc[:, sl], gon) * _silu(og))
    yb = _dot(jnp.concatenate(o_parts, axis=-1).astype(BF16), wb_ref[...])
    m = (_twice_sigmoid(z_sc[:, O_GA:O_GA + D_MODEL]) * ya_sc[...]
         + _twice_sigmoid(z_sc[:, O_GB:O_GB + D_MODEL]) * yb)
    x1_ref[0] = x_ref[0] + _dot(m.astype(BF16), wout_ref[...])

    for cp in cast_in:
        cp.wait()

    @pl.when(step > 0)
    def _drain_previous():
        for cp in cast_out(step - 1):
            cp.wait()

    up16[...] = up32[...].astype(BF16)
    dn16[...] = dn32[...].astype(BF16)
    for cp in cast_out(step):
        cp.start()

    @pl.when(step == last_step)
    def _drain_last():
        for cp in cast_out(step):
            cp.wait()
        for cp in _export_copies(mixer_weights, exports, export_sems):
            cp.wait()

    @pl.when(t == nt - 1)
    def _state_out():
        pool_ref[0] = ubuf[1:POOL_MAXW, :]
        for hd in range(HEADS):
            hgrn_ref[0, hd] = st_sc[hd].T


def _const_spec(shape):
    nd = len(shape)
    return pl.BlockSpec(shape, lambda *_: (0,) * nd, pipeline_mode=pl.Buffered(1))


def _prompt_mixer(x, meta, gmix, wpool, pscale, lbl, gon, cscale, win, wa, wb, wout, wup, wdown, *, tile):
    B, L, _ = x.shape
    assert L % tile == 0 and tile % CHUNK == 0 and tile % GATE_ROWS == 0
    nt = L // tile
    n_steps = B * nt
    assert wup.shape[0] % (SUB * n_steps) == 0 and wdown.shape[0] % (SUB * n_steps) == 0
    up_rows, dn_rows = wup.shape[0] // n_steps, wdown.shape[0] // n_steps
    own = (win, wa, wb, wout)
    assert all(w.shape[0] % STAGE_ROWS == 0 and w.shape[1] <= N_IN for w in own)
    kern = functools.partial(_prompt_mixer_kernel, tile=tile)
    return pl.pallas_call(
        kern,
        grid=(B, nt),
        in_specs=[
            pl.BlockSpec((1, tile, D_MODEL), lambda b, t: (b, t, 0)),
            _const_spec((N_META, D_MODEL)),
            _const_spec((1, D_MODEL)),
            _const_spec((len(POOL_WINDOWS) // 2, 2 * POOL_GROUP, 2 * POOL_GROUP)),
            _const_spec((1, POOL_WIDTH)),
            _const_spec((2, KTOT)),
            _const_spec((1, DV)),
            _const_spec((1, N_IN)),
        ] + [pl.BlockSpec(memory_space=pl.ANY)] * 6,
        out_specs=[
            pl.BlockSpec((1, tile, D_MODEL), lambda b, t: (b, t, 0)),
            pl.BlockSpec((1, POOL_BUF, POOL_WIDTH), lambda b, t: (b, 0, 0)),
            pl.BlockSpec((1, HEADS, DK, DV), lambda b, t: (b, 0, 0, 0)),
        ] + [pl.BlockSpec(memory_space=pl.ANY)] * 6,
        out_shape=[
            jax.ShapeDtypeStruct((B, L, D_MODEL), F32),
            jax.ShapeDtypeStruct((B, POOL_BUF, POOL_WIDTH), F32),
            jax.ShapeDtypeStruct((B, HEADS, DK, DV), F32),
        ] + [jax.ShapeDtypeStruct(w.shape, BF16) for w in own + (wup, wdown)],
        scratch_shapes=[
            pltpu.VMEM((tile, N_IN), F32),
            pltpu.VMEM((tile, D_MODEL), BF16),
            pltpu.VMEM((tile, D_MODEL), F32),
            pltpu.VMEM((tile + POOL_MAXW, POOL_WIDTH), F32),
            pltpu.VMEM((CHUNK, KTOT), F32),
            pltpu.VMEM((CHUNK, KTOT), F32),
            pltpu.VMEM((CHUNK, KTOT), F32),
            pltpu.VMEM((CHUNK, KTOT), F32),
            pltpu.VMEM((2, CHUNK, HWIDTH), F32),
            pltpu.VMEM((2, 8, KTOT), F32),
            pltpu.VMEM((2, len(LEVEL_HALVES), CHUNK, KTOT), BF16),
            pltpu.VMEM((2, CHUNK, KTOT), BF16),
            pltpu.VMEM((2, CHUNK, KTOT), BF16),
            pltpu.VMEM((tile, HWIDTH), F32),
            pltpu.VMEM((HEADS, DV, DK), F32),
            pltpu.VMEM((HEADS, DV, DK), F32),
            pltpu.VMEM((N_META, POOL_WIDTH), F32),
            pltpu.VMEM((up_rows, wup.shape[1]), F32),
            pltpu.VMEM((dn_rows, wdown.shape[1]), F32),
            pltpu.VMEM((up_rows, wup.shape[1]), BF16),
            pltpu.VMEM((dn_rows, wdown.shape[1]), BF16),
            pltpu.SemaphoreType.DMA((4,)),
        ] + [pltpu.VMEM(w.shape, BF16) for w in own] + [
            pltpu.VMEM((2, STAGE_ROWS, N_IN), F32),
            pltpu.SemaphoreType.DMA((2,)),
            pltpu.SemaphoreType.DMA((len(own),)),
        ],
        compiler_params=pltpu.CompilerParams(
            dimension_semantics=("arbitrary", "arbitrary"),
            vmem_limit_bytes=VMEM_LIMIT),
        name="prompt_mixer",
    )(x, meta, gmix, wpool, pscale, lbl, gon, cscale, win, wa, wb, wout, wup, wdown)


def _decode_mixer_kernel(x_ref, gmix_ref, win_ref, pscale_ref, lbl_ref, gon_ref,
                         s_hbm, poolT_hbm, wpool_hbm, wa_hbm, wb_hbm, wout_hbm,
                         x1_ref, unew_ref, snew_ref,
                         z_sc, o_sc, poolT_ref, wpool_ref, wa_ref, wb_ref, wout_ref, late_sems,
                         s_ring, ring_sems, *, ts):
    i = pl.program_id(0)
    n = pl.num_programs(0)
    lb = _lower_bound(lbl_ref[...])

    def state_fetch(step):
        slot = step % STATE_SLOTS
        return pltpu.make_async_copy(s_hbm.at[pl.ds(pl.multiple_of(step * ts, ts), ts)],
                                     s_ring.at[slot], ring_sems.at[slot])

    @pl.when(i == 0)
    def _prefill():
        for step in range(STATE_SLOTS - 1):
            @pl.when(step < n)
            def _():
                state_fetch(step).start()

    @pl.when(i + STATE_SLOTS - 1 < n)
    def _prefetch():
        state_fetch(i + STATE_SLOTS - 1).start()

    state_fetch(i).wait()
    s_ref = s_ring.at[i % STATE_SLOTS]
    late_src = (poolT_hbm, wpool_hbm, wa_hbm, wb_hbm, wout_hbm)
    late_dst = (poolT_ref, wpool_ref, wa_ref, wb_ref, wout_ref)

    @pl.when(i == 0)
    def _in_proj():
        for cp in _export_copies(late_src, late_dst, late_sems):
            cp.start()
        h = _rms(x_ref[...], gmix_ref[...]).astype(BF16)
        z_sc[...] = _dot(h, win_ref[...])

    r = pl.ds(pl.multiple_of(i * ts, ts), ts)
    q, fg = _gates(z_sc[r, O_Q:O_Q + KTOT], z_sc[r, O_F:O_F + KTOT], lb)
    k = 1.0 - fg
    v = z_sc[r, O_V:O_V + HWIDTH]
    row = lax.broadcasted_iota(jnp.int32, (ts, 1), 0)
    for hd in range(HEADS):
        sl = slice(hd * DK, (hd + 1) * DK)
        fT = fg[:, sl].T
        kT = k[:, sl].T
        qf = (q[:, sl] * fg[:, sl]).astype(BF16)
        qk = jnp.sum(q[:, sl] * k[:, sl], axis=-1, keepdims=True)
        o = qk * v[:, sl]
        for s in range(ts):
            s_old = s_ref[s, hd]
            snew_ref[s, hd] = fT[:, s:s + 1] * s_old + kT[:, s:s + 1] * v[s:s + 1, sl]
            o_row = _dot(qf[s:s + 1, :], s_old.astype(BF16))
            o = o + jnp.where(row == s, o_row, 0.0)
        o_sc[r, sl] = o

    @pl.when(i == n - 1)
    def _out_proj():
        for cp in _export_copies(late_src, late_dst, late_sems):
            cp.wait()
        x = x_ref[...]
        u = z_sc[:, O_U:O_U + POOL_WIDTH]
        unew_ref[...] = u
        ya_parts = []
        for gi, w in enumerate(POOL_WINDOWS):
            cs = slice(gi * POOL_GROUP, (gi + 1) * POOL_GROUP)
            wsum = u[:, cs]
            for j in range(1, w):
                wsum = wsum + poolT_ref[POOL_BUF - j, :, cs]
            pooled = wsum / float(w) - u[:, cs]
            ya_parts.append(pooled.astype(BF16))
        ya_pre = _group_maps(ya_parts, wpool_ref) * pscale_ref[...]
        ya = _dot(ya_pre.astype(BF16), wa_ref[...])
        gon = gon_ref[...]
        o_parts = []
        for hd in range(HEADS):
            sl = slice(hd * DV, (hd + 1) * DV)
            og = z_sc[:, O_OG + hd * DV:O_OG + (hd + 1) * DV]
            o_parts.append(_rms(o_sc[:, sl], gon) * _silu(og))
        yb = _dot(jnp.concatenate(o_parts, axis=-1).astype(BF16), wb_ref[...])
        m = (_twice_sigmoid(z_sc[:, O_GA:O_GA + D_MODEL]) * ya
             + _twice_sigmoid(z_sc[:, O_GB:O_GB + D_MODEL]) * yb)
        x1_ref[...] = x + _dot(m.astype(BF16), wout_ref[...])


def _decode_mixer(x, poolT, state, gmix, win, wpool, pscale, lbl, gon, wa, wb, wout, *, ts):
    n = x.shape[0]
    assert n % ts == 0
    late = (poolT, wpool, wa, wb, wout)
    kern = functools.partial(_decode_mixer_kernel, ts=ts)
    return pl.pallas_call(
        kern,
        grid=(n // ts,),
        in_specs=[
            _const_spec((n, D_MODEL)),
            _const_spec((1, D_MODEL)),
            _const_spec((D_MODEL, N_IN)),
            _const_spec((1, POOL_WIDTH)),
            _const_spec((2, KTOT)),
            _const_spec((1, DV)),
        ] + [pl.BlockSpec(memory_space=pl.ANY)] * (1 + len(late)),
        out_specs=[
            pl.BlockSpec((n, D_MODEL), lambda i: (0, 0)),
            pl.BlockSpec((n, POOL_WIDTH), lambda i: (0, 0)),
            pl.BlockSpec((ts, HEADS, DK, DV), lambda i: (i, 0, 0, 0)),
        ],
        out_shape=[
            jax.ShapeDtypeStruct((n, D_MODEL), F32),
            jax.ShapeDtypeStruct((n, POOL_WIDTH), F32),
            jax.ShapeDtypeStruct((n, HEADS, DK, DV), F32),
        ],
        scratch_shapes=[
            pltpu.VMEM((n, N_IN), F32),
            pltpu.VMEM((n, HWIDTH), F32),
        ] + [pltpu.VMEM(a.shape, a.dtype) for a in late] + [
            pltpu.SemaphoreType.DMA((len(late),)),
            pltpu.VMEM((STATE_SLOTS, ts, HEADS, DK, DV), F32),
            pltpu.SemaphoreType.DMA((STATE_SLOTS,)),
        ],
        compiler_params=pltpu.CompilerParams(
            dimension_semantics=("arbitrary",),
            vmem_limit_bytes=VMEM_LIMIT),
        name="decode_mixer",
    )(x, gmix, win, pscale, lbl, gon, state, *late)


def _mlp_kernel(x_ref, xs_ref, gmlp_ref, wup_ref, wdown_ref, gfin_ref, y_ref, ys_ref, *, ff_chunk):
    i = pl.program_id(0)
    last = pl.num_programs(0) - 1

    def rows(src_ref, dst_ref):
        x = src_ref[...]
        h = _rms(x, gmlp_ref[...]).astype(BF16)
        acc = x
        for c in range(D_FF // ff_chunk):
            cs = slice(c * ff_chunk, (c + 1) * ff_chunk)
            a = jnp.maximum(_dot(h, wup_ref[:, cs]), 0.0)
            acc = acc + _dot((a * a).astype(BF16), wdown_ref[cs, :])
        dst_ref[...] = _rms(acc, gfin_ref[...])

    @pl.when(i < last)
    def _prompt_tile():
        rows(x_ref, y_ref)

    @pl.when(i == last)
    def _sample_rows():
        rows(xs_ref, ys_ref)


def _mlp(x, xs, gmlp, wup, wdown, gfin, *, tm, ff_chunk=1024):
    n, ns = x.shape[0], xs.shape[0]
    assert n % tm == 0 and D_FF % ff_chunk == 0
    n_tiles = n // tm
    kern = functools.partial(_mlp_kernel, ff_chunk=ff_chunk)

    def tile(i):
        return jnp.minimum(i, n_tiles - 1), 0

    return pl.pallas_call(
        kern,
        grid=(n_tiles + 1,),
        in_specs=[
            pl.BlockSpec((tm, D_MODEL), tile),
            _const_spec((ns, D_MODEL)),
            _const_spec((1, D_MODEL)),
            _const_spec((D_MODEL, D_FF)),
            _const_spec((D_FF, D_MODEL)),
            _const_spec((1, D_MODEL)),
        ],
        out_specs=[
            pl.BlockSpec((tm, D_MODEL), tile),
            pl.BlockSpec((ns, D_MODEL), lambda i: (0, 0)),
        ],
        out_shape=[
            jax.ShapeDtypeStruct((n, D_MODEL), F32),
            jax.ShapeDtypeStruct((ns, D_MODEL), F32),
        ],
        compiler_params=pltpu.CompilerParams(
            dimension_semantics=("arbitrary",),
            vmem_limit_bytes=VMEM_LIMIT),
        name="channel_mlp",
    )(x, xs, gmlp, wup, wdown, gfin)


def kernel(x_prompt, x_sample, state_pool, state_hgrn, meta_tokens, g_mix, w_in, w_pool, pool_scale,
           hgrn_lb_logits, g_onorm, w_a, w_b, w_out, g_mlp, w_up, w_down, g_final):
    B, L, _ = x_prompt.shape
    NS = x_sample.shape[0]
    assert g_mix.shape[0] == 1, "single-layer trunk"
    gmix = g_mix[0][None, :]
    col = jnp.arange(N_IN)
    gate_col = ((col >= O_Q) & (col < O_V)) | (col >= O_OG)
    cscale = jnp.where(gate_col, 0.5, 1.0).astype(F32)[None, :]
    wp = w_pool[0].astype(BF16).reshape(len(POOL_WINDOWS) // 2, 2, POOL_GROUP, POOL_GROUP)
    zp = jnp.zeros_like(wp[:, 0])
    wpool = jnp.concatenate([jnp.concatenate([wp[:, 0], zp], axis=2),
                             jnp.concatenate([zp, wp[:, 1]], axis=2)], axis=1)
    pscale = pool_scale[0][None, :]
    gon = g_onorm[0][None, :]
    gmlp = g_mlp[0][None, :]
    gfin = g_final[None, :]

    x1_p, pool_p, hgrn_p, win, wa, wb, wout, wup, wdown = _prompt_mixer(
        x_prompt, meta_tokens, gmix, wpool, pscale, hgrn_lb_logits, gon, cscale,
        w_in[0], w_a[0], w_b[0], w_out[0], w_up[0], w_down[0], tile=MIXER_TILE)

    xs = x_sample.reshape(NS, D_MODEL)
    poolT = jnp.swapaxes(state_pool[0], 0, 1)
    x1_s, u_s, hgrn_s = _decode_mixer(
        xs, poolT, state_hgrn[0], gmix, win, wpool, pscale, hgrn_lb_logits, gon, wa, wb, wout,
        ts=DECODE_TILE)

    y_p, y_s = _mlp(x1_p.reshape(B * L, D_MODEL), x1_s, gmlp, wup, wdown, gfin, tm=MLP_TILE)
    pool_s = jnp.concatenate([state_pool[0][:, 1:, :], u_s[:, None, :]], axis=1)

    return (y_p.reshape(B, L, D_MODEL), y_s.reshape(NS, 1, D_MODEL),
            pool_p[None], hgrn_p[None], pool_s[None], hgrn_s[None])
```

```python
import functools

import jax
import jax.numpy as jnp
from jax import lax
from jax.experimental import pallas as pl
from jax.experimental.pallas import tpu as pltpu

D_MODEL = 1024
N_META = 16
POOL_WIDTH = 512
POOL_WINDOWS = (2, 4, 8, 16)
POOL_GROUP = 128
POOL_MAXW = 16
POOL_BUF = 15
HEADS = 4
DK = 128
DV = 128
KTOT = 512
HWIDTH = 512
D_FF = 4096
EPS = 1e-6
N_IN = 4608
O_U, O_Q, O_F, O_V, O_OG, O_GA, O_GB = 0, 512, 1024, 1536, 2048, 2560, 3584

SUB = 16
CHUNK = 256
GATE_PIECE = 256
GATE_ROWS = 512
STAGE_ROWS = 256
LEVEL_HALVES = tuple(1 << i for i in range(CHUNK.bit_length() - 1))

V7X_VMEM_BYTES = 64 * 1024 * 1024
VMEM_LIMIT = V7X_VMEM_BYTES - 8 * 1024 * 1024
MIXER_TILE = 2 * CHUNK
MLP_TILE = 512
DECODE_TILE = SUB
STATE_SLOTS = 3

F32 = jnp.float32
BF16 = jnp.bfloat16
F32_TINY = 1.1754944e-38


def _rms(x, g):
    return x * lax.rsqrt(jnp.mean(x * x, axis=-1, keepdims=True) + EPS) * g


def _twice_sigmoid(hx):
    return jnp.tanh(hx) + 1.0


def _silu(hx):
    return hx * jnp.tanh(hx) + hx


def _dot(a, b):
    return jnp.dot(a, b, preferred_element_type=F32)


def _dot_nt(a, b):
    return lax.dot_general(a, b, (((1,), (1,)), ((), ())), preferred_element_type=F32)


def _dot_tn(a, b):
    return lax.dot_general(a, b, (((0,), (0,)), ((), ())), preferred_element_type=F32)


def _group_maps(pooled_groups, wpool_ref):
    outs = []
    for p in range(len(pooled_groups) // 2):
        pair = jnp.concatenate(pooled_groups[2 * p:2 * p + 2], axis=-1)
        outs.append(_dot(pair, wpool_ref[p]))
    return jnp.concatenate(outs, axis=-1)


def _lower_bound(lb_logits):
    m = jnp.max(lb_logits, axis=0, keepdims=True)
    e = jnp.exp(lb_logits - m)
    return e[0:1, :] / jnp.sum(e, axis=0, keepdims=True)


def _gates(hq, hf, lb):
    q = _silu(hq)
    fg = 0.5 * (1.0 + lb) + (0.5 * (1.0 - lb)) * jnp.tanh(hf)
    return q, jnp.maximum(fg, F32_TINY)


def _cumsum_rows(g):
    n = g.shape[0]
    r = lax.broadcasted_iota(jnp.int32, (n, n), 0)
    c = lax.broadcasted_iota(jnp.int32, (n, n), 1)
    tril = (r >= c).astype(BF16)
    hi = g.astype(BF16)
    r1 = g - hi.astype(F32)
    mid = r1.astype(BF16)
    lo = (r1 - mid.astype(F32)).astype(BF16)
    return _dot(tril, hi) + _dot(tril, mid) + _dot(tril, lo)


def _level_map(n):
    t = lax.broadcasted_iota(jnp.int32, (n, n), 0)
    s = lax.broadcasted_iota(jnp.int32, (n, n), 1)
    x = t ^ s
    lvl = jnp.full((n, n), -1, jnp.int32)
    for li in range(n.bit_length() - 1):
        lvl = jnp.where((x >> li) == 1, li, lvl)
    return jnp.where(t > s, lvl, -1)


def _level_weights(m, r0, Gg, qg, kg, fgg, g_sc, row8, row16):
    if m == 1:
        return jnp.where((row16 & 1) != 0, qg * fgg, kg)
    if m >= SUB:
        blk = (r0 // (2 * m)) * (2 * m)
        ref = g_sc[blk + m - 1:blk + m, :]
        return (jnp.exp2(Gg - ref) * qg) if (r0 & m) else (jnp.exp2(ref - Gg) * kg)
    if m == SUB // 2:
        ref = g_sc[r0 + m - 1:r0 + m, :]
        return jnp.concatenate([jnp.exp2(ref - Gg[:m]) * kg[:m], jnp.exp2(Gg[m:] - ref) * qg[m:]], axis=0)
    halves = []
    for rb in (r0, r0 + 8):
        if m == 4:
            halves.append(jnp.broadcast_to(g_sc[rb + 3:rb + 4, :], (8, KTOT)))
        else:
            halves.append(jnp.where(row8 >= 4, g_sc[rb + 5:rb + 6, :], g_sc[rb + 1:rb + 2, :]))
    ref = jnp.concatenate(halves, axis=0)
    sel = jnp.where((row16 & m) != 0, qg, kg)
    return jnp.exp2(-jnp.abs(Gg - ref)) * sel


def _gates_stage(base, par, z_sc, lb, q_sc, k_sc, fg_sc, g_sc, od_sc, dec_sc):
    rs = slice(base, base + CHUNK)
    q, fg = _gates(z_sc[rs, O_Q:O_Q + KTOT], z_sc[rs, O_F:O_F + KTOT], lb)
    k = 1.0 - fg
    G = _cumsum_rows(jnp.log2(fg))
    q_sc[...] = q
    k_sc[...] = k
    fg_sc[...] = fg
    g_sc[...] = G
    v = z_sc[rs, O_V:O_V + HWIDTH]
    qk = q * k
    for hd in range(HEADS):
        sl = slice(hd * DK, (hd + 1) * DK)
        od_sc[par, :, sl] = jnp.sum(qk[:, sl], axis=-1, keepdims=True) * v[:, sl]
    dec_sc[par] = jnp.broadcast_to(jnp.exp2(G[CHUNK - 1:CHUNK, :]), (8, KTOT))


def _group_stage(j, par, q_sc, k_sc, fg_sc, g_sc, w_sc, qb_sc, kb_sc):
    row8 = lax.broadcasted_iota(jnp.int32, (8, 1), 0)
    row16 = lax.broadcasted_iota(jnp.int32, (SUB, 1), 0)
    r0 = j * SUB
    gs = slice(r0, r0 + SUB)
    Gg, qg, kg, fgg = g_sc[gs, :], q_sc[gs, :], k_sc[gs, :], fg_sc[gs, :]
    g_last = g_sc[CHUNK - 1:CHUNK, :]
    for li, m in enumerate(LEVEL_HALVES):
        w_sc[par, li, gs, :] = _level_weights(m, r0, Gg, qg, kg, fgg, g_sc, row8, row16).astype(BF16)
    qb_sc[par, gs, :] = (qg * jnp.exp2(Gg)).astype(BF16)
    kb_sc[par, gs, :] = (kg * jnp.exp2(g_last - Gg)).astype(BF16)


def _head_stage(base, hd, par, z_sc, lvl, w_sc, qb_sc, kb_sc, od_sc, dec_sc, o_sc, st_sc):
    rs = slice(base, base + CHUNK)
    sl = slice(hd * DK, (hd + 1) * DK)
    half = CHUNK // 2
    a_lo = jnp.zeros((half, half), F32)
    a_hi = jnp.zeros((half, half), F32)
    zero = jnp.zeros((half, DK), BF16)
    for li in range(len(LEVEL_HALVES) - 1):
        w_lo = w_sc[par, li, :half, sl]
        w_hi = w_sc[par, li, half:, sl]
        x = jnp.concatenate([w_lo, w_hi], axis=1)
        y = jnp.concatenate([jnp.concatenate([w_lo.T, zero], axis=1),
                             jnp.concatenate([zero, w_hi.T], axis=1)], axis=0)
        p = _dot(x, y)
        a_lo = jnp.where(lvl == li, p[:, :half], a_lo)
        a_hi = jnp.where(lvl == li, p[:, half:], a_hi)
    top = len(LEVEL_HALVES) - 1
    a_x = _dot_nt(w_sc[par, top, half:, sl], w_sc[par, top, :half, sl])
    A = jnp.concatenate([jnp.concatenate([a_lo, jnp.zeros((half, half), F32)], axis=1),
                         jnp.concatenate([a_x, a_hi], axis=1)], axis=0)
    v_bf = z_sc[rs, O_V + hd * DV:O_V + (hd + 1) * DV].astype(BF16)
    st = st_sc[hd]
    o_sc[rs, sl] = (_dot(A.astype(BF16), v_bf)
                    + _dot_nt(qb_sc[par, :, sl], st.astype(BF16))
                    + od_sc[par, :, sl])
    st_sc[hd] = st * dec_sc[par, 0:1, sl] + _dot_tn(v_bf, kb_sc[par, :, sl])


def _slice_copies(step, hbm_refs, vmem_refs, sems, to_hbm):
    copies = []
    for hbm, vmem, sem in zip(hbm_refs, vmem_refs, sems):
        rows = vmem.shape[0]
        window = hbm.at[pl.ds(pl.multiple_of(step * rows, rows), rows), :]
        copies.append(pltpu.make_async_copy(vmem, window, sem) if to_hbm
                      else pltpu.make_async_copy(window, vmem, sem))
    return copies


def _load_mixer_weights(hbm_refs, scales, vmem_refs, stage, sems):
    jobs = [(src, dst, scale, r0)
            for src, dst, scale in zip(hbm_refs, vmem_refs, scales)
            for r0 in range(0, src.shape[0], STAGE_ROWS)]

    def fetch(i):
        src, _, _, r0 = jobs[i]
        return pltpu.make_async_copy(src.at[r0:r0 + STAGE_ROWS, :],
                                     stage.at[i % 2, :, 0:src.shape[1]], sems.at[i % 2])

    fetch(0).start()
    for i, (src, dst, scale, r0) in enumerate(jobs):
        if i + 1 < len(jobs):
            fetch(i + 1).start()
        fetch(i).wait()
        vals = stage[i % 2, :, 0:src.shape[1]]
        if scale is not None:
            vals = vals * scale
        dst[r0:r0 + STAGE_ROWS, :] = vals.astype(BF16)


def _export_copies(src_refs, dst_refs, sems):
    return [pltpu.make_async_copy(s, d, sems.at[i]) for i, (s, d) in enumerate(zip(src_refs, dst_refs))]


def _prompt_mixer_kernel(x_ref, meta_ref, gmix_ref, wpool_ref, pscale_ref, lbl_ref, gon_ref, cscale_ref,
                         win_hbm, wa_hbm, wb_hbm, wout_hbm, wup_hbm, wdn_hbm,
                         x1_ref, pool_ref, hgrn_ref,
                         win16_hbm, wa16_hbm, wb16_hbm, wout16_hbm, wup16_hbm, wdn16_hbm,
                         z_sc, h_sc, ya_sc, ubuf, q_sc, k_sc, fg_sc, g_sc, od_sc, dec_sc,
                         w_sc, qb_sc, kb_sc, o_sc, st_sc, st_meta,
                         u_meta, up32, dn32, up16, dn16, cast_sems,
                         win_ref, wa_ref, wb_ref, wout_ref, stage, load_sems, export_sems, *, tile):
    b = pl.program_id(0)
    t = pl.program_id(1)
    nt = pl.num_programs(1)
    lb = _lower_bound(lbl_ref[...])
    gmix = gmix_ref[...]
    mixer_weights = (win_ref, wa_ref, wb_ref, wout_ref)
    exports = (win16_hbm, wa16_hbm, wb16_hbm, wout16_hbm)

    step = b * nt + t
    last_step = pl.num_programs(0) * nt - 1
    in_sems, out_sems = (cast_sems.at[0], cast_sems.at[1]), (cast_sems.at[2], cast_sems.at[3])

    def cast_out(s):
        return _slice_copies(s, (wup16_hbm, wdn16_hbm), (up16, dn16), out_sems, to_hbm=True)

    cast_in = _slice_copies(step, (wup_hbm, wdn_hbm), (up32, dn32), in_sems, to_hbm=False)
    for cp in cast_in:
        cp.start()

    @pl.when((b == 0) & (t == 0))
    def _meta():
        _load_mixer_weights((win_hbm, wa_hbm, wb_hbm, wout_hbm), (cscale_ref[...], None, None, 0.5),
                            mixer_weights, stage, load_sems)
        for cp in _export_copies(mixer_weights, exports, export_sems):
            cp.start()
        hm = _rms(meta_ref[...], gmix).astype(BF16)
        zm = _dot(hm, win_ref[:, 0:O_OG])
        u_meta[...] = zm[:, O_U:O_U + POOL_WIDTH]
        _, fg = _gates(zm[:, O_Q:O_Q + KTOT], zm[:, O_F:O_F + KTOT], lb)
        G = _cumsum_rows(jnp.log2(fg))
        kt = (1.0 - fg) * jnp.exp2(G[N_META - 1:N_META, :] - G)
        v = zm[:, O_V:O_V + HWIDTH]
        for h in range(HEADS):
            sl = slice(h * DK, (h + 1) * DK)
            st_meta[h] = _dot_tn(v[:, sl].astype(BF16), kt[:, sl].astype(BF16))

    @pl.when(t == 0)
    def _init():
        st_sc[...] = st_meta[...]
        ubuf[0:POOL_MAXW, :] = u_meta[...]

    n_chunks = tile // CHUNK
    h_sc[...] = _rms(x_ref[0], gmix).astype(BF16)
    z_sc[:, 0:O_OG] = _dot(h_sc[...], win_ref[:, 0:O_OG])

    u = z_sc[:, O_U:O_U + POOL_WIDTH]
    ubuf[POOL_MAXW:POOL_MAXW + tile, :] = u
    ya_parts = []
    for gi, w in enumerate(POOL_WINDOWS):
        cs = slice(gi * POOL_GROUP, (gi + 1) * POOL_GROUP)
        wsum = ubuf[:, cs]
        span = 1
        while span < w:
            wsum = wsum + pltpu.roll(wsum, span, 0)
            span *= 2
        pooled = wsum[POOL_MAXW:, :] / float(w) - u[:, cs]
        ya_parts.append(pooled.astype(BF16))
    ya_pre = _group_maps(ya_parts, wpool_ref) * pscale_ref[...]
    ya_sc[...] = _dot(ya_pre.astype(BF16), wa_ref[...])
    ubuf[0:POOL_MAXW, :] = ubuf[tile:tile + POOL_MAXW, :]

    def gate_piece(r0, c0):
        z_sc[r0:r0 + GATE_ROWS, c0:c0 + GATE_PIECE] = _dot(
            h_sc[r0:r0 + GATE_ROWS, :], win_ref[:, c0:c0 + GATE_PIECE])
    pending = [functools.partial(gate_piece, r0, c0)
               for c0 in range(O_OG, N_IN, GATE_PIECE) for r0 in range(0, tile, GATE_ROWS)]
    n_pieces = len(pending)
    n_groups = CHUNK // SUB
    n_slots = n_chunks * (n_groups + HEADS)
    slot = [0]

    def interleave():
        slot[0] += 1
        while len(pending) > n_pieces - (-(-slot[0] * n_pieces // n_slots)):
            pending.pop(0)()

    def gates(c):
        _gates_stage(c * CHUNK, c % 2, z_sc, lb, q_sc, k_sc, fg_sc, g_sc, od_sc, dec_sc)

    def group(c, j):
        _group_stage(j, c % 2, q_sc, k_sc, fg_sc, g_sc, w_sc, qb_sc, kb_sc)
        interleave()

    def head(c, hd):
        _head_stage(c * CHUNK, hd, c % 2, z_sc, lvl, w_sc, qb_sc, kb_sc, od_sc, dec_sc, o_sc, st_sc)
        interleave()

    lvl = _level_map(CHUNK // 2)
    gates(0)
    for j in range(n_groups):
        group(0, j)
    for c in range(n_chunks):
        if c + 1 < n_chunks:
            gates(c + 1)
        for hd in range(HEADS):
            head(c, hd)
            if c + 1 < n_chunks:
                for j in range(hd * n_groups // HEADS, (hd + 1) * n_groups // HEADS):
                    group(c + 1, j)
    assert not pending and slot[0] == n_slots

    gon = gon_ref[...]
    o_parts = []
    for hd in range(HEADS):
        sl = slice(hd * DV, (hd + 1) * DV)
        og = z_sc[:, O_OG + hd * DV:O_OG + (hd + 1) * DV]
        o_parts.append(_rms(o_sc[:, sl], gon) * _silu(og))
    yb = _dot(jnp.concatenate(o_parts, axis=-1).astype(BF16), wb_ref[...])
    m = (_twice_sigmoid(z_sc[:, O_GA:O_GA + D_MODEL]) * ya_sc[...]
         + _twice_sigmoid(z_sc[:, O_GB:O_GB + D_MODEL]) * yb)
    x1_ref[0] = x_ref[0] + _dot(m.astype(BF16), wout_ref[...])

    for cp in cast_in:
        cp.wait()

    @pl.when(step > 0)
    def _drain_previous():
        for cp in cast_out(step - 1):
            cp.wait()

    up16[...] = up32[...].astype(BF16)
    dn16[...] = dn32[...].astype(BF16)
    for cp in cast_out(step):
        cp.start()

    @pl.when(step == last_step)
    def _drain_last():
        for cp in cast_out(step):
            cp.wait()
        for cp in _export_copies(mixer_weights, exports, export_sems):
            cp.wait()

    @pl.when(t == nt - 1)
    def _state_out():
        pool_ref[0] = ubuf[1:POOL_MAXW, :]
        for hd in range(HEADS):
            hgrn_ref[0, hd] = st_sc[hd].T


def _const_spec(shape):
    nd = len(shape)
    return pl.BlockSpec(shape, lambda *_: (0,) * nd, pipeline_mode=pl.Buffered(1))


def _prompt_mixer(x, meta, gmix, wpool, pscale, lbl, gon, cscale, win, wa, wb, wout, wup, wdown, *, tile):
    B, L, _ = x.shape
    assert L % tile == 0 and tile % CHUNK == 0 and tile % GATE_ROWS == 0
    nt = L // tile
    n_steps = B * nt
    assert wup.shape[0] % (SUB * n_steps) == 0 and wdown.shape[0] % (SUB * n_steps) == 0
    up_rows, dn_rows = wup.shape[0] // n_steps, wdown.shape[0] // n_steps
    own = (win, wa, wb, wout)
    assert all(w.shape[0] % STAGE_ROWS == 0 and w.shape[1] <= N_IN for w in own)
    kern = functools.partial(_prompt_mixer_kernel, tile=tile)
    return pl.pallas_call(
        kern,
        grid=(B, nt),
        in_specs=[
            pl.BlockSpec((1, tile, D_MODEL), lambda b, t: (b, t, 0)),
            _const_spec((N_META, D_MODEL)),
            _const_spec((1, D_MODEL)),
            _const_spec((len(POOL_WINDOWS) // 2, 2 * POOL_GROUP, 2 * POOL_GROUP)),
            _const_spec((1, POOL_WIDTH)),
            _const_spec((2, KTOT)),
            _const_spec((1, DV)),
            _const_spec((1, N_IN)),
        ] + [pl.BlockSpec(memory_space=pl.ANY)] * 6,
        out_specs=[
            pl.BlockSpec((1, tile, D_MODEL), lambda b, t: (b, t, 0)),
            pl.BlockSpec((1, POOL_BUF, POOL_WIDTH), lambda b, t: (b, 0, 0)),
            pl.BlockSpec((1, HEADS, DK, DV), lambda b, t: (b, 0, 0, 0)),
        ] + [pl.BlockSpec(memory_space=pl.ANY)] * 6,
        out_shape=[
            jax.ShapeDtypeStruct((B, L, D_MODEL), F32),
            jax.ShapeDtypeStruct((B, POOL_BUF, POOL_WIDTH), F32),
            jax.ShapeDtypeStruct((B, HEADS, DK, DV), F32),
        ] + [jax.ShapeDtypeStruct(w.shape, BF16) for w in own + (wup, wdown)],
        scratch_shapes=[
            pltpu.VMEM((tile, N_IN), F32),
            pltpu.VMEM((tile, D_MODEL), BF16),
            pltpu.VMEM((tile, D_MODEL), F32),
            pltpu.VMEM((tile + POOL_MAXW, POOL_WIDTH), F32),
            pltpu.VMEM((CHUNK, KTOT), F32),
            pltpu.VMEM((CHUNK, KTOT), F32),
            pltpu.VMEM((CHUNK, KTOT), F32),
            pltpu.VMEM((CHUNK, KTOT), F32),
            pltpu.VMEM((2, CHUNK, HWIDTH), F32),
            pltpu.VMEM((2, 8, KTOT), F32),
            pltpu.VMEM((2, len(LEVEL_HALVES), CHUNK, KTOT), BF16),
            pltpu.VMEM((2, CHUNK, KTOT), BF16),
            pltpu.VMEM((2, CHUNK, KTOT), BF16),
            pltpu.VMEM((tile, HWIDTH), F32),
            pltpu.VMEM((HEADS, DV, DK), F32),
            pltpu.VMEM((HEADS, DV, DK), F32),
            pltpu.VMEM((N_META, POOL_WIDTH), F32),
            pltpu.VMEM((up_rows, wup.shape[1]), F32),
            pltpu.VMEM((dn_rows, wdown.shape[1]), F32),
            pltpu.VMEM((up_rows, wup.shape[1]), BF16),
            pltpu.VMEM((dn_rows, wdown.shape[1]), BF16),
            pltpu.SemaphoreType.DMA((4,)),
        ] + [pltpu.VMEM(w.shape, BF16) for w in own] + [
            pltpu.VMEM((2, STAGE_ROWS, N_IN), F32),
            pltpu.SemaphoreType.DMA((2,)),
            pltpu.SemaphoreType.DMA((len(own),)),
        ],
        compiler_params=pltpu.CompilerParams(
            dimension_semantics=("arbitrary", "arbitrary"),
            vmem_limit_bytes=VMEM_LIMIT),
        name="prompt_mixer",
    )(x, meta, gmix, wpool, pscale, lbl, gon, cscale, win, wa, wb, wout, wup, wdown)


def _decode_mixer_kernel(x_ref, gmix_ref, pscale_ref, lbl_ref, gon_ref,
                         win_hbm, s_hbm, poolT_hbm, wpool_hbm, wa_hbm, wb_hbm, wout_hbm,
                         x1_ref, unew_ref, snew_ref,
                         z_sc, o_sc, poolT_ref, wpool_ref, wa_ref, wb_ref, wout_ref, late_sems,
                         s_ring, ring_sems, win_ref, win_sem, *, ts):
    i = pl.program_id(0)
    n = pl.num_programs(0)
    lb = _lower_bound(lbl_ref[...])

    def state_fetch(step):
        slot = step % STATE_SLOTS
        return pltpu.make_async_copy(s_hbm.at[pl.ds(pl.multiple_of(step * ts, ts), ts)],
                                     s_ring.at[slot], ring_sems.at[slot])

    late_src = (poolT_hbm, wpool_hbm, wa_hbm, wb_hbm, wout_hbm)
    late_dst = (poolT_ref, wpool_ref, wa_ref, wb_ref, wout_ref)

    @pl.when(i == 0)
    def _first_step():
        win_copy = pltpu.make_async_copy(win_hbm, win_ref, win_sem.at[0])
        win_copy.start()
        for step in range(STATE_SLOTS - 1):
            @pl.when(step < n)
            def _():
                state_fetch(step).start()
        for cp in _export_copies(late_src, late_dst, late_sems):
            cp.start()
        h = _rms(x_ref[...], gmix_ref[...]).astype(BF16)
        win_copy.wait()
        z_sc[...] = _dot(h, win_ref[...])

    @pl.when(i + STATE_SLOTS - 1 < n)
    def _prefetch():
        state_fetch(i + STATE_SLOTS - 1).start()

    state_fetch(i).wait()
    s_ref = s_ring.at[i % STATE_SLOTS]

    r = pl.ds(pl.multiple_of(i * ts, ts), ts)
    q, fg = _gates(z_sc[r, O_Q:O_Q + KTOT], z_sc[r, O_F:O_F + KTOT], lb)
    k = 1.0 - fg
    v = z_sc[r, O_V:O_V + HWIDTH]
    row = lax.broadcasted_iota(jnp.int32, (ts, 1), 0)
    for hd in range(HEADS):
        sl = slice(hd * DK, (hd + 1) * DK)
        fT = fg[:, sl].T
        kT = k[:, sl].T
        qf = (q[:, sl] * fg[:, sl]).astype(BF16)
        qk = jnp.sum(q[:, sl] * k[:, sl], axis=-1, keepdims=True)
        o = qk * v[:, sl]
        for s in range(ts):
            s_old = s_ref[s, hd]
            snew_ref[s, hd] = fT[:, s:s + 1] * s_old + kT[:, s:s + 1] * v[s:s + 1, sl]
            o_row = _dot(qf[s:s + 1, :], s_old.astype(BF16))
            o = o + jnp.where(row == s, o_row, 0.0)
        o_sc[r, sl] = o

    @pl.when(i == n - 1)
    def _out_proj():
        for cp in _export_copies(late_src, late_dst, late_sems):
            cp.wait()
        x = x_ref[...]
        u = z_sc[:, O_U:O_U + POOL_WIDTH]
        unew_ref[...] = u
        ya_parts = []
        for gi, w in enumerate(POOL_WINDOWS):
            cs = slice(gi * POOL_GROUP, (gi + 1) * POOL_GROUP)
            wsum = u[:, cs]
            for j in range(1, w):
                wsum = wsum + poolT_ref[POOL_BUF - j, :, cs]
            pooled = wsum / float(w) - u[:, cs]
            ya_parts.append(pooled.astype(BF16))
        ya_pre = _group_maps(ya_parts, wpool_ref) * pscale_ref[...]
        ya = _dot(ya_pre.astype(BF16), wa_ref[...])
        gon = gon_ref[...]
        o_parts = []
        for hd in range(HEADS):
            sl = slice(hd * DV, (hd + 1) * DV)
            og = z_sc[:, O_OG + hd * DV:O_OG + (hd + 1) * DV]
            o_parts.append(_rms(o_sc[:, sl], gon) * _silu(og))
        yb = _dot(jnp.concatenate(o_parts, axis=-1).astype(BF16), wb_ref[...])
        m = (_twice_sigmoid(z_sc[:, O_GA:O_GA + D_MODEL]) * ya
             + _twice_sigmoid(z_sc[:, O_GB:O_GB + D_MODEL]) * yb)
        x1_ref[...] = x + _dot(m.astype(BF16), wout_ref[...])


def _decode_mixer(x, poolT, state, gmix, win, wpool, pscale, lbl, gon, wa, wb, wout, *, ts):
    n = x.shape[0]
    assert n % ts == 0
    late = (poolT, wpool, wa, wb, wout)
    kern = functools.partial(_decode_mixer_kernel, ts=ts)
    return pl.pallas_call(
        kern,
        grid=(n // ts,),
        in_specs=[
            _const_spec((n, D_MODEL)),
            _const_spec((1, D_MODEL)),
            _const_spec((1, POOL_WIDTH)),
            _const_spec((2, KTOT)),
            _const_spec((1, DV)),
        ] + [pl.BlockSpec(memory_space=pl.ANY)] * (2 + len(late)),
        out_specs=[
            pl.BlockSpec((n, D_MODEL), lambda i: (0, 0)),
            pl.BlockSpec((n, POOL_WIDTH), lambda i: (0, 0)),
            pl.BlockSpec((ts, HEADS, DK, DV), lambda i: (i, 0, 0, 0)),
        ],
        out_shape=[
            jax.ShapeDtypeStruct((n, D_MODEL), F32),
            jax.ShapeDtypeStruct((n, POOL_WIDTH), F32),
            jax.ShapeDtypeStruct((n, HEADS, DK, DV), F32),
        ],
        scratch_shapes=[
            pltpu.VMEM((n, N_IN), F32),
            pltpu.VMEM((n, HWIDTH), F32),
        ] + [pltpu.VMEM(a.shape, a.dtype) for a in late] + [
            pltpu.SemaphoreType.DMA((len(late),)),
            pltpu.VMEM((STATE_SLOTS, ts, HEADS, DK, DV), F32),
            pltpu.SemaphoreType.DMA((STATE_SLOTS,)),
            pltpu.VMEM(win.shape, win.dtype),
            pltpu.SemaphoreType.DMA((1,)),
        ],
        compiler_params=pltpu.CompilerParams(
            dimension_semantics=("arbitrary",),
            vmem_limit_bytes=VMEM_LIMIT),
        name="decode_mixer",
    )(x, gmix, pscale, lbl, gon, win, state, *late)


def _mlp_kernel(x_ref, xs_ref, gmlp_ref, wup_ref, wdown_ref, gfin_ref, y_ref, ys_ref, *, ff_chunk):
    i = pl.program_id(0)
    last = pl.num_programs(0) - 1

    def rows(src_ref, dst_ref):
        x = src_ref[...]
        h = _rms(x, gmlp_ref[...]).astype(BF16)
        acc = x
        for c in range(D_FF // ff_chunk):
            cs = slice(c * ff_chunk, (c + 1) * ff_chunk)
            a = jnp.maximum(_dot(h, wup_ref[:, cs]), 0.0)
            acc = acc + _dot((a * a).astype(BF16), wdown_ref[cs, :])
        dst_ref[...] = _rms(acc, gfin_ref[...])

    @pl.when(i < last)
    def _prompt_tile():
        rows(x_ref, y_ref)

    @pl.when(i == last)
    def _sample_rows():
        rows(xs_ref, ys_ref)


def _mlp(x, xs, gmlp, wup, wdown, gfin, *, tm, ff_chunk=1024):
    n, ns = x.shape[0], xs.shape[0]
    assert n % tm == 0 and D_FF % ff_chunk == 0
    n_tiles = n // tm
    kern = functools.partial(_mlp_kernel, ff_chunk=ff_chunk)

    def tile(i):
        return jnp.minimum(i, n_tiles - 1), 0

    return pl.pallas_call(
        kern,
        grid=(n_tiles + 1,),
        in_specs=[
            pl.BlockSpec((tm, D_MODEL), tile),
            _const_spec((ns, D_MODEL)),
            _const_spec((1, D_MODEL)),
            _const_spec((D_MODEL, D_FF)),
            _const_spec((D_FF, D_MODEL)),
            _const_spec((1, D_MODEL)),
        ],
        out_specs=[
            pl.BlockSpec((tm, D_MODEL), tile),
            pl.BlockSpec((ns, D_MODEL), lambda i: (0, 0)),
        ],
        out_shape=[
            jax.ShapeDtypeStruct((n, D_MODEL), F32),
            jax.ShapeDtypeStruct((ns, D_MODEL), F32),
        ],
        compiler_params=pltpu.CompilerParams(
            dimension_semantics=("arbitrary",),
            vmem_limit_bytes=VMEM_LIMIT),
        name="channel_mlp",
    )(x, xs, gmlp, wup, wdown, gfin)


def kernel(x_prompt, x_sample, state_pool, state_hgrn, meta_tokens, g_mix, w_in, w_pool, pool_scale,
           hgrn_lb_logits, g_onorm, w_a, w_b, w_out, g_mlp, w_up, w_down, g_final):
    B, L, _ = x_prompt.shape
    NS = x_sample.shape[0]
    assert g_mix.shape[0] == 1, "single-layer trunk"
    gmix = g_mix[0][None, :]
    col = jnp.arange(N_IN)
    gate_col = ((col >= O_Q) & (col < O_V)) | (col >= O_OG)
    cscale = jnp.where(gate_col, 0.5, 1.0).astype(F32)[None, :]
    wp = w_pool[0].astype(BF16).reshape(len(POOL_WINDOWS) // 2, 2, POOL_GROUP, POOL_GROUP)
    zp = jnp.zeros_like(wp[:, 0])
    wpool = jnp.concatenate([jnp.concatenate([wp[:, 0], zp], axis=2),
                             jnp.concatenate([zp, wp[:, 1]], axis=2)], axis=1)
    pscale = pool_scale[0][None, :]
    gon = g_onorm[0][None, :]
    gmlp = g_mlp[0][None, :]
    gfin = g_final[None, :]

    x1_p, pool_p, hgrn_p, win, wa, wb, wout, wup, wdown = _prompt_mixer(
        x_prompt, meta_tokens, gmix, wpool, pscale, hgrn_lb_logits, gon, cscale,
        w_in[0], w_a[0], w_b[0], w_out[0], w_up[0], w_down[0], tile=MIXER_TILE)

    xs = x_sample.reshape(NS, D_MODEL)
    poolT = jnp.swapaxes(state_pool[0], 0, 1)
    x1_s, u_s, hgrn_s = _decode_mixer(
        xs, poolT, state_hgrn[0], gmix, win, wpool, pscale, hgrn_lb_logits, gon, wa, wb, wout,
        ts=DECODE_TILE)

    y_p, y_s = _mlp(x1_p.reshape(B * L, D_MODEL), x1_s, gmlp, wup, wdown, gfin, tm=MLP_TILE)
    pool_s = jnp.concatenate([state_pool[0][:, 1:, :], u_s[:, None, :]], axis=1)

    return (y_p.reshape(B, L, D_MODEL), y_s.reshape(NS, 1, D_MODEL),
            pool_p[None], hgrn_p[None], pool_s[None], hgrn_s[None])
```
